```python
import jax, jax.numpy as jnp
from jax import lax
import numpy as np

D_MODEL = 1024
BATCH = 32
SEQ = 2048
DEPTH = 2

D_MIX = D_MODEL
DN_HEADS = 6
DN_DK = 64
DN_DV = 64
DN_CONV = 4
DN_CHUNK = 64
CF_CH = 256
CF_KERNEL = 31
NSA_HEADS = 6
NSA_KV = 2
NSA_HPG = NSA_HEADS // NSA_KV
NSA_DH = 64
CMP_LEN = 32
CMP_STRIDE = 16
CMP_HID = 2 * NSA_DH
SEL_BLOCK = 64
N_SEL = 8
WINDOW = 512
Q_BLOCK = 64
D_FF = -(-8 * D_MODEL // (3 * 256)) * 256

EPS = 1e-6
NEG = -1e30
SEL_BIG = 1e4

DN_QKV = 2 * DN_HEADS * DN_DK + DN_HEADS * DN_DV
IN_SPLITS = (DN_QKV, DN_HEADS * DN_DV, DN_HEADS, DN_HEADS, 2 * CF_CH,
             NSA_HEADS * NSA_DH,
             NSA_KV * NSA_DH, NSA_KV * NSA_DH, NSA_KV * NSA_DH,
             NSA_KV * NSA_DH, NSA_KV * NSA_DH, NSA_KV * NSA_DH,
             3 * NSA_HEADS)
N_IN = DN_QKV + DN_HEADS * DN_DV + 2 * DN_HEADS + 2 * CF_CH + NSA_HEADS * NSA_DH + 6 * NSA_KV * NSA_DH + 3 * NSA_HEADS

kernel_name = "hybrid_deltanet_conformer_nsa_block"


def rms_norm(x, w):
    xf = x.astype(jnp.float32)
    y = xf * lax.rsqrt(jnp.mean(xf * xf, axis=-1, keepdims=True) + EPS)
    return (y * w.astype(jnp.float32)).astype(x.dtype)


def layer_norm(x, w, b):
    xf = x.astype(jnp.float32)
    mu = jnp.mean(xf, axis=-1, keepdims=True)
    var = jnp.mean(jnp.square(xf - mu), axis=-1, keepdims=True)
    y = (xf - mu) * lax.rsqrt(var + EPS) * w.astype(jnp.float32) + b.astype(jnp.float32)
    return y.astype(x.dtype)


def l2norm(t):
    return t * lax.rsqrt(jnp.sum(t * t, axis=-1, keepdims=True) + EPS)


def masked_softmax(s, mask):
    return jax.nn.softmax(jnp.where(mask, s.astype(jnp.float32), NEG), axis=-1)


def split_cols(t, sizes):
    out, start = [], 0
    for s in sizes:
        out.append(t[..., start:start + s])
        start += s
    return out


def causal_depthwise_conv(x, w):
    k, c = w.shape
    return lax.conv_general_dilated(
        x, w[:, None, :].astype(x.dtype), window_strides=(1,), padding=[(k - 1, 0)],
        dimension_numbers=("NWC", "WIO", "NWC"), feature_group_count=c)


def chunked_gated_delta(q, k, v, g, beta):
    bsz, nh, s, dk = q.shape
    dv = v.shape[-1]
    c = DN_CHUNK
    n = s // c
    f32 = jnp.float32
    q = q.reshape(bsz, nh, n, c, dk)
    k = k.reshape(bsz, nh, n, c, dk)
    v = v.reshape(bsz, nh, n, c, dv)
    g = g.reshape(bsz, nh, n, c)
    beta = beta.reshape(bsz, nh, n, c)
    gam = jnp.cumsum(g, axis=-1)
    idx = jnp.arange(c)
    causal = idx[:, None] >= idx[None, :]
    strict = idx[:, None] > idx[None, :]
    decay = jnp.exp(jnp.where(causal, gam[..., :, None] - gam[..., None, :], NEG))
    kk = jnp.einsum("bhnid,bhnjd->bhnij", k, k)
    a_mat = jnp.where(strict, kk * beta[..., :, None] * decay, 0.0) + jnp.eye(c, dtype=f32)
    u = lax.linalg.triangular_solve(a_mat, v * beta[..., None], left_side=True, lower=True, unit_diagonal=True)
    w = lax.linalg.triangular_solve(a_mat, k * (beta * jnp.exp(gam))[..., None], left_side=True, lower=True, unit_diagonal=True)
    qk = jnp.einsum("bhnid,bhnjd->bhnij", q, k) * decay
    q_dec = q * jnp.exp(gam)[..., None]
    k_dec = k * jnp.exp(gam[..., -1:] - gam)[..., None]
    g_tot = jnp.exp(gam[..., -1])
    xs = tuple(jnp.moveaxis(t, 2, 0) for t in (u, w, qk, q_dec, k_dec, g_tot))

    def step(state, inp):
        u_c, w_c, qk_c, qd_c, kd_c, gt_c = inp
        v_new = u_c - jnp.einsum("bhck,bhkv->bhcv", w_c, state)
        o_c = jnp.einsum("bhck,bhkv->bhcv", qd_c, state) + jnp.einsum("bhcs,bhsv->bhcv", qk_c, v_new)
        state = state * gt_c[..., None, None] + jnp.einsum("bhck,bhcv->bhkv", kd_c, v_new)
        return state, o_c

    s0 = jnp.zeros((bsz, nh, dk, dv), f32)
    _, o = lax.scan(step, s0, xs)
    return jnp.moveaxis(o, 0, 2).reshape(bsz, nh, s, dv)


def gated_deltanet(qkv, z, b_raw, a_raw, conv_w, a_log, dt_bias, norm_w):
    bsz, s, _ = qkv.shape
    f32 = jnp.float32
    qkv_c = jax.nn.silu(causal_depthwise_conv(qkv, conv_w))
    q, k, v = split_cols(qkv_c, (DN_HEADS * DN_DK, DN_HEADS * DN_DK, DN_HEADS * DN_DV))

    def heads(t, d):
        return t.reshape(bsz, s, DN_HEADS, d).transpose(0, 2, 1, 3).astype(f32)

    q = l2norm(heads(q, DN_DK)) * DN_DK ** -0.5
    k = l2norm(heads(k, DN_DK))
    v = heads(v, DN_DV)
    beta = jax.nn.sigmoid(b_raw.astype(f32)).transpose(0, 2, 1)
    g = (-jnp.exp(a_log.astype(f32)) * jax.nn.softplus(a_raw.astype(f32) + dt_bias.astype(f32))).transpose(0, 2, 1)
    o = chunked_gated_delta(q, k, v, g, beta).transpose(0, 2, 1, 3)
    o = o * lax.rsqrt(jnp.mean(o * o, axis=-1, keepdims=True) + EPS) * norm_w.astype(f32)
    o = o * jax.nn.silu(z.reshape(bsz, s, DN_HEADS, DN_DV).astype(f32))
    return o.reshape(bsz, s, DN_HEADS * DN_DV).astype(qkv.dtype)


def conformer_conv(u, dw_w, dw_b, ln_w, ln_b):
    val, gate = jnp.split(u, 2, axis=-1)
    h = val * jax.nn.sigmoid(gate)
    h = causal_depthwise_conv(h, dw_w) + dw_b
    h = layer_norm(h, ln_w, ln_b)
    return jax.nn.silu(h)


def nsa_mixer(q, k_cmp, v_cmp, k_slc, v_slc, k_win, v_win, gate_raw,
              k_pos, v_pos, k_w1, k_w2, v_w1, v_w2):
    bsz, s, _ = q.shape
    dt = q.dtype

    def kv_heads(t):
        return t.reshape(bsz, s, NSA_KV, NSA_DH).transpose(0, 2, 1, 3)

    qh = q.reshape(bsz, s, NSA_KV, NSA_HPG, NSA_DH).transpose(0, 2, 3, 1, 4) * NSA_DH ** -0.5
    gates = jax.nn.sigmoid(gate_raw.astype(jnp.float32)).astype(dt)
    gates = gates.reshape(bsz, s, 3, NSA_KV, NSA_HPG).transpose(2, 0, 3, 4, 1)

    n_cmp = (s - CMP_LEN) // CMP_STRIDE + 1
    win_idx = np.arange(n_cmp)[:, None] * CMP_STRIDE + np.arange(CMP_LEN)[None, :]

    def compress(t, pos, w1, w2):
        blocks = kv_heads(t)[:, :, win_idx] + pos
        blocks = blocks.reshape(bsz, NSA_KV, n_cmp, CMP_LEN * NSA_DH)
        return jax.nn.silu(blocks @ w1) @ w2

    kc = compress(k_cmp, k_pos, k_w1, k_w2)
    vc = compress(v_cmp, v_pos, v_w1, v_w2)
    cmp_start = np.arange(n_cmp) * CMP_STRIDE
    cmp_end = jnp.asarray(cmp_start + CMP_LEN - 1)

    n_blk = s // SEL_BLOCK
    n_sel = min(N_SEL, n_blk)
    blk_start = np.arange(n_blk) * SEL_BLOCK
    overlap = jnp.asarray(((cmp_start[:, None] < blk_start[None, :] + SEL_BLOCK)
                           & (cmp_start[:, None] + CMP_LEN > blk_start[None, :])).astype(np.float32))
    ks = kv_heads(k_slc).reshape(bsz, NSA_KV, n_blk, SEL_BLOCK, NSA_DH)
    vs = kv_heads(v_slc).reshape(bsz, NSA_KV, n_blk, SEL_BLOCK, NSA_DH)

    kw = jnp.pad(kv_heads(k_win), ((0, 0), (0, 0), (WINDOW, 0), (0, 0)))
    vw = jnp.pad(kv_heads(v_win), ((0, 0), (0, 0), (WINDOW, 0), (0, 0)))

    bi = jnp.arange(bsz)[:, None, None, None]
    gi = jnp.arange(NSA_KV)[None, :, None, None]
    blk_ids = jnp.arange(n_blk)

    def block_fn(c):
        start = c * Q_BLOCK
        t = start + jnp.arange(Q_BLOCK)
        qc = lax.dynamic_slice_in_dim(qh, start, Q_BLOCK, axis=3)
        gc = lax.dynamic_slice_in_dim(gates, start, Q_BLOCK, axis=4)
        m_c = cmp_end[None, :] <= t[:, None]
        p_c = masked_softmax(jnp.einsum("bghqd,bgnd->bghqn", qc, kc), m_c) * m_c
        o_c = jnp.einsum("bghqn,bgnd->bghqd", p_c.astype(dt), vc)
        imp = jnp.einsum("bghqn,nj->bgqj", p_c, overlap)
        cur = t[:, None] // SEL_BLOCK
        elig = blk_ids[None, :] <= cur
        forced = (blk_ids[None, :] == 0) | (blk_ids[None, :] == cur) | (blk_ids[None, :] == cur - 1)
        score = jnp.where(elig, imp + jnp.where(forced, SEL_BIG, 0.0), -SEL_BIG)
        _, sel = lax.top_k(score, n_sel)
        k_sel = ks[bi, gi, sel]
        v_sel = vs[bi, gi, sel].reshape(bsz, NSA_KV, Q_BLOCK, n_sel * SEL_BLOCK, NSA_DH)
        kpos = (sel[..., None] * SEL_BLOCK + jnp.arange(SEL_BLOCK)).reshape(bsz, NSA_KV, 1, Q_BLOCK, n_sel * SEL_BLOCK)
        m_s = kpos <= t[:, None]
        s_s = jnp.einsum("bghqd,bgqnld->bghqnl", qc, k_sel).reshape(bsz, NSA_KV, NSA_HPG, Q_BLOCK, n_sel * SEL_BLOCK)
        p_s = masked_softmax(s_s, m_s)
        o_s = jnp.einsum("bghqk,bgqkd->bghqd", p_s.astype(dt), v_sel)
        kwc = lax.dynamic_slice_in_dim(kw, start, WINDOW + Q_BLOCK, axis=2)
        vwc = lax.dynamic_slice_in_dim(vw, start, WINDOW + Q_BLOCK, axis=2)
        kpos_w = start - WINDOW + jnp.arange(WINDOW + Q_BLOCK)
        m_w = (kpos_w[None, :] <= t[:, None]) & (kpos_w[None, :] > t[:, None] - WINDOW) & (kpos_w[None, :] >= 0)
        p_w = masked_softmax(jnp.einsum("bghqd,bgkd->bghqk", qc, kwc), m_w)
        o_w = jnp.einsum("bghqk,bgkd->bghqd", p_w.astype(dt), vwc)
        return gc[0][..., None] * o_c + gc[1][..., None] * o_s + gc[2][..., None] * o_w

    out = lax.map(block_fn, jnp.arange(s // Q_BLOCK))
    return out.transpose(1, 0, 4, 2, 3, 5).reshape(bsz, s, NSA_HEADS * NSA_DH)


def setup_inputs(seed: int = 0) -> dict:
    key = jax.random.key(seed)
    ks = jax.random.split(key, 24)
    f32 = jnp.float32
    nrm = lambda k, shape, scale: scale * jax.random.normal(k, shape, f32)
    dt_init = jnp.exp(jax.random.uniform(ks[5], (DEPTH, DN_HEADS), f32, np.log(1e-3), np.log(1e-1)))
    return {
        "x": nrm(ks[0], (BATCH, SEQ, D_MODEL), 1.0),
        "norm_w": 1.0 + nrm(ks[1], (DEPTH, 4, D_MODEL), 0.05),
        "w_in": nrm(ks[2], (DEPTH, D_MODEL, N_IN), D_MODEL ** -0.5),
        "dn_conv_w": nrm(ks[3], (DEPTH, DN_CONV, DN_QKV), DN_CONV ** -0.5),
        "dn_a_log": jnp.log(jax.random.uniform(ks[4], (DEPTH, DN_HEADS), f32, 1.0, 16.0)),
        "dn_dt_bias": dt_init + jnp.log(-jnp.expm1(-dt_init)),
        "dn_norm_w": 1.0 + nrm(ks[6], (DEPTH, DN_DV), 0.05),
        "cf_dw_w": nrm(ks[7], (DEPTH, CF_KERNEL, CF_CH), CF_KERNEL ** -0.5),
        "cf_dw_b": nrm(ks[8], (DEPTH, CF_CH), 0.02),
        "cf_ln_w": 1.0 + nrm(ks[9], (DEPTH, CF_CH), 0.05),
        "cf_ln_b": nrm(ks[10], (DEPTH, CF_CH), 0.02),
        "nsa_k_pos": nrm(ks[11], (DEPTH, CMP_LEN, NSA_DH), 0.1),
        "nsa_v_pos": nrm(ks[12], (DEPTH, CMP_LEN, NSA_DH), 0.1),
        "nsa_k_w1": nrm(ks[13], (DEPTH, CMP_LEN * NSA_DH, CMP_HID), (CMP_LEN * NSA_DH) ** -0.5),
        "nsa_k_w2": nrm(ks[14], (DEPTH, CMP_HID, NSA_DH), CMP_HID ** -0.5),
        "nsa_v_w1": nrm(ks[15], (DEPTH, CMP_LEN * NSA_DH, CMP_HID), (CMP_LEN * NSA_DH) ** -0.5),
        "nsa_v_w2": nrm(ks[16], (DEPTH, CMP_HID, NSA_DH), CMP_HID ** -0.5),
        "w_out": nrm(ks[17], (DEPTH, D_MIX, D_MODEL), D_MIX ** -0.5),
        "ffn_w_gate": nrm(ks[18], (DEPTH, D_MODEL, D_FF), D_MODEL ** -0.5),
        "ffn_w_up": nrm(ks[19], (DEPTH, D_MODEL, D_FF), D_MODEL ** -0.5),
        "ffn_w_down": nrm(ks[20], (DEPTH, D_FF, D_MODEL), D_FF ** -0.5),
    }


def reference(x, norm_w, w_in, dn_conv_w, dn_a_log, dn_dt_bias, dn_norm_w,
              cf_dw_w, cf_dw_b, cf_ln_w, cf_ln_b,
              nsa_k_pos, nsa_v_pos, nsa_k_w1, nsa_k_w2, nsa_v_w1, nsa_v_w2,
              w_out, ffn_w_gate, ffn_w_up, ffn_w_down):
    for l in range(DEPTH):
        h = rms_norm(x, norm_w[l, 0])
        proj = h @ w_in[l]
        (dn_qkv, dn_z, dn_b, dn_a, cf_u, n_q, n_kc, n_vc, n_ks, n_vs,
         n_kw, n_vw, n_gate) = split_cols(proj, IN_SPLITS)
        o_dn = gated_deltanet(dn_qkv, dn_z, dn_b, dn_a, dn_conv_w[l], dn_a_log[l], dn_dt_bias[l], dn_norm_w[l])
        o_cf = conformer_conv(cf_u, cf_dw_w[l], cf_dw_b[l], cf_ln_w[l], cf_ln_b[l])
        o_nsa = nsa_mixer(n_q, n_kc, n_vc, n_ks, n_vs, n_kw, n_vw, n_gate,
                          nsa_k_pos[l], nsa_v_pos[l], nsa_k_w1[l], nsa_k_w2[l], nsa_v_w1[l], nsa_v_w2[l])
        mix = jnp.concatenate([o_dn, o_cf, o_nsa], axis=-1) @ w_out[l]
        x = x + rms_norm(mix, norm_w[l, 1])
        h = rms_norm(x, norm_w[l, 2])
        f = (jax.nn.silu(h @ ffn_w_gate[l]) * (h @ ffn_w_up[l])) @ ffn_w_down[l]
        x = x + rms_norm(f, norm_w[l, 3])
    return x
```

```python
import functools

import jax
import jax.numpy as jnp
from jax import lax
from jax.experimental import pallas as pl
from jax.experimental.pallas import tpu as pltpu

F32 = jnp.float32
BF16 = jnp.bfloat16
HIGHEST = lax.Precision.HIGHEST

DN_HEADS = 6
DN_DK = 64
DN_DV = 64
DN_CONV = 4
DN_CHUNK = 64
CF_CH = 256
CF_KERNEL = 31
NSA_HEADS = 6
NSA_KV = 2
NSA_HPG = NSA_HEADS // NSA_KV
NSA_DH = 64
CMP_LEN = 32
CMP_STRIDE = 16
CMP_HID = 2 * NSA_DH
SEL_BLOCK = 64
N_SEL = 8
WINDOW = 512
EPS = 1e-6
NEG = -1e30
SEL_BIG = 1e4

LANES = 128
DN_QK = DN_HEADS * DN_DK
DN_QKV = 2 * DN_QK + DN_HEADS * DN_DV
SUPER = 256
TQ = 256
VMEM_LIMIT = 56 * 1024 * 1024

SEC_WIDTHS = (DN_QKV, DN_HEADS * DN_DV, LANES, 2 * CF_CH, NSA_HEADS * LANES,
              LANES, LANES, LANES, LANES, LANES, LANES)
LANE_BETA = 0
LANE_A = DN_HEADS
LANE_GATE = 2 * DN_HEADS


def _dot(a, b, precision=None):
    return jnp.dot(a, b, preferred_element_type=F32, precision=precision)


def _dot_nt(a, b, precision=None):
    return lax.dot_general(a, b, (((1,), (1,)), ((), ())), preferred_element_type=F32, precision=precision)


def _dot_tn(a, b, precision=None):
    return lax.dot_general(a, b, (((0,), (0,)), ((), ())), preferred_element_type=F32, precision=precision)


def _sigmoid(x):
    return 1.0 / (1.0 + jnp.exp(-x))


def _silu(x):
    return x * _sigmoid(x)


def _rms(x, w):
    return x * lax.rsqrt(jnp.mean(x * x, axis=-1, keepdims=True) + EPS) * w


def _in_proj_kernel(x_ref, nw_ref, w_ref, *out_refs):
    h = _rms(x_ref[...], nw_ref[...]).astype(BF16)
    off = 0
    for o_ref, wd in zip(out_refs, SEC_WIDTHS):
        o_ref[...] = _dot(h, w_ref[:, off:off + wd])
        off += wd


def _in_proj(x2, nw, w_perm, tm):
    m, d = x2.shape
    n = w_perm.shape[1]
    return pl.pallas_call(
        _in_proj_kernel,
        grid=(m // tm,),
        in_specs=[pl.BlockSpec((tm, d), lambda i: (i, 0)),
                  pl.BlockSpec((1, d), lambda i: (0, 0)),
                  pl.BlockSpec((d, n), lambda i: (0, 0))],
        out_specs=[pl.BlockSpec((tm, wd), lambda i: (i, 0)) for wd in SEC_WIDTHS],
        out_shape=[jax.ShapeDtypeStruct((m, wd), F32) for wd in SEC_WIDTHS],
        compiler_params=pltpu.CompilerParams(dimension_semantics=("parallel",), vmem_limit_bytes=VMEM_LIMIT),
        name="in_proj",
    )(x2, nw, w_perm)


def _mix_ffn_kernel(x_ref, odn_ref, ocf_ref, onsa_ref, wodn_ref, wocf_ref, wonsa_ref, nw_ref,
                    wg_ref, wu_ref, wd_ref, out_ref):
    mix = (_dot(odn_ref[...], wodn_ref[...]) + _dot(ocf_ref[...], wocf_ref[...])
           + _dot(onsa_ref[...], wonsa_ref[...]))
    x1 = x_ref[...] + _rms(mix, nw_ref[1:2, :])
    h = _rms(x1, nw_ref[2:3, :]).astype(BF16)
    g = _dot(h, wg_ref[...])
    u = _dot(h, wu_ref[...])
    a = (_silu(g) * u).astype(BF16)
    f = _dot(a, wd_ref[...])
    out_ref[...] = x1 + _rms(f, nw_ref[3:4, :])


def _mix_ffn(x2, odn, ocf, onsa, wodn, wocf, wonsa, nw, wg, wu, wd, tm):
    m, d = x2.shape
    dff = wg.shape[1]
    const = lambda i: (0, 0)
    row = lambda i: (i, 0)
    return pl.pallas_call(
        _mix_ffn_kernel,
        grid=(m // tm,),
        in_specs=[pl.BlockSpec((tm, d), row),
                  pl.BlockSpec((tm, odn.shape[1]), row),
                  pl.BlockSpec((tm, ocf.shape[1]), row),
                  pl.BlockSpec((tm, onsa.shape[1]), row),
                  pl.BlockSpec(wodn.shape, const),
                  pl.BlockSpec(wocf.shape, const),
                  pl.BlockSpec(wonsa.shape, const),
                  pl.BlockSpec(nw.shape, const),
                  pl.BlockSpec((d, dff), const),
                  pl.BlockSpec((d, dff), const),
                  pl.BlockSpec((dff, d), const)],
        out_specs=pl.BlockSpec((tm, d), row),
        out_shape=jax.ShapeDtypeStruct((m, d), F32),
        compiler_params=pltpu.CompilerParams(dimension_semantics=("parallel",), vmem_limit_bytes=VMEM_LIMIT),
        name="mix_ffn",
    )(x2, odn, ocf, onsa, wodn, wocf, wonsa, nw, wg, wu, wd)


CF_ROWS = 64
CF_PAD = 32


def _conformer_kernel(u_ref, dw_ref, db_ref, lnw_ref, lnb_ref, out_ref, hbuf):
    ts = u_ref.shape[1]

    @pl.when(pl.program_id(1) == 0)
    def _():
        hbuf[0:CF_PAD, :] = jnp.zeros((CF_PAD, CF_CH), F32)

    u = u_ref[0]
    hbuf[CF_PAD:CF_PAD + ts, :] = u[:, :CF_CH] * _sigmoid(u[:, CF_CH:])
    first = CF_PAD - (CF_KERNEL - 1)
    for r in range(ts // CF_ROWS):
        acc = jnp.zeros((CF_ROWS, CF_CH), F32) + db_ref[...]
        for j in range(CF_KERNEL):
            acc = acc + dw_ref[j:j + 1, :] * hbuf[pl.ds(r * CF_ROWS + first + j, CF_ROWS), :]
        mu = jnp.mean(acc, axis=-1, keepdims=True)
        cen = acc - mu
        var = jnp.mean(cen * cen, axis=-1, keepdims=True)
        y = cen * lax.rsqrt(var + EPS) * lnw_ref[...] + lnb_ref[...]
        out_ref[0, r * CF_ROWS:(r + 1) * CF_ROWS, :] = _silu(y).astype(out_ref.dtype)
    hbuf[0:CF_PAD, :] = hbuf[ts:ts + CF_PAD, :]


def _conformer(cf_u, dw, db, lnw, lnb, ts):
    b, s, _ = cf_u.shape
    const = lambda i, t: (0, 0)
    return pl.pallas_call(
        _conformer_kernel,
        grid=(b, s // ts),
        in_specs=[pl.BlockSpec((1, ts, 2 * CF_CH), lambda i, t: (i, t, 0)),
                  pl.BlockSpec((CF_KERNEL, CF_CH), const),
                  pl.BlockSpec((1, CF_CH), const),
                  pl.BlockSpec((1, CF_CH), const),
                  pl.BlockSpec((1, CF_CH), const)],
        out_specs=pl.BlockSpec((1, ts, CF_CH), lambda i, t: (i, t, 0)),
        out_shape=jax.ShapeDtypeStruct((b, s, CF_CH), BF16),
        scratch_shapes=[pltpu.VMEM((ts + CF_PAD, CF_CH), F32)],
        compiler_params=pltpu.CompilerParams(dimension_semantics=("parallel", "arbitrary")),
        name="conformer",
    )(cf_u, dw, db, lnw, lnb)


DN_PAD = 8


def _deltanet_kernel(qkv_ref, z_ref, sm_ref, cw_ref, alog_ref, dtb_ref, nw_ref, out_ref, xbuf, state):
    n = SUPER

    @pl.when(pl.program_id(1) == 0)
    def _():
        xbuf[0:DN_PAD, :] = jnp.zeros((DN_PAD, DN_QKV), F32)
        state[...] = jnp.zeros(state.shape, F32)

    xbuf[DN_PAD:DN_PAD + n, :] = qkv_ref[0]
    first = DN_PAD - (DN_CONV - 1)
    y = cw_ref[0:1, :] * xbuf[pl.ds(first, n), :]
    for j in range(1, DN_CONV):
        y = y + cw_ref[j:j + 1, :] * xbuf[pl.ds(first + j, n), :]
    xbuf[0:DN_PAD, :] = xbuf[n:n + DN_PAD, :]
    y = _silu(y)
    q_all = y[:, 0:DN_QK]
    k_all = y[:, DN_QK:2 * DN_QK]
    v_all = y[:, 2 * DN_QK:]

    hr = lax.broadcasted_iota(jnp.int32, (DN_QK, DN_QK), 0) // DN_DK
    hc = lax.broadcasted_iota(jnp.int32, (DN_QK, DN_QK), 1) // DN_DK
    head_ones = (hr == hc).astype(F32)
    q_n = q_all * lax.rsqrt(_dot(q_all * q_all, head_ones, HIGHEST) + EPS) * (DN_DK ** -0.5)
    k_n = k_all * lax.rsqrt(_dot(k_all * k_all, head_ones, HIGHEST) + EPS)

    sm = sm_ref[0]
    beta_all = _sigmoid(sm)
    sp_in = sm + dtb_ref[...]
    softplus = jnp.maximum(sp_in, 0.0) + jnp.log1p(jnp.exp(-jnp.abs(sp_in)))
    g_all = -jnp.exp(alog_ref[...]) * softplus
    ri = lax.broadcasted_iota(jnp.int32, (n, n), 0)
    ci = lax.broadcasted_iota(jnp.int32, (n, n), 1)
    same_chunk = (ri // DN_CHUNK) == (ci // DN_CHUNK)
    causal = same_chunk & (ri >= ci)
    strict = same_chunk & (ri > ci)
    gam = _dot(causal.astype(F32), g_all, HIGHEST)
    glast = _dot(same_chunk.astype(F32), g_all, HIGHEST)
    gam_t = gam.T
    e_gam = jnp.exp(gam)
    e_rest = jnp.exp(glast - gam)
    e_tot = jnp.exp(glast)

    outs = []
    for h in range(DN_HEADS):
        sl = slice(h * DN_DK, (h + 1) * DN_DK)
        lg = LANE_A + h
        q = q_n[:, sl]
        k = k_n[:, sl]
        v = v_all[:, sl]
        bcol = beta_all[:, LANE_BETA + h:LANE_BETA + h + 1]
        gcol = gam[:, lg:lg + 1]
        egcol = e_gam[:, lg:lg + 1]
        ercol = e_rest[:, lg:lg + 1]
        etcol = e_tot[:, lg:lg + 1]
        grow = gam_t[lg:lg + 1, :]
        kb = k.astype(BF16)
        kk = _dot_nt(kb, kb)
        qk = _dot_nt(q.astype(BF16), kb)
        dec = jnp.exp(jnp.where(causal, gcol - grow, NEG))
        lmat = jnp.where(strict, kk * bcol * dec, 0.0)
        qkd = qk * dec
        ps = -lmat
        lb = lmat.astype(BF16)
        mp = _dot(lb, lb)
        nfac = (DN_CHUNK - 1).bit_length() - 1
        for s in range(nfac):
            mb = mp.astype(BF16)
            ps = ps + mp + _dot(ps.astype(BF16), mb)
            if s < nfac - 1:
                mp = _dot(mb, mb)
        rhs = jnp.concatenate([v * bcol, k * (bcol * egcol)], axis=1)
        x = rhs + _dot(ps.astype(BF16), rhs.astype(BF16))
        qx = _dot(qkd.astype(BF16), x.astype(BF16))
        u = x[:, :DN_DV]
        w = x[:, DN_DV:]
        qeff = q * egcol - qx[:, DN_DV:]
        o_intra = qx[:, :DN_DV]
        kd = k * ercol
        st = state[h]
        o_chunks = []
        for c in range(n // DN_CHUNK):
            rs = slice(c * DN_CHUNK, (c + 1) * DN_CHUNK)
            lhs = jnp.concatenate([w[rs], qeff[rs]], axis=0).astype(BF16)
            ws = _dot(lhs, st.astype(BF16))
            vnew = u[rs] - ws[:DN_CHUNK]
            o_chunks.append(ws[DN_CHUNK:] + o_intra[rs])
            st = st * etcol[rs] + _dot_tn(kd[rs].astype(BF16), vnew.astype(BF16))
        state[h] = st
        outs.append(jnp.concatenate(o_chunks, axis=0))
    o = jnp.concatenate(outs, axis=1)
    ms = _dot(o * o, head_ones, HIGHEST) * (1.0 / DN_DV)
    o = o * lax.rsqrt(ms + EPS) * nw_ref[...]
    out_ref[0] = (o * _silu(z_ref[0])).astype(out_ref.dtype)


def _deltanet(qkv, z, small, cw, alog_row, dtb_row, nw_row):
    b, s, _ = qkv.shape
    const = lambda i, t: (0, 0)
    tile = lambda i, t: (i, t, 0)
    return pl.pallas_call(
        _deltanet_kernel,
        grid=(b, s // SUPER),
        in_specs=[pl.BlockSpec((1, SUPER, DN_QKV), tile),
                  pl.BlockSpec((1, SUPER, DN_QK), tile),
                  pl.BlockSpec((1, SUPER, LANES), tile),
                  pl.BlockSpec((DN_CONV, DN_QKV), const),
                  pl.BlockSpec((1, LANES), const),
                  pl.BlockSpec((1, LANES), const),
                  pl.BlockSpec((1, DN_QK), const)],
        out_specs=pl.BlockSpec((1, SUPER, DN_QK), tile),
        out_shape=jax.ShapeDtypeStruct((b, s, DN_QK), BF16),
        scratch_shapes=[pltpu.VMEM((SUPER + DN_PAD, DN_QKV), F32),
                        pltpu.VMEM((DN_HEADS, DN_DK, DN_DV), F32)],
        compiler_params=pltpu.CompilerParams(dimension_semantics=("parallel", "arbitrary"),
                                             vmem_limit_bytes=VMEM_LIMIT),
        name="deltanet",
    )(qkv, z, small, cw, alog_row, dtb_row, nw_row)


def _compress(t_ref, posa_ref, posb_ref, w1a_ref, w1b_ref, w2_ref):
    t = t_ref[0]
    nrow = t.shape[0]
    p = _dot((t + posa_ref[...]).astype(BF16), w1a_ref[...])
    q = _dot((t + posb_ref[...]).astype(BF16), w1b_ref[...])
    hid = _silu(p + pltpu.roll(q, nrow - 1, 0))
    return _dot(hid.astype(BF16), w2_ref[...])


def _online_update(idx, s, allowed, vch, m_s, l_s, acc_s):
    m_prev = m_s[idx][:, 0:1]
    l_prev = l_s[idx][:, 0:1]
    m_cur = jnp.max(jnp.where(allowed, s, NEG), axis=-1, keepdims=True)
    m_new = jnp.maximum(m_prev, m_cur)
    alpha = jnp.exp(m_prev - m_new)
    p = jnp.where(allowed, jnp.exp(s - m_new), 0.0)
    l_new = alpha * l_prev + jnp.sum(p, axis=-1, keepdims=True)
    acc_s[idx] = alpha * acc_s[idx] + _dot(p.astype(BF16), vch)
    m_s[idx] = jnp.broadcast_to(m_new, m_s.shape[1:])
    l_s[idx] = jnp.broadcast_to(l_new, l_s.shape[1:])


def _nsa_kernel(q_ref, sm_ref, kct_ref, vct_ref, ks_ref, vs_ref, kw_ref, vw_ref,
                kposa_ref, kposb_ref, kw1a_ref, kw1b_ref, kw2_ref,
                vposa_ref, vposb_ref, vw1a_ref, vw1b_ref, vw2_ref,
                out_ref,
                kc_s, vc_s, ksb, vsb, kwb, vwb, qs_s, sel_s, oc_s, m_s, l_s, acc_s):
    i = pl.program_id(1)
    ncmp = kc_s.shape[0]
    nblk = ks_ref.shape[1] // SEL_BLOCK

    @pl.when(i == 0)
    def _():
        kc_s[...] = _compress(kct_ref, kposa_ref, kposb_ref, kw1a_ref, kw1b_ref, kw2_ref)
        vc_s[...] = _compress(vct_ref, vposa_ref, vposb_ref, vw1a_ref, vw1b_ref, vw2_ref).astype(BF16)
        ksb[...] = ks_ref[0].astype(BF16)
        vsb[...] = vs_ref[0].astype(BF16)
        kwb[...] = kw_ref[0].astype(BF16)
        vwb[...] = vw_ref[0].astype(BF16)

    tcol = i * TQ + lax.broadcasted_iota(jnp.int32, (TQ, 1), 0)
    trow = i * TQ + lax.broadcasted_iota(jnp.int32, (1, TQ), 1)
    lane = lax.broadcasted_iota(jnp.int32, (1, LANES), 1)
    valid_c = (lane * CMP_STRIDE + (CMP_LEN - 1)) <= tcol
    oj = lax.broadcasted_iota(jnp.int32, (nblk, ncmp), 0) * SEL_BLOCK
    on = lax.broadcasted_iota(jnp.int32, (nblk, ncmp), 1) * CMP_STRIDE
    ov_t = ((on < oj + SEL_BLOCK) & (on + CMP_LEN > oj)).astype(F32)
    jj = lax.broadcasted_iota(jnp.int32, (nblk, TQ), 0)
    cur = trow // SEL_BLOCK
    elig = jj <= cur
    forced = (jj == 0) | (jj == cur) | (jj == cur - 1)

    kc = kc_s[...]
    vc = vc_s[...]
    for g in range(NSA_KV):
        psum = jnp.zeros((TQ, ncmp), F32)
        for hh in range(NSA_HPG):
            idx = g * NSA_HPG + hh
            qs = q_ref[0, :, idx * LANES:(idx + 1) * LANES] * (NSA_DH ** -0.5)
            qs_s[idx] = qs.astype(BF16)
            s_c = jnp.where(valid_c, _dot_nt(qs, kc, HIGHEST), NEG)
            e = jnp.where(valid_c, jnp.exp(s_c - jnp.max(s_c, axis=-1, keepdims=True)), 0.0)
            den = jnp.sum(e, axis=-1, keepdims=True)
            p = e / jnp.maximum(den, 1e-30)
            oc_s[idx] = _dot(p.astype(BF16), vc)
            psum = psum + p
        imp_t = _dot_nt(ov_t, psum, HIGHEST)
        score = jnp.where(elig, imp_t + jnp.where(forced, SEL_BIG, 0.0), -SEL_BIG)
        rank = jnp.zeros((nblk, TQ), F32)
        for r in range(nblk):
            row = score[r:r + 1, :]
            beats = (row > score) | ((row == score) & (r < jj))
            rank = rank + beats.astype(F32)
        sel_t = (rank < float(N_SEL)).astype(F32)
        sel_pad = jnp.concatenate([sel_t, jnp.zeros((LANES - nblk, TQ), F32)], axis=0)
        sel_s[g] = sel_pad.T.astype(BF16)

    m_s[...] = jnp.full(m_s.shape, NEG, F32)
    l_s[...] = jnp.zeros(l_s.shape, F32)
    acc_s[...] = jnp.zeros(acc_s.shape, F32)
    blk_row = lax.broadcasted_iota(jnp.int32, (LANES, TQ), 0)

    def sel_body(j, carry):
        start = pl.multiple_of(j * TQ, TQ)
        kpos = j * TQ + lax.broadcasted_iota(jnp.int32, (1, TQ), 1)
        expand = (blk_row == kpos // SEL_BLOCK).astype(BF16)
        kch = ksb[pl.ds(start, TQ), :]
        vch = vsb[pl.ds(start, TQ), :]
        causal = kpos <= tcol
        for g in range(NSA_KV):
            allowed = (_dot(sel_s[g], expand) > 0.5) & causal
            for hh in range(NSA_HPG):
                idx = g * NSA_HPG + hh
                _online_update(idx, _dot_nt(qs_s[idx], kch), allowed, vch, m_s, l_s, acc_s)
        return carry

    lax.fori_loop(0, i + 1, sel_body, 0)

    def win_body(j, carry):
        start = pl.multiple_of(j * TQ, TQ)
        kpos = j * TQ + lax.broadcasted_iota(jnp.int32, (1, TQ), 1)
        kch = kwb[pl.ds(start, TQ), :]
        vch = vwb[pl.ds(start, TQ), :]
        allowed = (kpos <= tcol) & (kpos > tcol - WINDOW)
        for idx in range(NSA_HEADS):
            _online_update(NSA_HEADS + idx, _dot_nt(qs_s[idx], kch), allowed, vch, m_s, l_s, acc_s)
        return carry

    lax.fori_loop(jnp.maximum(i - WINDOW // TQ, 0), i + 1, win_body, 0)

    gates = _sigmoid(sm_ref[0])
    for idx in range(NSA_HEADS):
        g = idx // NSA_HPG
        o_s = acc_s[idx] / l_s[idx]
        o_w = acc_s[NSA_HEADS + idx] / l_s[NSA_HEADS + idx]
        gc = gates[:, LANE_GATE + idx:LANE_GATE + idx + 1]
        gs = gates[:, LANE_GATE + NSA_HEADS + idx:LANE_GATE + NSA_HEADS + idx + 1]
        gw = gates[:, LANE_GATE + 2 * NSA_HEADS + idx:LANE_GATE + 2 * NSA_HEADS + idx + 1]
        o = gc * oc_s[idx] + gs * o_s + gw * o_w
        out_ref[0, :, idx * NSA_DH:(idx + 1) * NSA_DH] = o[:, g * NSA_DH:(g + 1) * NSA_DH].astype(out_ref.dtype)


def _nsa(nq, small, kct, vct, ks, vs, kw, vw, kparams, vparams):
    b, s, _ = nq.shape
    ncmp = kct.shape[1]
    tile = lambda bi, i: (bi, i, 0)
    full = lambda bi, i: (bi, 0, 0)
    const = lambda bi, i: (0, 0)
    wspecs = [pl.BlockSpec(p.shape, const) for p in kparams + vparams]
    return pl.pallas_call(
        _nsa_kernel,
        grid=(b, s // TQ),
        in_specs=[pl.BlockSpec((1, TQ, NSA_HEADS * LANES), tile),
                  pl.BlockSpec((1, TQ, LANES), tile),
                  pl.BlockSpec((1, ncmp, kct.shape[2]), full),
                  pl.BlockSpec((1, ncmp, vct.shape[2]), full),
                  pl.BlockSpec((1, s, LANES), full),
                  pl.BlockSpec((1, s, LANES), full),
                  pl.BlockSpec((1, s, LANES), full),
                  pl.BlockSpec((1, s, LANES), full)] + wspecs,
        out_specs=pl.BlockSpec((1, TQ, NSA_HEADS * NSA_DH), tile),
        out_shape=jax.ShapeDtypeStruct((b, s, NSA_HEADS * NSA_DH), BF16),
        scratch_shapes=[pltpu.VMEM((ncmp, LANES), F32),
                        pltpu.VMEM((ncmp, LANES), BF16),
                        pltpu.VMEM((s, LANES), BF16),
                        pltpu.VMEM((s, LANES), BF16),
                        pltpu.VMEM((s, LANES), BF16),
                        pltpu.VMEM((s, LANES), BF16),
                        pltpu.VMEM((NSA_HEADS, TQ, LANES), BF16),
                        pltpu.VMEM((NSA_KV, TQ, LANES), BF16),
                        pltpu.VMEM((NSA_HEADS, TQ, LANES), F32),
                        pltpu.VMEM((2 * NSA_HEADS, TQ, LANES), F32),
                        pltpu.VMEM((2 * NSA_HEADS, TQ, LANES), F32),
                        pltpu.VMEM((2 * NSA_HEADS, TQ, LANES), F32)],
        compiler_params=pltpu.CompilerParams(dimension_semantics=("parallel", "arbitrary"),
                                             vmem_limit_bytes=VMEM_LIMIT),
        name="nsa",
    )(nq, small, kct, vct, ks, vs, kw, vw, *kparams, *vparams)


def _permute_w_in(w):
    d = w.shape[0]
    o = 0
    qkv = w[:, o:o + DN_QKV]; o += DN_QKV
    z = w[:, o:o + DN_HEADS * DN_DV]; o += DN_HEADS * DN_DV
    b_raw = w[:, o:o + DN_HEADS]; o += DN_HEADS
    a_raw = w[:, o:o + DN_HEADS]; o += DN_HEADS
    cf_u = w[:, o:o + 2 * CF_CH]; o += 2 * CF_CH
    n_q = w[:, o:o + NSA_HEADS * NSA_DH]; o += NSA_HEADS * NSA_DH
    kv = []
    for _ in range(6):
        kv.append(w[:, o:o + NSA_KV * NSA_DH]); o += NSA_KV * NSA_DH
    gate = w[:, o:o + 3 * NSA_HEADS]; o += 3 * NSA_HEADS
    n_kc, n_vc, n_ks, n_vs, n_kw, n_vw = kv
    small = jnp.concatenate([b_raw, a_raw, gate, jnp.zeros((d, LANES - 2 * DN_HEADS - 3 * NSA_HEADS), w.dtype)], axis=1)
    slots = []
    zero = jnp.zeros((d, NSA_DH), w.dtype)
    for idx in range(NSA_HEADS):
        g = idx // NSA_HPG
        qh = n_q[:, idx * NSA_DH:(idx + 1) * NSA_DH]
        slots.append(jnp.concatenate([qh, zero] if g == 0 else [zero, qh], axis=1))
    nq = jnp.concatenate(slots, axis=1)
    return jnp.concatenate([qkv, z, small, cf_u, nq, n_ks, n_vs, n_kw, n_vw, n_kc, n_vc], axis=1).astype(BF16)


def _compress_params(pos, w1, w2):
    eye = jnp.eye(NSA_KV, dtype=F32)
    half = CMP_LEN // 2
    w1r = w1.reshape(CMP_LEN, NSA_DH, CMP_HID)

    def expand_w1(part):
        return jnp.einsum("ldj,gh->lgdhj", part, eye).reshape(half * NSA_KV * NSA_DH, NSA_KV * CMP_HID).astype(BF16)

    def expand_pos(part):
        return jnp.broadcast_to(part[:, None, :], (half, NSA_KV, NSA_DH)).reshape(1, half * NSA_KV * NSA_DH)

    w2e = jnp.einsum("jd,gh->gjhd", w2, eye).reshape(NSA_KV * CMP_HID, NSA_KV * NSA_DH).astype(BF16)
    return [expand_pos(pos[:half]), expand_pos(pos[half:]), expand_w1(w1r[:half]), expand_w1(w1r[half:]), w2e]


def _lane_row(vals, first_lane):
    row = jnp.zeros((1, LANES), F32)
    return row.at[0, first_lane:first_lane + vals.shape[0]].set(vals.astype(F32))


def kernel(x, norm_w, w_in, dn_conv_w, dn_a_log, dn_dt_bias, dn_norm_w, cf_dw_w, cf_dw_b, cf_ln_w, cf_ln_b,
           nsa_k_pos, nsa_v_pos, nsa_k_w1, nsa_k_w2, nsa_v_w1, nsa_v_w2, w_out, ffn_w_gate, ffn_w_up, ffn_w_down):
    b, s, d = x.shape
    depth = w_in.shape[0]
    assert s % SUPER == 0 and s % TQ == 0 and s // CMP_STRIDE == LANES and s // SEL_BLOCK <= LANES
    m = b * s
    tm = 512 if m % 512 == 0 else 256
    ts_cf = 512 if s % 512 == 0 else 256
    x2 = x.reshape(m, d)
    ndn = DN_HEADS * DN_DV
    for l in range(depth):
        secs = _in_proj(x2, norm_w[l, 0:1], _permute_w_in(w_in[l]), tm)
        qkv, z, small, cf_u, nq, n_ks, n_vs, n_kw, n_vw, n_kc, n_vc = [t.reshape(b, s, t.shape[1]) for t in secs]
        o_dn = _deltanet(qkv, z, small, dn_conv_w[l], _lane_row(dn_a_log[l], LANE_A), _lane_row(dn_dt_bias[l], LANE_A),
                         jnp.tile(dn_norm_w[l], DN_HEADS)[None, :])
        o_cf = _conformer(cf_u, cf_dw_w[l], cf_dw_b[l][None, :], cf_ln_w[l][None, :], cf_ln_b[l][None, :], ts_cf)
        rows16 = s // CMP_STRIDE
        o_nsa = _nsa(nq, small, n_kc.reshape(b, rows16, CMP_STRIDE * LANES), n_vc.reshape(b, rows16, CMP_STRIDE * LANES),
                     n_ks, n_vs, n_kw, n_vw,
                     _compress_params(nsa_k_pos[l], nsa_k_w1[l], nsa_k_w2[l]),
                     _compress_params(nsa_v_pos[l], nsa_v_w1[l], nsa_v_w2[l]))
        wo = w_out[l].astype(BF16)
        x2 = _mix_ffn(x2, o_dn.reshape(m, ndn), o_cf.reshape(m, CF_CH), o_nsa.reshape(m, NSA_HEADS * NSA_DH),
                      wo[:ndn], wo[ndn:ndn + CF_CH], wo[ndn + CF_CH:], norm_w[l],
                      ffn_w_gate[l].astype(BF16), ffn_w_up[l].astype(BF16), ffn_w_down[l].astype(BF16), tm)
    return x2.reshape(b, s, d)
```

```python
import jax
import jax.numpy as jnp
from jax import lax
from jax.experimental import pallas as pl
from jax.experimental.pallas import tpu as pltpu

F32 = jnp.float32
BF16 = jnp.bfloat16
HIGHEST = lax.Precision.HIGHEST

DN_HEADS = 6
DN_DK = 64
DN_DV = 64
DN_CONV = 4
DN_CHUNK = 64
CF_CH = 256
CF_KERNEL = 31
NSA_HEADS = 6
NSA_KV = 2
NSA_HPG = NSA_HEADS // NSA_KV
NSA_DH = 64
CMP_LEN = 32
CMP_STRIDE = 16
CMP_HID = 2 * NSA_DH
SEL_BLOCK = 64
N_SEL = 8
WINDOW = 512
EPS = 1e-6
NEG = -1e30
SEL_BIG = 1e4

LANES = 128
BF16_ROWS = 16
DN_QK = DN_HEADS * DN_DK
DN_QKV = 2 * DN_QK + DN_HEADS * DN_DV
SUPER = 256
DN_GROUP = 6
TQ = 256
VT_ROWS = NSA_DH + BF16_ROWS
VMEM_LIMIT = 56 * 1024 * 1024

SEC_WIDTHS = (DN_QKV, DN_HEADS * DN_DV, LANES, 2 * CF_CH, NSA_HEADS * LANES,
              LANES, LANES, LANES, LANES, LANES, LANES)
N_ROWMAJOR = 9
LANE_BETA = 0
LANE_A = DN_HEADS
LANE_GATE = 2 * DN_HEADS


def _dot(a, b, precision=None):
    return jnp.dot(a, b, preferred_element_type=F32, precision=precision)


def _dot_nt(a, b, precision=None):
    return lax.dot_general(a, b, (((1,), (1,)), ((), ())), preferred_element_type=F32, precision=precision)


def _dot_tn(a, b, precision=None):
    return lax.dot_general(a, b, (((0,), (0,)), ((), ())), preferred_element_type=F32, precision=precision)


def _split_terms(x, terms):
    out = []
    for _ in range(terms):
        hi = x.astype(BF16)
        out.append(hi)
        x = x - hi.astype(F32)
    return out


def _dot_x01(x, mat01, terms):
    return sum(_dot(p, mat01) for p in _split_terms(x, terms))


def _dot_01x(mat01, x, terms):
    return sum(_dot(mat01, p) for p in _split_terms(x, terms))


def _sigmoid(x):
    return 1.0 / (1.0 + jnp.exp(-x))


def _silu(x):
    return x * _sigmoid(x)


def _rms(x, w):
    return x * lax.rsqrt(jnp.mean(x * x, axis=-1, keepdims=True) + EPS) * w


def _in_proj_kernel(x_ref, nw_ref, w_ref, *refs):
    out_refs, stage = refs[:-1], refs[-1]
    h = _rms(x_ref[...], nw_ref[...]).astype(BF16)
    rows16 = x_ref.shape[0] // CMP_STRIDE
    off = 0
    for k, (o_ref, wd) in enumerate(zip(out_refs, SEC_WIDTHS)):
        y = _dot(h, w_ref[:, off:off + wd])
        off += wd
        if k < N_ROWMAJOR:
            o_ref[...] = y
        else:
            stage[...] = y
            for t in range(CMP_STRIDE):
                o_ref[:, t * LANES:(t + 1) * LANES] = stage[pl.ds(t, rows16, stride=CMP_STRIDE), :]


def _in_proj(x2, nw, w_perm, tm):
    m, d = x2.shape
    n = w_perm.shape[1]
    shapes = [(m, wd) if k < N_ROWMAJOR else (m // CMP_STRIDE, CMP_STRIDE * wd) for k, wd in enumerate(SEC_WIDTHS)]
    blocks = [(tm, wd) if k < N_ROWMAJOR else (tm // CMP_STRIDE, CMP_STRIDE * wd) for k, wd in enumerate(SEC_WIDTHS)]
    return pl.pallas_call(
        _in_proj_kernel,
        grid=(m // tm,),
        in_specs=[pl.BlockSpec((tm, d), lambda i: (i, 0)),
                  pl.BlockSpec((1, d), lambda i: (0, 0)),
                  pl.BlockSpec((d, n), lambda i: (0, 0))],
        out_specs=[pl.BlockSpec(blk, lambda i: (i, 0)) for blk in blocks],
        out_shape=[jax.ShapeDtypeStruct(shp, F32) for shp in shapes],
        scratch_shapes=[pltpu.VMEM((tm, LANES), F32)],
        compiler_params=pltpu.CompilerParams(dimension_semantics=("parallel",), vmem_limit_bytes=VMEM_LIMIT),
        name="in_proj",
    )(x2, nw, w_perm)


def _mix_ffn_kernel(x_ref, odn_ref, ocf_ref, onsa_ref, wodn_ref, wocf_ref, wonsa_ref, nw_ref,
                    wg_ref, wu_ref, wd_ref, out_ref):
    mix = (_dot(odn_ref[...], wodn_ref[...]) + _dot(ocf_ref[...], wocf_ref[...])
           + _dot(onsa_ref[...], wonsa_ref[...]))
    x1 = x_ref[...] + _rms(mix, nw_ref[1:2, :])
    h = _rms(x1, nw_ref[2:3, :]).astype(BF16)
    g = _dot(h, wg_ref[...])
    u = _dot(h, wu_ref[...])
    a = (_silu(g) * u).astype(BF16)
    f = _dot(a, wd_ref[...])
    out_ref[...] = x1 + _rms(f, nw_ref[3:4, :])


def _mix_ffn(x2, odn, ocf, onsa, wodn, wocf, wonsa, nw, wg, wu, wd, tm):
    m, d = x2.shape
    dff = wg.shape[1]
    const = lambda i: (0, 0)
    row = lambda i: (i, 0)
    return pl.pallas_call(
        _mix_ffn_kernel,
        grid=(m // tm,),
        in_specs=[pl.BlockSpec((tm, d), row),
                  pl.BlockSpec((tm, odn.shape[1]), row),
                  pl.BlockSpec((tm, ocf.shape[1]), row),
                  pl.BlockSpec((tm, onsa.shape[1]), row),
                  pl.BlockSpec(wodn.shape, const),
                  pl.BlockSpec(wocf.shape, const),
                  pl.BlockSpec(wonsa.shape, const),
                  pl.BlockSpec(nw.shape, const),
                  pl.BlockSpec((d, dff), const),
                  pl.BlockSpec((d, dff), const),
                  pl.BlockSpec((dff, d), const)],
        out_specs=pl.BlockSpec((tm, d), row),
        out_shape=jax.ShapeDtypeStruct((m, d), F32),
        compiler_params=pltpu.CompilerParams(dimension_semantics=("parallel",), vmem_limit_bytes=VMEM_LIMIT),
        name="mix_ffn",
    )(x2, odn, ocf, onsa, wodn, wocf, wonsa, nw, wg, wu, wd)


CF_ROWS = 64
CF_PAD = 32


def _conformer_kernel(u_ref, dw_ref, db_ref, lnw_ref, lnb_ref, out_ref, hbuf):
    ts = u_ref.shape[1]

    @pl.when(pl.program_id(1) == 0)
    def _():
        hbuf[0:CF_PAD, :] = jnp.zeros((CF_PAD, CF_CH), F32)

    u = u_ref[0]
    hbuf[CF_PAD:CF_PAD + ts, :] = u[:, :CF_CH] * _sigmoid(u[:, CF_CH:])
    first = CF_PAD - (CF_KERNEL - 1)
    for r in range(ts // CF_ROWS):
        acc = jnp.zeros((CF_ROWS, CF_CH), F32) + db_ref[...]
        for j in range(CF_KERNEL):
            acc = acc + dw_ref[j:j + 1, :] * hbuf[pl.ds(r * CF_ROWS + first + j, CF_ROWS), :]
        mu = jnp.mean(acc, axis=-1, keepdims=True)
        cen = acc - mu
        var = jnp.mean(cen * cen, axis=-1, keepdims=True)
        y = cen * lax.rsqrt(var + EPS) * lnw_ref[...] + lnb_ref[...]
        out_ref[0, r * CF_ROWS:(r + 1) * CF_ROWS, :] = _silu(y).astype(out_ref.dtype)
    hbuf[0:CF_PAD, :] = hbuf[ts:ts + CF_PAD, :]


def _conformer(cf_u, dw, db, lnw, lnb, ts):
    b, s, _ = cf_u.shape
    const = lambda i, t: (0, 0)
    return pl.pallas_call(
        _conformer_kernel,
        grid=(b, s // ts),
        in_specs=[pl.BlockSpec((1, ts, 2 * CF_CH), lambda i, t: (i, t, 0)),
                  pl.BlockSpec((CF_KERNEL, CF_CH), const),
                  pl.BlockSpec((1, CF_CH), const),
                  pl.BlockSpec((1, CF_CH), const),
                  pl.BlockSpec((1, CF_CH), const)],
        out_specs=pl.BlockSpec((1, ts, CF_CH), lambda i, t: (i, t, 0)),
        out_shape=jax.ShapeDtypeStruct((b, s, CF_CH), BF16),
        scratch_shapes=[pltpu.VMEM((ts + CF_PAD, CF_CH), F32)],
        compiler_params=pltpu.CompilerParams(dimension_semantics=("parallel", "arbitrary")),
        name="conformer",
    )(cf_u, dw, db, lnw, lnb)


DN_PAD = 8


def _deltanet_kernel(qkv_ref, z_ref, sm_ref, cw_ref, alog_ref, dtb_ref, nw_ref, out_ref, xbuf, state):
    n = SUPER
    nchunk = n // DN_CHUNK

    @pl.when(pl.program_id(1) == 0)
    def _():
        xbuf[0:DN_PAD, :] = jnp.zeros((DN_PAD, DN_QKV), F32)
        state[...] = jnp.zeros(state.shape, F32)

    xbuf[DN_PAD:DN_PAD + n, :] = qkv_ref[0]
    first = DN_PAD - (DN_CONV - 1)
    y = cw_ref[0:1, :] * xbuf[pl.ds(first, n), :]
    for j in range(1, DN_CONV):
        y = y + cw_ref[j:j + 1, :] * xbuf[pl.ds(first + j, n), :]
    xbuf[0:DN_PAD, :] = xbuf[n:n + DN_PAD, :]
    y = _silu(y)
    q_all = y[:, 0:DN_QK]
    k_all = y[:, DN_QK:2 * DN_QK]
    v_all = y[:, 2 * DN_QK:]

    hr = lax.broadcasted_iota(jnp.int32, (DN_QK, DN_QK), 0) // DN_DK
    hc = lax.broadcasted_iota(jnp.int32, (DN_QK, DN_QK), 1) // DN_DK
    head_ones = (hr == hc).astype(BF16)
    q_n = q_all * lax.rsqrt(_dot_x01(q_all * q_all, head_ones, 2) + EPS) * (DN_DK ** -0.5)
    k_n = k_all * lax.rsqrt(_dot_x01(k_all * k_all, head_ones, 2) + EPS)

    sm = sm_ref[0]
    beta_all = _sigmoid(sm)
    sp_in = sm + dtb_ref[...]
    softplus = jnp.maximum(sp_in, 0.0) + jnp.log1p(jnp.exp(-jnp.abs(sp_in)))
    g_all = -jnp.exp(alog_ref[...]) * softplus
    ri = lax.broadcasted_iota(jnp.int32, (n, n), 0)
    ci = lax.broadcasted_iota(jnp.int32, (n, n), 1)
    same_chunk = (ri // DN_CHUNK) == (ci // DN_CHUNK)
    causal = same_chunk & (ri >= ci)
    strict = same_chunk & (ri > ci)
    gam = _dot_01x(causal.astype(BF16), g_all, 3)
    glast = _dot_01x(same_chunk.astype(BF16), g_all, 3)
    gam_t = gam.T
    e_gam = jnp.exp(gam)
    e_rest = jnp.exp(glast - gam)
    e_tot = jnp.exp(glast)
    nfac = (DN_CHUNK - 1).bit_length() - 1

    outs = [None] * DN_HEADS
    for h0 in range(0, DN_HEADS, DN_GROUP):
        heads = range(h0, h0 + DN_GROUP)
        q, k, v, bcol, egcol, ercol, etcol, lmat, qkd = {}, {}, {}, {}, {}, {}, {}, {}, {}
        for h in heads:
            sl = slice(h * DN_DK, (h + 1) * DN_DK)
            lg = LANE_A + h
            q[h], k[h], v[h] = q_n[:, sl], k_n[:, sl], v_all[:, sl]
            bcol[h] = beta_all[:, LANE_BETA + h:LANE_BETA + h + 1]
            egcol[h] = e_gam[:, lg:lg + 1]
            ercol[h] = e_rest[:, lg:lg + 1]
            etcol[h] = e_tot[:, lg:lg + 1]
            kb = k[h].astype(BF16)
            dec = jnp.exp(jnp.where(causal, gam[:, lg:lg + 1] - gam_t[lg:lg + 1, :], NEG))
            lmat[h] = jnp.where(strict, _dot_nt((k[h] * bcol[h]).astype(BF16), kb) * dec, 0.0)
            qkd[h] = (_dot_nt(q[h].astype(BF16), kb) * dec).astype(BF16)
        ps = {h: -lmat[h] for h in heads}
        mp = {}
        for h in heads:
            lb = lmat[h].astype(BF16)
            mp[h] = _dot(lb, lb)
        for s in range(nfac):
            mb = {h: mp[h].astype(BF16) for h in heads}
            prod = {h: _dot(ps[h].astype(BF16), mb[h]) for h in heads}
            ps = {h: ps[h] + mp[h] + prod[h] for h in heads}
            if s < nfac - 1:
                mp = {h: _dot(mb[h], mb[h]) for h in heads}
        u, w, qeff, o_intra, kd = {}, {}, {}, {}, {}
        for h in heads:
            rhs = jnp.concatenate([v[h] * bcol[h], k[h] * (bcol[h] * egcol[h])], axis=1)
            x = rhs + _dot(ps[h].astype(BF16), rhs.astype(BF16))
            qx = _dot(qkd[h], x.astype(BF16))
            u[h], w[h] = x[:, :DN_DV], x[:, DN_DV:]
            qeff[h] = q[h] * egcol[h] - qx[:, DN_DV:]
            o_intra[h] = qx[:, :DN_DV]
            kd[h] = k[h] * ercol[h]
        st = {h: state[h] for h in heads}
        o_chunks = {h: [] for h in heads}
        for c in range(nchunk):
            rs = slice(c * DN_CHUNK, (c + 1) * DN_CHUNK)
            ws = {h: _dot(jnp.concatenate([w[h][rs], qeff[h][rs]], axis=0).astype(BF16), st[h].astype(BF16))
                  for h in heads}
            for h in heads:
                vnew = u[h][rs] - ws[h][:DN_CHUNK]
                o_chunks[h].append(ws[h][DN_CHUNK:] + o_intra[h][rs])
                st[h] = st[h] * etcol[h][rs] + _dot_tn(kd[h][rs].astype(BF16), vnew.astype(BF16))
        for h in heads:
            state[h] = st[h]
            outs[h] = jnp.concatenate(o_chunks[h], axis=0)
    o = jnp.concatenate(outs, axis=1)
    ms = _dot_x01(o * o, head_ones, 2) * (1.0 / DN_DV)
    o = o * lax.rsqrt(ms + EPS) * nw_ref[...]
    out_ref[0] = (o * _silu(z_ref[0])).astype(out_ref.dtype)


def _deltanet(qkv, z, small, cw, alog_row, dtb_row, nw_row):
    b, s, _ = qkv.shape
    const = lambda i, t: (0, 0)
    tile = lambda i, t: (i, t, 0)
    return pl.pallas_call(
        _deltanet_kernel,
        grid=(b, s // SUPER),
        in_specs=[pl.BlockSpec((1, SUPER, DN_QKV), tile),
                  pl.BlockSpec((1, SUPER, DN_QK), tile),
                  pl.BlockSpec((1, SUPER, LANES), tile),
                  pl.BlockSpec((DN_CONV, DN_QKV), const),
                  pl.BlockSpec((1, LANES), const),
                  pl.BlockSpec((1, LANES), const),
                  pl.BlockSpec((1, DN_QK), const)],
        out_specs=pl.BlockSpec((1, SUPER, DN_QK), tile),
        out_shape=jax.ShapeDtypeStruct((b, s, DN_QK), BF16),
        scratch_shapes=[pltpu.VMEM((SUPER + DN_PAD, DN_QKV), F32),
                        pltpu.VMEM((DN_HEADS, DN_DK, DN_DV), F32)],
        compiler_params=pltpu.CompilerParams(dimension_semantics=("parallel", "arbitrary"),
                                             vmem_limit_bytes=VMEM_LIMIT),
        name="deltanet",
    )(qkv, z, small, cw, alog_row, dtb_row, nw_row)


def _compress(t_ref, posa_ref, posb_ref, w1a_ref, w1b_ref, w2_ref):
    t = t_ref[0]
    nrow = t.shape[0]
    p = _dot((t + posa_ref[...]).astype(BF16), w1a_ref[...])
    q = _dot((t + posb_ref[...]).astype(BF16), w1b_ref[...])
    hid = _silu(p + pltpu.roll(q, nrow - 1, 0))
    return _dot(hid.astype(BF16), w2_ref[...])


def _store_vt(dst, v_ref):
    vt = v_ref[0].T
    ones = jnp.ones((BF16_ROWS, TQ), BF16)
    for g in range(NSA_KV):
        for c in range(dst.shape[1]):
            blk = vt[g * NSA_DH:(g + 1) * NSA_DH, c * TQ:(c + 1) * TQ].astype(BF16)
            dst[g, c] = jnp.concatenate([blk, ones], axis=0)


def _online_update(slots, kch, qts, allowed, vts, m_s, acc_s):
    scores = [_dot(kch, qt) for qt in qts]
    probs, alphas = [], []
    for slot, s, ok in zip(slots, scores, allowed):
        s = jnp.where(ok, s, NEG)
        m_prev = m_s[slot]
        m_new = jnp.maximum(m_prev, jnp.max(s, axis=0, keepdims=True))
        probs.append(jnp.exp(s - m_new).astype(BF16))
        alphas.append(jnp.exp(m_prev - m_new))
        m_s[slot] = m_new
    pv = [_dot(vt, p) for vt, p in zip(vts, probs)]
    for slot, a, o in zip(slots, alphas, pv):
        acc_s[slot] = a * acc_s[slot] + o


def _nsa_kernel(q_ref, sm_ref, kct_ref, vct_ref, ks_ref, vs_ref, kw_ref, vw_ref,
                kposa_ref, kposb_ref, kw1a_ref, kw1b_ref, kw2_ref,
                vposa_ref, vposb_ref, vw1a_ref, vw1b_ref, vw2_ref,
                out_ref,
                kc_s, vct_s, ksb, kwb, vst_s, vwt_s, qt_s, sel_s, oc_s, m_s, acc_s):
    i = pl.program_id(1)
    ncmp = kc_s.shape[1]
    nblk = ks_ref.shape[1] // SEL_BLOCK
    blk_per_chunk = TQ // SEL_BLOCK
    heads = range(NSA_HEADS)

    @pl.when(i == 0)
    def _():
        kc = _compress(kct_ref, kposa_ref, kposb_ref, kw1a_ref, kw1b_ref, kw2_ref)
        kc_hi, kc_lo = _split_terms(kc, 2)
        kc_s[0] = kc_hi
        kc_s[1] = kc_lo
        vct_s[...] = _compress(vct_ref, vposa_ref, vposb_ref, vw1a_ref, vw1b_ref, vw2_ref).T.astype(BF16)
        ksb[...] = ks_ref[0].astype(BF16)
        kwb[...] = kw_ref[0].astype(BF16)
        _store_vt(vst_s, vs_ref)
        _store_vt(vwt_s, vw_ref)

    trow = i * TQ + lax.broadcasted_iota(jnp.int32, (1, TQ), 1)
    krel = lax.broadcasted_iota(jnp.int32, (TQ, 1), 0)
    cmp_end = lax.broadcasted_iota(jnp.int32, (ncmp, 1), 0) * CMP_STRIDE + (CMP_LEN - 1)
    valid_c = cmp_end <= trow
    oj = lax.broadcasted_iota(jnp.int32, (nblk, ncmp), 0) * SEL_BLOCK
    on = lax.broadcasted_iota(jnp.int32, (nblk, ncmp), 1) * CMP_STRIDE
    ov_t = ((on < oj + SEL_BLOCK) & (on + CMP_LEN > oj)).astype(BF16)
    jj = lax.broadcasted_iota(jnp.int32, (nblk, TQ), 0)
    cur = trow // SEL_BLOCK
    elig = jj <= cur
    forced = (jj == 0) | (jj == cur) | (jj == cur - 1)

    q_all = q_ref[0] * (NSA_DH ** -0.5)
    kc_hi, kc_lo = kc_s[0], kc_s[1]
    s_c = []
    for idx in heads:
        qt = q_all[:, idx * LANES:(idx + 1) * LANES].T
        qt_hi, qt_lo = _split_terms(qt, 2)
        qt_s[idx] = qt_hi
        s_c.append(_dot(kc_hi, qt_hi) + _dot(kc_hi, qt_lo) + _dot(kc_lo, qt_hi))
    probs = []
    for idx in heads:
        s = jnp.where(valid_c, s_c[idx], NEG)
        e = jnp.where(valid_c, jnp.exp(s - jnp.max(s, axis=0, keepdims=True)), 0.0)
        probs.append(e / jnp.maximum(jnp.sum(e, axis=0, keepdims=True), 1e-30))
    for idx in heads:
        g = idx // NSA_HPG
        oc_s[idx] = _dot(vct_s[g * NSA_DH:(g + 1) * NSA_DH, :], probs[idx].astype(BF16))
    for g in range(NSA_KV):
        psum = sum(probs[g * NSA_HPG:(g + 1) * NSA_HPG])
        imp_t = _dot_01x(ov_t, psum, 3)
        score = jnp.where(elig, imp_t + jnp.where(forced, SEL_BIG, 0.0), -SEL_BIG)
        rank = jnp.zeros((nblk, TQ), F32)
        for r in range(nblk):
            row = score[r:r + 1, :]
            beats = (row > score) | ((row == score) & (r < jj))
            rank = rank + beats.astype(F32)
        sel_t = (rank < float(N_SEL)).astype(F32)
        for r in range(nblk):
            sel_s[g * nblk + r] = sel_t[r:r + 1, :]

    m_s[...] = jnp.full(m_s.shape, NEG, F32)
    acc_s[...] = jnp.zeros(acc_s.shape, F32)

    def sel_body(j, carry):
        kch = ksb[pl.ds(pl.multiple_of(j * TQ, TQ), TQ), :]
        causal = (j * TQ + krel) <= trow
        allowed = []
        for g in range(NSA_KV):
            rows = [jnp.broadcast_to(sel_s[g * nblk + j * blk_per_chunk + bb], (SEL_BLOCK, TQ))
                    for bb in range(blk_per_chunk)]
            allowed.append((jnp.concatenate(rows, axis=0) > 0.5) & causal)
        _online_update(list(heads), kch, [qt_s[idx] for idx in heads], [allowed[idx // NSA_HPG] for idx in heads],
                       [vst_s[idx // NSA_HPG, j] for idx in heads], m_s, acc_s)
        return carry

    lax.fori_loop(0, i + 1, sel_body, 0)

    def win_body(jr, carry):
        j = i - jr
        kch = kwb[pl.ds(pl.multiple_of(j * TQ, TQ), TQ), :]
        kpos = j * TQ + krel
        allowed = (kpos <= trow) & (kpos > trow - WINDOW)
        _online_update([NSA_HEADS + idx for idx in heads], kch, [qt_s[idx] for idx in heads], [allowed] * NSA_HEADS,
                       [vwt_s[idx // NSA_HPG, j] for idx in heads], m_s, acc_s)
        return carry

    lax.fori_loop(0, jnp.minimum(i, WINDOW // TQ) + 1, win_body, 0)

    gates_t = _sigmoid(sm_ref[0]).T
    outs = []
    for idx in range(NSA_HEADS):
        acc = acc_s[idx]
        o_s = acc[:NSA_DH] * (1.0 / acc[NSA_DH:NSA_DH + 1])
        acc = acc_s[NSA_HEADS + idx]
        o_w = acc[:NSA_DH] * (1.0 / acc[NSA_DH:NSA_DH + 1])
        lg = LANE_GATE + idx
        outs.append(gates_t[lg:lg + 1] * oc_s[idx] + gates_t[lg + NSA_HEADS:lg + NSA_HEADS + 1] * o_s
                    + gates_t[lg + 2 * NSA_HEADS:lg + 2 * NSA_HEADS + 1] * o_w)
    out_ref[0] = jnp.concatenate(outs, axis=0).T.astype(out_ref.dtype)


def _nsa(nq, small, kct, vct, ks, vs, kw, vw, kparams, vparams):
    b, s, _ = nq.shape
    ncmp = kct.shape[1]
    nchunks = s // TQ
    tile = lambda bi, i: (bi, i, 0)
    full = lambda bi, i: (bi, 0, 0)
    const = lambda bi, i: (0, 0)
    wspecs = [pl.BlockSpec(p.shape, const) for p in kparams + vparams]
    return pl.pallas_call(
        _nsa_kernel,
        grid=(b, nchunks),
        in_specs=[pl.BlockSpec((1, TQ, NSA_HEADS * LANES), tile),
                  pl.BlockSpec((1, TQ, LANES), tile),
                  pl.BlockSpec((1, ncmp, kct.shape[2]), full),
                  pl.BlockSpec((1, ncmp, vct.shape[2]), full),
                  pl.BlockSpec((1, s, LANES), full),
                  pl.BlockSpec((1, s, LANES), full),
                  pl.BlockSpec((1, s, LANES), full),
                  pl.BlockSpec((1, s, LANES), full)] + wspecs,
        out_specs=pl.BlockSpec((1, TQ, NSA_HEADS * NSA_DH), tile),
        out_shape=jax.ShapeDtypeStruct((b, s, NSA_HEADS * NSA_DH), BF16),
        scratch_shapes=[pltpu.VMEM((2, ncmp, LANES), BF16),
                        pltpu.VMEM((LANES, ncmp), BF16),
                        pltpu.VMEM((s, LANES), BF16),
                        pltpu.VMEM((s, LANES), BF16),
                        pltpu.VMEM((NSA_KV, nchunks, VT_ROWS, TQ), BF16),
                        pltpu.VMEM((NSA_KV, nchunks, VT_ROWS, TQ), BF16),
                        pltpu.VMEM((NSA_HEADS, LANES, TQ), BF16),
                        pltpu.VMEM((NSA_KV * (s // SEL_BLOCK), 1, TQ), F32),
                        pltpu.VMEM((NSA_HEADS, NSA_DH, TQ), F32),
                        pltpu.VMEM((2 * NSA_HEADS, 1, TQ), F32),
                        pltpu.VMEM((2 * NSA_HEADS, VT_ROWS, TQ), F32)],
        compiler_params=pltpu.CompilerParams(dimension_semantics=("parallel", "arbitrary"),
                                             vmem_limit_bytes=VMEM_LIMIT),
        name="nsa",
    )(nq, small, kct, vct, ks, vs, kw, vw, *kparams, *vparams)


def _permute_w_in(w):
    d = w.shape[0]
    o = 0
    qkv = w[:, o:o + DN_QKV]; o += DN_QKV
    z = w[:, o:o + DN_HEADS * DN_DV]; o += DN_HEADS * DN_DV
    b_raw = w[:, o:o + DN_HEADS]; o += DN_HEADS
    a_raw = w[:, o:o + DN_HEADS]; o += DN_HEADS
    cf_u = w[:, o:o + 2 * CF_CH]; o += 2 * CF_CH
    n_q = w[:, o:o + NSA_HEADS * NSA_DH]; o += NSA_HEADS * NSA_DH
    kv = []
    for _ in range(6):
        kv.append(w[:, o:o + NSA_KV * NSA_DH]); o += NSA_KV * NSA_DH
    gate = w[:, o:o + 3 * NSA_HEADS]; o += 3 * NSA_HEADS
    n_kc, n_vc, n_ks, n_vs, n_kw, n_vw = kv
    small = jnp.concatenate([b_raw, a_raw, gate, jnp.zeros((d, LANES - 2 * DN_HEADS - 3 * NSA_HEADS), w.dtype)], axis=1)
    slots = []
    zero = jnp.zeros((d, NSA_DH), w.dtype)
    for idx in range(NSA_HEADS):
        g = idx // NSA_HPG
        qh = n_q[:, idx * NSA_DH:(idx + 1) * NSA_DH]
        slots.append(jnp.concatenate([qh, zero] if g == 0 else [zero, qh], axis=1))
    nq = jnp.concatenate(slots, axis=1)
    return jnp.concatenate([qkv, z, small, cf_u, nq, n_ks, n_vs, n_kw, n_vw, n_kc, n_vc], axis=1).astype(BF16)


def _compress_params(pos, w1, w2):
    eye = jnp.eye(NSA_KV, dtype=F32)
    half = CMP_LEN // 2
    w1r = w1.reshape(CMP_LEN, NSA_DH, CMP_HID)

    def expand_w1(part):
        return jnp.einsum("ldj,gh->lgdhj", part, eye).reshape(half * NSA_KV * NSA_DH, NSA_KV * CMP_HID).astype(BF16)

    def expand_pos(part):
        return jnp.broadcast_to(part[:, None, :], (half, NSA_KV, NSA_DH)).reshape(1, half * NSA_KV * NSA_DH)

    w2e = jnp.einsum("jd,gh->gjhd", w2, eye).reshape(NSA_KV * CMP_HID, NSA_KV * NSA_DH).astype(BF16)
    return [expand_pos(pos[:half]), expand_pos(pos[half:]), expand_w1(w1r[:half]), expand_w1(w1r[half:]), w2e]


def _lane_row(vals, first_lane):
    pad = jnp.zeros((LANES - first_lane - vals.shape[0],), F32)
    return jnp.concatenate([jnp.zeros((first_lane,), F32), vals.astype(F32), pad])[None, :]


def kernel(x, norm_w, w_in, dn_conv_w, dn_a_log, dn_dt_bias, dn_norm_w, cf_dw_w, cf_dw_b, cf_ln_w, cf_ln_b,
           nsa_k_pos, nsa_v_pos, nsa_k_w1, nsa_k_w2, nsa_v_w1, nsa_v_w2, w_out, ffn_w_gate, ffn_w_up, ffn_w_down):
    b, s, d = x.shape
    depth = w_in.shape[0]
    assert s % SUPER == 0 and s % TQ == 0 and s // CMP_STRIDE == LANES and s // SEL_BLOCK <= LANES
    m = b * s
    tm = 512 if m % 512 == 0 else 256
    ts_cf = 512 if s % 512 == 0 else 256
    x2 = x.reshape(m, d)
    ndn = DN_HEADS * DN_DV
    rows16 = s // CMP_STRIDE
    for l in range(depth):
        secs = _in_proj(x2, norm_w[l, 0:1], _permute_w_in(w_in[l]), tm)
        qkv, z, small, cf_u, nq, n_ks, n_vs, n_kw, n_vw = [t.reshape(b, s, t.shape[1]) for t in secs[:N_ROWMAJOR]]
        n_kct, n_vct = [t.reshape(b, rows16, t.shape[1]) for t in secs[N_ROWMAJOR:]]
        o_dn = _deltanet(qkv, z, small, dn_conv_w[l], _lane_row(dn_a_log[l], LANE_A), _lane_row(dn_dt_bias[l], LANE_A),
                         jnp.tile(dn_norm_w[l], DN_HEADS)[None, :])
        o_cf = _conformer(cf_u, cf_dw_w[l], cf_dw_b[l][None, :], cf_ln_w[l][None, :], cf_ln_b[l][None, :], ts_cf)
        o_nsa = _nsa(nq, small, n_kct, n_vct, n_ks, n_vs, n_kw, n_vw,
                     _compress_params(nsa_k_pos[l], nsa_k_w1[l], nsa_k_w2[l]),
                     _compress_params(nsa_v_pos[l], nsa_v_w1[l], nsa_v_w2[l]))
        wo = w_out[l].astype(BF16)
        x2 = _mix_ffn(x2, o_dn.reshape(m, ndn), o_cf.reshape(m, CF_CH), o_nsa.reshape(m, NSA_HEADS * NSA_DH),
                      wo[:ndn], wo[ndn:ndn + CF_CH], wo[ndn + CF_CH:], norm_w[l],
                      ffn_w_gate[l].astype(BF16), ffn_w_up[l].astype(BF16), ffn_w_down[l].astype(BF16), tm)
    return x2.reshape(b, s, d)
```

```python
import jax
import jax.numpy as jnp
import numpy as np
from jax import lax
from jax.experimental import pallas as pl
from jax.experimental.pallas import tpu as pltpu

F32 = jnp.float32
BF16 = jnp.bfloat16
HIGHEST = lax.Precision.HIGHEST

DN_HEADS = 6
DN_DK = 64
DN_DV = 64
DN_CONV = 4
DN_CHUNK = 64
CF_CH = 256
CF_KERNEL = 31
NSA_HEADS = 6
NSA_KV = 2
NSA_HPG = NSA_HEADS // NSA_KV
NSA_DH = 64
CMP_LEN = 32
CMP_STRIDE = 16
CMP_HID = 2 * NSA_DH
SEL_BLOCK = 64
N_SEL = 8
WINDOW = 512
EPS = 1e-6
NEG = -1e30
SEL_BIG = 1e4

LANES = 128
SUBLANES = 8
BF16_ROWS = 16
DN_QK = DN_HEADS * DN_DK
DN_QKV = 2 * DN_QK + DN_HEADS * DN_DV
SUPER = 256
DN_GROUP = 6
TQ = 256
VT_ROWS = NSA_DH + BF16_ROWS
VMEM_LIMIT = 56 * 1024 * 1024

SEC_WIDTHS = (DN_QKV, DN_HEADS * DN_DV, LANES, 2 * CF_CH, NSA_HEADS * LANES,
              LANES, LANES, LANES, LANES, LANES, LANES)
N_ROWMAJOR = 9
LANE_BETA = 0
LANE_A = DN_HEADS
LANE_GATE = 2 * DN_HEADS


def _dot(a, b, precision=None):
    return jnp.dot(a, b, preferred_element_type=F32, precision=precision)


def _dot_nt(a, b, precision=None):
    return lax.dot_general(a, b, (((1,), (1,)), ((), ())), preferred_element_type=F32, precision=precision)


def _dot_tn(a, b, precision=None):
    return lax.dot_general(a, b, (((0,), (0,)), ((), ())), preferred_element_type=F32, precision=precision)


def _split_terms(x, terms):
    out = []
    for _ in range(terms):
        hi = x.astype(BF16)
        out.append(hi)
        x = x - hi.astype(F32)
    return out


def _dot_x01(x, mat01, terms):
    return sum(_dot(p, mat01) for p in _split_terms(x, terms))


def _dot_01x(mat01, x, terms):
    return sum(_dot(mat01, p) for p in _split_terms(x, terms))


def _sigmoid(x):
    return 1.0 / (1.0 + jnp.exp(-x))


def _silu(x):
    return x * _sigmoid(x)


def _rms(x, w):
    return x * lax.rsqrt(jnp.mean(x * x, axis=-1, keepdims=True) + EPS) * w


def _in_proj_kernel(x_ref, nw_ref, w_ref, *refs):
    out_refs, stage = refs[:-1], refs[-1]
    h = _rms(x_ref[...], nw_ref[...]).astype(BF16)
    rows16 = x_ref.shape[0] // CMP_STRIDE
    off = 0
    for k, (o_ref, wd) in enumerate(zip(out_refs, SEC_WIDTHS)):
        y = _dot(h, w_ref[:, off:off + wd])
        off += wd
        if k < N_ROWMAJOR:
            o_ref[...] = y
        else:
            stage[...] = y
            for t in range(CMP_STRIDE):
                o_ref[:, t * LANES:(t + 1) * LANES] = stage[pl.ds(t, rows16, stride=CMP_STRIDE), :]


def _in_proj(x2, nw, w_perm, tm):
    m, d = x2.shape
    n = w_perm.shape[1]
    shapes = [(m, wd) if k < N_ROWMAJOR else (m // CMP_STRIDE, CMP_STRIDE * wd) for k, wd in enumerate(SEC_WIDTHS)]
    blocks = [(tm, wd) if k < N_ROWMAJOR else (tm // CMP_STRIDE, CMP_STRIDE * wd) for k, wd in enumerate(SEC_WIDTHS)]
    return pl.pallas_call(
        _in_proj_kernel,
        grid=(m // tm,),
        in_specs=[pl.BlockSpec((tm, d), lambda i: (i, 0)),
                  pl.BlockSpec((1, d), lambda i: (0, 0)),
                  pl.BlockSpec((d, n), lambda i: (0, 0))],
        out_specs=[pl.BlockSpec(blk, lambda i: (i, 0)) for blk in blocks],
        out_shape=[jax.ShapeDtypeStruct(shp, F32) for shp in shapes],
        scratch_shapes=[pltpu.VMEM((tm, LANES), F32)],
        compiler_params=pltpu.CompilerParams(dimension_semantics=("parallel",), vmem_limit_bytes=VMEM_LIMIT),
        name="in_proj",
    )(x2, nw, w_perm)


def _mix_ffn_kernel(x_ref, odn_ref, ocf_ref, onsa_ref, wodn_ref, wocf_ref, wonsa_ref, nw_ref,
                    wg_ref, wu_ref, wd_ref, out_ref):
    mix = (_dot(odn_ref[...], wodn_ref[...]) + _dot(ocf_ref[...], wocf_ref[...])
           + _dot(onsa_ref[...], wonsa_ref[...]))
    x1 = x_ref[...] + _rms(mix, nw_ref[1:2, :])
    h = _rms(x1, nw_ref[2:3, :]).astype(BF16)
    g = _dot(h, wg_ref[...])
    u = _dot(h, wu_ref[...])
    a = (_silu(g) * u).astype(BF16)
    f = _dot(a, wd_ref[...])
    out_ref[...] = x1 + _rms(f, nw_ref[3:4, :])


def _mix_ffn(x2, odn, ocf, onsa, wodn, wocf, wonsa, nw, wg, wu, wd, tm):
    m, d = x2.shape
    dff = wg.shape[1]
    const = lambda i: (0, 0)
    row = lambda i: (i, 0)
    return pl.pallas_call(
        _mix_ffn_kernel,
        grid=(m // tm,),
        in_specs=[pl.BlockSpec((tm, d), row),
                  pl.BlockSpec((tm, odn.shape[1]), row),
                  pl.BlockSpec((tm, ocf.shape[1]), row),
                  pl.BlockSpec((tm, onsa.shape[1]), row),
                  pl.BlockSpec(wodn.shape, const),
                  pl.BlockSpec(wocf.shape, const),
                  pl.BlockSpec(wonsa.shape, const),
                  pl.BlockSpec(nw.shape, const),
                  pl.BlockSpec((d, dff), const),
                  pl.BlockSpec((d, dff), const),
                  pl.BlockSpec((dff, d), const)],
        out_specs=pl.BlockSpec((tm, d), row),
        out_shape=jax.ShapeDtypeStruct((m, d), F32),
        compiler_params=pltpu.CompilerParams(dimension_semantics=("parallel",), vmem_limit_bytes=VMEM_LIMIT),
        name="mix_ffn",
    )(x2, odn, ocf, onsa, wodn, wocf, wonsa, nw, wg, wu, wd)


CF_ROWS = 64
CF_PAD = 32


def _conformer_kernel(u_ref, dw_ref, db_ref, lnw_ref, lnb_ref, out_ref, hbuf):
    ts = u_ref.shape[1]

    @pl.when(pl.program_id(1) == 0)
    def _():
        hbuf[0:CF_PAD, :] = jnp.zeros((CF_PAD, CF_CH), F32)

    u = u_ref[0]
    hbuf[CF_PAD:CF_PAD + ts, :] = u[:, :CF_CH] * _sigmoid(u[:, CF_CH:])
    first = CF_PAD - (CF_KERNEL - 1)
    for r in range(ts // CF_ROWS):
        acc = jnp.zeros((CF_ROWS, CF_CH), F32) + db_ref[...]
        for phase in range(SUBLANES):
            part = None
            for j in range(CF_KERNEL):
                if (first + j) % SUBLANES != phase:
                    continue
                base = r * CF_ROWS + first + j - phase
                rows = CF_ROWS + (SUBLANES if phase else 0)
                term = dw_ref[j:j + 1, :] * hbuf[pl.ds(base, rows), :]
                part = term if part is None else part + term
            if part is not None:
                acc = acc + part[phase:phase + CF_ROWS, :]
        mu = jnp.mean(acc, axis=-1, keepdims=True)
        cen = acc - mu
        var = jnp.mean(cen * cen, axis=-1, keepdims=True)
        y = cen * lax.rsqrt(var + EPS) * lnw_ref[...] + lnb_ref[...]
        out_ref[0, r * CF_ROWS:(r + 1) * CF_ROWS, :] = _silu(y).astype(out_ref.dtype)
    hbuf[0:CF_PAD, :] = hbuf[ts:ts + CF_PAD, :]


def _conformer(cf_u, dw, db, lnw, lnb, ts):
    b, s, _ = cf_u.shape
    const = lambda i, t: (0, 0)
    return pl.pallas_call(
        _conformer_kernel,
        grid=(b, s // ts),
        in_specs=[pl.BlockSpec((1, ts, 2 * CF_CH), lambda i, t: (i, t, 0)),
                  pl.BlockSpec((CF_KERNEL, CF_CH), const),
                  pl.BlockSpec((1, CF_CH), const),
                  pl.BlockSpec((1, CF_CH), const),
                  pl.BlockSpec((1, CF_CH), const)],
        out_specs=pl.BlockSpec((1, ts, CF_CH), lambda i, t: (i, t, 0)),
        out_shape=jax.ShapeDtypeStruct((b, s, CF_CH), BF16),
        scratch_shapes=[pltpu.VMEM((ts + CF_PAD, CF_CH), F32)],
        compiler_params=pltpu.CompilerParams(dimension_semantics=("parallel", "arbitrary")),
        name="conformer",
    )(cf_u, dw, db, lnw, lnb)


DN_PAD = 8


def _deltanet_kernel(qkv_ref, z_ref, sm_ref, cw_ref, alog_ref, dtb_ref, nw_ref, out_ref, xbuf, state):
    n = SUPER
    nchunk = n // DN_CHUNK

    @pl.when(pl.program_id(1) == 0)
    def _():
        xbuf[0:DN_PAD, :] = jnp.zeros((DN_PAD, DN_QKV), F32)
        state[...] = jnp.zeros(state.shape, F32)

    xbuf[DN_PAD:DN_PAD + n, :] = qkv_ref[0]
    first = DN_PAD - (DN_CONV - 1)
    y = cw_ref[0:1, :] * xbuf[pl.ds(first, n), :]
    for j in range(1, DN_CONV):
        y = y + cw_ref[j:j + 1, :] * xbuf[pl.ds(first + j, n), :]
    xbuf[0:DN_PAD, :] = xbuf[n:n + DN_PAD, :]
    y = _silu(y)
    q_all = y[:, 0:DN_QK]
    k_all = y[:, DN_QK:2 * DN_QK]
    v_all = y[:, 2 * DN_QK:]

    hr = lax.broadcasted_iota(jnp.int32, (DN_QK, DN_QK), 0) // DN_DK
    hc = lax.broadcasted_iota(jnp.int32, (DN_QK, DN_QK), 1) // DN_DK
    head_ones = (hr == hc).astype(BF16)
    q_n = q_all * lax.rsqrt(_dot_x01(q_all * q_all, head_ones, 1) + EPS) * (DN_DK ** -0.5)
    k_n = k_all * lax.rsqrt(_dot_x01(k_all * k_all, head_ones, 1) + EPS)

    sm = sm_ref[0]
    beta_all = _sigmoid(sm)
    sp_in = sm + dtb_ref[...]
    softplus = jnp.maximum(sp_in, 0.0) + jnp.log1p(jnp.exp(-jnp.abs(sp_in)))
    g_all = -jnp.exp(alog_ref[...]) * softplus
    ri = lax.broadcasted_iota(jnp.int32, (n, n), 0)
    ci = lax.broadcasted_iota(jnp.int32, (n, n), 1)
    same_chunk = (ri // DN_CHUNK) == (ci // DN_CHUNK)
    causal = same_chunk & (ri >= ci)
    strict = same_chunk & (ri > ci)
    gam = _dot_01x(causal.astype(BF16), g_all, 3)
    glast = _dot_01x(same_chunk.astype(BF16), g_all, 3)
    gam_t = gam.T
    e_gam = jnp.exp(gam)
    e_rest = jnp.exp(glast - gam)
    e_tot = jnp.exp(glast)
    nfac = (DN_CHUNK - 1).bit_length() - 1
    k_t = k_n.T.astype(BF16)

    outs = [None] * DN_HEADS
    for h0 in range(0, DN_HEADS, DN_GROUP):
        heads = range(h0, h0 + DN_GROUP)
        q, k, v, bcol, egcol, ercol, etcol, lmat, qkd = {}, {}, {}, {}, {}, {}, {}, {}, {}
        for h in heads:
            sl = slice(h * DN_DK, (h + 1) * DN_DK)
            lg = LANE_A + h
            q[h], k[h], v[h] = q_n[:, sl], k_n[:, sl], v_all[:, sl]
            bcol[h] = beta_all[:, LANE_BETA + h:LANE_BETA + h + 1]
            egcol[h] = e_gam[:, lg:lg + 1]
            ercol[h] = e_rest[:, lg:lg + 1]
            etcol[h] = e_tot[:, lg:lg + 1]
            kt = k_t[h * DN_DK:(h + 1) * DN_DK, :]
            dec = jnp.exp(jnp.where(causal, gam[:, lg:lg + 1] - gam_t[lg:lg + 1, :], NEG))
            lmat[h] = jnp.where(strict, _dot((k[h] * bcol[h]).astype(BF16), kt) * dec, 0.0)
            qkd[h] = (_dot(q[h].astype(BF16), kt) * dec).astype(BF16)
        ps = {h: -lmat[h] for h in heads}
        mp = {}
        for h in heads:
            lb = lmat[h].astype(BF16)
            mp[h] = _dot(lb, lb)
        for s in range(nfac):
            mb = {h: mp[h].astype(BF16) for h in heads}
            prod = {h: _dot(ps[h].astype(BF16), mb[h]) for h in heads}
            ps = {h: ps[h] + mp[h] + prod[h] for h in heads}
            if s < nfac - 1:
                mp = {h: _dot(mb[h], mb[h]) for h in heads}
        u, w, qeff, o_intra, kd = {}, {}, {}, {}, {}
        for h in heads:
            rhs = jnp.concatenate([v[h] * bcol[h], k[h] * (bcol[h] * egcol[h])], axis=1)
            x = rhs + _dot(ps[h].astype(BF16), rhs.astype(BF16))
            qx = _dot(qkd[h], x.astype(BF16))
            u[h], w[h] = x[:, :DN_DV], x[:, DN_DV:]
            qeff[h] = q[h] * egcol[h] - qx[:, DN_DV:]
            o_intra[h] = qx[:, :DN_DV]
            kd[h] = k[h] * ercol[h]
        st = {h: state[h] for h in heads}
        o_chunks = {h: [] for h in heads}
        for c in range(nchunk):
            rs = slice(c * DN_CHUNK, (c + 1) * DN_CHUNK)
            ws = {h: _dot(jnp.concatenate([w[h][rs], qeff[h][rs]], axis=0).astype(BF16), st[h].astype(BF16))
                  for h in heads}
            for h in heads:
                vnew = u[h][rs] - ws[h][:DN_CHUNK]
                o_chunks[h].append(ws[h][DN_CHUNK:] + o_intra[h][rs])
                st[h] = st[h] * etcol[h][rs] + _dot_tn(kd[h][rs].astype(BF16), vnew.astype(BF16))
        for h in heads:
            state[h] = st[h]
            outs[h] = jnp.concatenate(o_chunks[h], axis=0)
    o = jnp.concatenate(outs, axis=1)
    ms = _dot_x01(o * o, head_ones, 1) * (1.0 / DN_DV)
    o = o * lax.rsqrt(ms + EPS) * nw_ref[...]
    out_ref[0] = (o * _silu(z_ref[0])).astype(out_ref.dtype)


def _deltanet(qkv, z, small, cw, alog_row, dtb_row, nw_row):
    b, s, _ = qkv.shape
    const = lambda i, t: (0, 0)
    tile = lambda i, t: (i, t, 0)
    return pl.pallas_call(
        _deltanet_kernel,
        grid=(b, s // SUPER),
        in_specs=[pl.BlockSpec((1, SUPER, DN_QKV), tile),
                  pl.BlockSpec((1, SUPER, DN_QK), tile),
                  pl.BlockSpec((1, SUPER, LANES), tile),
                  pl.BlockSpec((DN_CONV, DN_QKV), const),
                  pl.BlockSpec((1, LANES), const),
                  pl.BlockSpec((1, LANES), const),
                  pl.BlockSpec((1, DN_QK), const)],
        out_specs=pl.BlockSpec((1, SUPER, DN_QK), tile),
        out_shape=jax.ShapeDtypeStruct((b, s, DN_QK), BF16),
        scratch_shapes=[pltpu.VMEM((SUPER + DN_PAD, DN_QKV), F32),
                        pltpu.VMEM((DN_HEADS, DN_DK, DN_DV), F32)],
        compiler_params=pltpu.CompilerParams(dimension_semantics=("parallel", "arbitrary"),
                                             vmem_limit_bytes=VMEM_LIMIT),
        name="deltanet",
    )(qkv, z, small, cw, alog_row, dtb_row, nw_row)


def _compress(t_ref, posa_ref, posb_ref, w1a_ref, w1b_ref, w2_ref):
    t = t_ref[0]
    nrow = t.shape[0]
    p = _dot((t + posa_ref[...]).astype(BF16), w1a_ref[...])
    q = _dot((t + posb_ref[...]).astype(BF16), w1b_ref[...])
    hid = _silu(p + pltpu.roll(q, nrow - 1, 0))
    return _dot(hid.astype(BF16), w2_ref[...])


def _store_vt(dst, v_ref):
    vt = v_ref[0].T
    ones = jnp.ones((BF16_ROWS, TQ), BF16)
    for g in range(NSA_KV):
        for c in range(dst.shape[1]):
            blk = vt[g * NSA_DH:(g + 1) * NSA_DH, c * TQ:(c + 1) * TQ].astype(BF16)
            dst[g, c] = jnp.concatenate([blk, ones], axis=0)


def _attend_chunks(chunks, slots, key_chunk, queries, vt_chunk, m_s, acc_s):
    scores = [[_dot(key_chunk(j), q) for q in queries] for j, _ in chunks]
    for (j, mask), chunk_scores in zip(chunks, scores):
        probs, alphas = [], []
        for slot, s in zip(slots, chunk_scores):
            if mask is not None:
                s = jnp.where(mask, s, NEG)
            m_prev = m_s[slot]
            m_new = jnp.maximum(m_prev, jnp.max(s, axis=0, keepdims=True))
            probs.append(jnp.exp(s - m_new).astype(BF16))
            alphas.append(jnp.exp(m_prev - m_new))
            m_s[slot] = m_new
        pv = [_dot(vt_chunk(n, j), p) for n, p in enumerate(probs)]
        for slot, a, o in zip(slots, alphas, pv):
            acc_s[slot] = a * acc_s[slot] + o


def _nsa_kernel(q_ref, sm_ref, kct_ref, vct_ref, ks_ref, vs_ref, kw_ref, vw_ref,
                kposa_ref, kposb_ref, kw1a_ref, kw1b_ref, kw2_ref,
                vposa_ref, vposb_ref, vw1a_ref, vw1b_ref, vw2_ref,
                out_ref,
                kc_s, vct_s, ksb, kwb, vst_s, vwt_s, qt_s, oc_s, m_s, acc_s):
    i = pl.program_id(1)
    ncmp = kc_s.shape[1]
    nblk = ks_ref.shape[1] // SEL_BLOCK
    heads = range(NSA_HEADS)

    @pl.when(i == 0)
    def _():
        kc = _compress(kct_ref, kposa_ref, kposb_ref, kw1a_ref, kw1b_ref, kw2_ref)
        kc_hi, kc_lo = _split_terms(kc, 2)
        kc_s[0] = kc_hi
        kc_s[1] = kc_lo
        vct_s[...] = _compress(vct_ref, vposa_ref, vposb_ref, vw1a_ref, vw1b_ref, vw2_ref).T.astype(BF16)
        s_len = ks_ref.shape[1]
        key_blk = lax.broadcasted_iota(jnp.int32, (s_len, LANES), 0) // SEL_BLOCK
        ksb[:, 0:LANES] = ks_ref[0].astype(BF16)
        ksb[:, LANES:] = (lax.broadcasted_iota(jnp.int32, (s_len, LANES), 1) == key_blk).astype(BF16)
        kwb[...] = kw_ref[0].astype(BF16)
        for idx in heads:
            qt_s[idx, LANES + nblk:, :] = jnp.zeros((LANES - nblk, TQ), BF16)
        _store_vt(vst_s, vs_ref)
        _store_vt(vwt_s, vw_ref)

    trow = i * TQ + lax.broadcasted_iota(jnp.int32, (1, TQ), 1)
    krel = lax.broadcasted_iota(jnp.int32, (TQ, 1), 0)
    cmp_end = lax.broadcasted_iota(jnp.int32, (ncmp, 1), 0) * CMP_STRIDE + (CMP_LEN - 1)
    valid_c = cmp_end <= trow
    oj = lax.broadcasted_iota(jnp.int32, (nblk, ncmp), 0) * SEL_BLOCK
    on = lax.broadcasted_iota(jnp.int32, (nblk, ncmp), 1) * CMP_STRIDE
    ov_t = ((on < oj + SEL_BLOCK) & (on + CMP_LEN > oj)).astype(BF16)
    jj = lax.broadcasted_iota(jnp.int32, (nblk, TQ), 0)
    cur = trow // SEL_BLOCK
    elig = jj <= cur
    forced = (jj == 0) | (jj == cur) | (jj == cur - 1)

    q_all = q_ref[0] * (NSA_DH ** -0.5)
    kc_hi, kc_lo = kc_s[0], kc_s[1]
    s_c = []
    for idx in heads:
        qt = q_all[:, idx * LANES:(idx + 1) * LANES].T
        qt_hi, qt_lo = _split_terms(qt, 2)
        qt_s[idx, 0:LANES, :] = qt_hi
        s_c.append(_dot(kc_hi, qt_hi) + _dot(kc_hi, qt_lo) + _dot(kc_lo, qt_hi))
    probs = []
    for idx in heads:
        s = jnp.where(valid_c, s_c[idx], NEG)
        e = jnp.where(valid_c, jnp.exp(s - jnp.max(s, axis=0, keepdims=True)), 0.0)
        probs.append(e / jnp.maximum(jnp.sum(e, axis=0, keepdims=True), 1e-30))
    for idx in heads:
        g = idx // NSA_HPG
        oc_s[idx] = _dot(vct_s[g * NSA_DH:(g + 1) * NSA_DH, :], probs[idx].astype(BF16))
    for g in range(NSA_KV):
        psum = sum(probs[g * NSA_HPG:(g + 1) * NSA_HPG])
        imp_t = _dot_01x(ov_t, psum, 3)
        score = jnp.where(elig, imp_t + jnp.where(forced, SEL_BIG, 0.0), -SEL_BIG)
        rank = jnp.zeros((nblk, TQ), F32)
        for r in range(nblk):
            row = score[r:r + 1, :]
            beats = (row > score) | ((row == score) & (r < jj))
            rank = rank + beats.astype(F32)
        bias = jnp.where(rank < float(N_SEL), 0.0, NEG).astype(BF16)
        for hh in range(NSA_HPG):
            qt_s[g * NSA_HPG + hh, LANES:LANES + nblk, :] = bias

    m_s[...] = jnp.full(m_s.shape, NEG, F32)
    acc_s[...] = jnp.zeros(acc_s.shape, F32)
    qrel = lax.broadcasted_iota(jnp.int32, (1, TQ), 1)
    on_or_below = krel <= qrel

    def attend_sel(chunks):
        _attend_chunks(chunks, list(heads), lambda j: ksb[pl.ds(pl.multiple_of(j * TQ, TQ), TQ), :],
                       [qt_s[idx] for idx in heads], lambda n, j: vst_s[n // NSA_HPG, j], m_s, acc_s)

    def sel_pair(p, carry):
        attend_sel([(2 * p, None), (2 * p + 1, None)])
        return carry

    lax.fori_loop(0, i // 2, sel_pair, 0)

    @pl.when(i % 2 == 0)
    def _():
        attend_sel([(i, on_or_below)])

    @pl.when(i % 2 == 1)
    def _():
        attend_sel([(i - 1, None), (i, on_or_below)])

    def attend_win(chunks):
        _attend_chunks(chunks, [NSA_HEADS + idx for idx in heads],
                       lambda j: kwb[pl.ds(pl.multiple_of(j * TQ, TQ), TQ), :],
                       [qt_s[idx, 0:LANES, :] for idx in heads], lambda n, j: vwt_s[n // NSA_HPG, j], m_s, acc_s)

    @pl.when(i == 0)
    def _():
        attend_win([(i, on_or_below)])

    @pl.when(i >= 1)
    def _():
        attend_win([(i, on_or_below), (i - 1, None)])

    @pl.when(i >= 2)
    def _():
        attend_win([(i - 2, krel > qrel)])

    gates_t = _sigmoid(sm_ref[0]).T
    outs = []
    for idx in range(NSA_HEADS):
        acc = acc_s[idx]
        o_s = acc[:NSA_DH] * (1.0 / acc[NSA_DH:NSA_DH + 1])
        acc = acc_s[NSA_HEADS + idx]
        o_w = acc[:NSA_DH] * (1.0 / acc[NSA_DH:NSA_DH + 1])
        lg = LANE_GATE + idx
        outs.append(gates_t[lg:lg + 1] * oc_s[idx] + gates_t[lg + NSA_HEADS:lg + NSA_HEADS + 1] * o_s
                    + gates_t[lg + 2 * NSA_HEADS:lg + 2 * NSA_HEADS + 1] * o_w)
    out_ref[0] = jnp.concatenate(outs, axis=0).T.astype(out_ref.dtype)


def _nsa(nq, small, kct, vct, ks, vs, kw, vw, kparams, vparams):
    b, s, _ = nq.shape
    ncmp = kct.shape[1]
    nchunks = s // TQ
    tile = lambda bi, i: (bi, i, 0)
    full = lambda bi, i: (bi, 0, 0)
    const = lambda bi, i: (0, 0)
    wspecs = [pl.BlockSpec(p.shape, const) for p in kparams + vparams]
    return pl.pallas_call(
        _nsa_kernel,
        grid=(b, nchunks),
        in_specs=[pl.BlockSpec((1, TQ, NSA_HEADS * LANES), tile),
                  pl.BlockSpec((1, TQ, LANES), tile),
                  pl.BlockSpec((1, ncmp, kct.shape[2]), full),
                  pl.BlockSpec((1, ncmp, vct.shape[2]), full),
                  pl.BlockSpec((1, s, LANES), full),
                  pl.BlockSpec((1, s, LANES), full),
                  pl.BlockSpec((1, s, LANES), full),
                  pl.BlockSpec((1, s, LANES), full)] + wspecs,
        out_specs=pl.BlockSpec((1, TQ, NSA_HEADS * NSA_DH), tile),
        out_shape=jax.ShapeDtypeStruct((b, s, NSA_HEADS * NSA_DH), BF16),
        scratch_shapes=[pltpu.VMEM((2, ncmp, LANES), BF16),
                        pltpu.VMEM((LANES, ncmp), BF16),
                        pltpu.VMEM((s, 2 * LANES), BF16),
                        pltpu.VMEM((s, LANES), BF16),
                        pltpu.VMEM((NSA_KV, nchunks, VT_ROWS, TQ), BF16),
                        pltpu.VMEM((NSA_KV, nchunks, VT_ROWS, TQ), BF16),
                        pltpu.VMEM((NSA_HEADS, 2 * LANES, TQ), BF16),
                        pltpu.VMEM((NSA_HEADS, NSA_DH, TQ), F32),
                        pltpu.VMEM((2 * NSA_HEADS, 1, TQ), F32),
                        pltpu.VMEM((2 * NSA_HEADS, VT_ROWS, TQ), F32)],
        compiler_params=pltpu.CompilerParams(dimension_semantics=("parallel", "arbitrary"),
                                             vmem_limit_bytes=VMEM_LIMIT),
        name="nsa",
    )(nq, small, kct, vct, ks, vs, kw, vw, *kparams, *vparams)


def _source_columns():
    o = 0
    qkv = list(range(o, o + DN_QKV)); o += DN_QKV
    z = list(range(o, o + DN_HEADS * DN_DV)); o += DN_HEADS * DN_DV
    b_raw = list(range(o, o + DN_HEADS)); o += DN_HEADS
    a_raw = list(range(o, o + DN_HEADS)); o += DN_HEADS
    cf_u = list(range(o, o + 2 * CF_CH)); o += 2 * CF_CH
    n_q = list(range(o, o + NSA_HEADS * NSA_DH)); o += NSA_HEADS * NSA_DH
    kv = []
    for _ in range(6):
        kv.append(list(range(o, o + NSA_KV * NSA_DH))); o += NSA_KV * NSA_DH
    gate = list(range(o, o + 3 * NSA_HEADS)); o += 3 * NSA_HEADS
    n_kc, n_vc, n_ks, n_vs, n_kw, n_vw = kv
    small = b_raw + a_raw + gate
    small = small + [-1] * (LANES - len(small))
    nq = []
    zero = [-1] * NSA_DH
    for idx in range(NSA_HEADS):
        qh = n_q[idx * NSA_DH:(idx + 1) * NSA_DH]
        nq += (qh + zero) if idx // NSA_HPG == 0 else (zero + qh)
    cols = qkv + z + small + cf_u + nq + n_ks + n_vs + n_kw + n_vw + n_kc + n_vc
    assert len(cols) == sum(SEC_WIDTHS)
    return np.asarray(cols, np.int32), o


def _permute_w_in(w):
    cols, n_in = _source_columns()
    assert w.shape[1] == n_in
    select = (lax.broadcasted_iota(jnp.int32, (n_in, cols.shape[0]), 0) == cols[None, :]).astype(BF16)
    return jnp.dot(w.astype(BF16), select, preferred_element_type=BF16)


def _compress_params(pos, w1, w2):
    eye = jnp.eye(NSA_KV, dtype=F32)
    half = CMP_LEN // 2
    w1r = w1.reshape(CMP_LEN, NSA_DH, CMP_HID)

    def expand_w1(part):
        return jnp.einsum("ldj,gh->lgdhj", part, eye).reshape(half * NSA_KV * NSA_DH, NSA_KV * CMP_HID).astype(BF16)

    def expand_pos(part):
        return jnp.broadcast_to(part[:, None, :], (half, NSA_KV, NSA_DH)).reshape(1, half * NSA_KV * NSA_DH)

    w2e = jnp.einsum("jd,gh->gjhd", w2, eye).reshape(NSA_KV * CMP_HID, NSA_KV * NSA_DH).astype(BF16)
    return [expand_pos(pos[:half]), expand_pos(pos[half:]), expand_w1(w1r[:half]), expand_w1(w1r[half:]), w2e]


def _lane_row(vals, first_lane):
    pad = jnp.zeros((LANES - first_lane - vals.shape[0],), F32)
    return jnp.concatenate([jnp.zeros((first_lane,), F32), vals.astype(F32), pad])[None, :]


def kernel(x, norm_w, w_in, dn_conv_w, dn_a_log, dn_dt_bias, dn_norm_w, cf_dw_w, cf_dw_b, cf_ln_w, cf_ln_b,
           nsa_k_pos, nsa_v_pos, nsa_k_w1, nsa_k_w2, nsa_v_w1, nsa_v_w2, w_out, ffn_w_gate, ffn_w_up, ffn_w_down):
    b, s, d = x.shape
    depth = w_in.shape[0]
    assert s % SUPER == 0 and s % TQ == 0 and s // CMP_STRIDE == LANES and s // SEL_BLOCK <= LANES
    assert WINDOW == 2 * TQ and TQ % SEL_BLOCK == 0
    m = b * s
    tm = 512 if m % 512 == 0 else 256
    ts_cf = 512 if s % 512 == 0 else 256
    x2 = x.reshape(m, d)
    ndn = DN_HEADS * DN_DV
    rows16 = s // CMP_STRIDE
    for l in range(depth):
        secs = _in_proj(x2, norm_w[l, 0:1], _permute_w_in(w_in[l]), tm)
        qkv, z, small, cf_u, nq, n_ks, n_vs, n_kw, n_vw = [t.reshape(b, s, t.shape[1]) for t in secs[:N_ROWMAJOR]]
        n_kct, n_vct = [t.reshape(b, rows16, t.shape[1]) for t in secs[N_ROWMAJOR:]]
        o_dn = _deltanet(qkv, z, small, dn_conv_w[l], _lane_row(dn_a_log[l], LANE_A), _lane_row(dn_dt_bias[l], LANE_A),
                         jnp.tile(dn_norm_w[l], DN_HEADS)[None, :])
        o_cf = _conformer(cf_u, cf_dw_w[l], cf_dw_b[l][None, :], cf_ln_w[l][None, :], cf_ln_b[l][None, :], ts_cf)
        o_nsa = _nsa(nq, small, n_kct, n_vct, n_ks, n_vs, n_kw, n_vw,
                     _compress_params(nsa_k_pos[l], nsa_k_w1[l], nsa_k_w2[l]),
                     _compress_params(nsa_v_pos[l], nsa_v_w1[l], nsa_v_w2[l]))
        wo = w_out[l].astype(BF16)
        x2 = _mix_ffn(x2, o_dn.reshape(m, ndn), o_cf.reshape(m, CF_CH), o_nsa.reshape(m, NSA_HEADS * NSA_DH),
                      wo[:ndn], wo[ndn:ndn + CF_CH], wo[ndn + CF_CH:], norm_w[l],
                      ffn_w_gate[l].astype(BF16), ffn_w_up[l].astype(BF16), ffn_w_down[l].astype(BF16), tm)
    return x2.reshape(b, s, d)
```

```python
import jax
import jax.numpy as jnp
import numpy as np
from jax import lax
from jax.experimental import pallas as pl
from jax.experimental.pallas import tpu as pltpu

F32 = jnp.float32
BF16 = jnp.bfloat16
HIGHEST = lax.Precision.HIGHEST

DN_HEADS = 6
DN_DK = 64
DN_DV = 64
DN_CONV = 4
DN_CHUNK = 64
CF_CH = 256
CF_KERNEL = 31
NSA_HEADS = 6
NSA_KV = 2
NSA_HPG = NSA_HEADS // NSA_KV
NSA_DH = 64
CMP_LEN = 32
CMP_STRIDE = 16
CMP_HID = 2 * NSA_DH
SEL_BLOCK = 64
N_SEL = 8
WINDOW = 512
EPS = 1e-6
NEG = -1e30
SEL_BIG = 1e4

LANES = 128
SUBLANES = 8
BF16_ROWS = 16
DN_QK = DN_HEADS * DN_DK
DN_QKV = 2 * DN_QK + DN_HEADS * DN_DV
SUPER = 256
DN_GROUP = 6
TQ = 256
VT_ROWS = NSA_DH + BF16_ROWS
VMEM_LIMIT = 56 * 1024 * 1024

SEC_WIDTHS = (DN_QKV, DN_HEADS * DN_DV, LANES, 2 * CF_CH, NSA_HEADS * LANES,
              LANES, LANES, LANES, LANES, LANES, LANES)
N_ROWMAJOR = 9
LANE_BETA = 0
LANE_A = DN_HEADS
LANE_GATE = 2 * DN_HEADS


def _dot(a, b, precision=None):
    return jnp.dot(a, b, preferred_element_type=F32, precision=precision)


def _split_terms(x, terms):
    out = []
    for _ in range(terms):
        hi = x.astype(BF16)
        out.append(hi)
        x = x - hi.astype(F32)
    return out


def _dot_x01(x, mat01, terms):
    return sum(_dot(p, mat01) for p in _split_terms(x, terms))


def _dot_01x(mat01, x, terms):
    return sum(_dot(mat01, p) for p in _split_terms(x, terms))


def _sigmoid(x):
    return 1.0 / (1.0 + jnp.exp(-x))


def _silu(x):
    return x * _sigmoid(x)


def _rms(x, w):
    return x * lax.rsqrt(jnp.mean(x * x, axis=-1, keepdims=True) + EPS) * w


def _in_proj_kernel(x_ref, nw_ref, w_ref, *refs):
    out_refs, stage = refs[:-1], refs[-1]
    h = _rms(x_ref[...], nw_ref[...]).astype(BF16)
    rows16 = x_ref.shape[0] // CMP_STRIDE
    off = 0
    for k, (o_ref, wd) in enumerate(zip(out_refs, SEC_WIDTHS)):
        y = _dot(h, w_ref[:, off:off + wd])
        off += wd
        if k < N_ROWMAJOR:
            o_ref[...] = y
        else:
            stage[...] = y
            for t in range(CMP_STRIDE):
                o_ref[:, t * LANES:(t + 1) * LANES] = stage[pl.ds(t, rows16, stride=CMP_STRIDE), :]


def _in_proj(x2, nw, w_perm, tm):
    m, d = x2.shape
    n = w_perm.shape[1]
    shapes = [(m, wd) if k < N_ROWMAJOR else (m // CMP_STRIDE, CMP_STRIDE * wd) for k, wd in enumerate(SEC_WIDTHS)]
    blocks = [(tm, wd) if k < N_ROWMAJOR else (tm // CMP_STRIDE, CMP_STRIDE * wd) for k, wd in enumerate(SEC_WIDTHS)]
    return pl.pallas_call(
        _in_proj_kernel,
        grid=(m // tm,),
        in_specs=[pl.BlockSpec((tm, d), lambda i: (i, 0)),
                  pl.BlockSpec((1, d), lambda i: (0, 0)),
                  pl.BlockSpec((d, n), lambda i: (0, 0))],
        out_specs=[pl.BlockSpec(blk, lambda i: (i, 0)) for blk in blocks],
        out_shape=[jax.ShapeDtypeStruct(shp, F32) for shp in shapes],
        scratch_shapes=[pltpu.VMEM((tm, LANES), F32)],
        compiler_params=pltpu.CompilerParams(dimension_semantics=("parallel",), vmem_limit_bytes=VMEM_LIMIT),
        name="in_proj",
    )(x2, nw, w_perm)


def _mix_ffn_kernel(x_ref, odn_ref, ocf_ref, onsa_ref, wodn_ref, wocf_ref, wonsa_ref, nw_ref,
                    wg_ref, wu_ref, wd_ref, out_ref):
    mix = (_dot(odn_ref[...], wodn_ref[...]) + _dot(ocf_ref[...], wocf_ref[...])
           + _dot(onsa_ref[...], wonsa_ref[...]))
    x1 = x_ref[...] + _rms(mix, nw_ref[1:2, :])
    h = _rms(x1, nw_ref[2:3, :]).astype(BF16)
    g = _dot(h, wg_ref[...])
    u = _dot(h, wu_ref[...])
    a = (_silu(g) * u).astype(BF16)
    f = _dot(a, wd_ref[...])
    out_ref[...] = x1 + _rms(f, nw_ref[3:4, :])


def _mix_ffn(x2, odn, ocf, onsa, wodn, wocf, wonsa, nw, wg, wu, wd, tm):
    m, d = x2.shape
    dff = wg.shape[1]
    const = lambda i: (0, 0)
    row = lambda i: (i, 0)
    return pl.pallas_call(
        _mix_ffn_kernel,
        grid=(m // tm,),
        in_specs=[pl.BlockSpec((tm, d), row),
                  pl.BlockSpec((tm, odn.shape[1]), row),
                  pl.BlockSpec((tm, ocf.shape[1]), row),
                  pl.BlockSpec((tm, onsa.shape[1]), row),
                  pl.BlockSpec(wodn.shape, const),
                  pl.BlockSpec(wocf.shape, const),
                  pl.BlockSpec(wonsa.shape, const),
                  pl.BlockSpec(nw.shape, const),
                  pl.BlockSpec((d, dff), const),
                  pl.BlockSpec((d, dff), const),
                  pl.BlockSpec((dff, d), const)],
        out_specs=pl.BlockSpec((tm, d), row),
        out_shape=jax.ShapeDtypeStruct((m, d), F32),
        compiler_params=pltpu.CompilerParams(dimension_semantics=("parallel",), vmem_limit_bytes=VMEM_LIMIT),
        name="mix_ffn",
    )(x2, odn, ocf, onsa, wodn, wocf, wonsa, nw, wg, wu, wd)


CF_ROWS = 64
CF_PAD = 32


def _conformer_kernel(u_ref, dw_ref, db_ref, lnw_ref, lnb_ref, out_ref, hbuf):
    ts = u_ref.shape[1]

    @pl.when(pl.program_id(1) == 0)
    def _():
        hbuf[0:CF_PAD, :] = jnp.zeros((CF_PAD, CF_CH), F32)

    u = u_ref[0]
    hbuf[CF_PAD:CF_PAD + ts, :] = u[:, :CF_CH] * _sigmoid(u[:, CF_CH:])
    first = CF_PAD - (CF_KERNEL - 1)
    for r in range(ts // CF_ROWS):
        acc = jnp.zeros((CF_ROWS, CF_CH), F32) + db_ref[...]
        for phase in range(SUBLANES):
            part = None
            for j in range(CF_KERNEL):
                if (first + j) % SUBLANES != phase:
                    continue
                base = r * CF_ROWS + first + j - phase
                rows = CF_ROWS + (SUBLANES if phase else 0)
                term = dw_ref[j:j + 1, :] * hbuf[pl.ds(base, rows), :]
                part = term if part is None else part + term
            if part is not None:
                acc = acc + part[phase:phase + CF_ROWS, :]
        mu = jnp.mean(acc, axis=-1, keepdims=True)
        cen = acc - mu
        var = jnp.mean(cen * cen, axis=-1, keepdims=True)
        y = cen * lax.rsqrt(var + EPS) * lnw_ref[...] + lnb_ref[...]
        out_ref[0, r * CF_ROWS:(r + 1) * CF_ROWS, :] = _silu(y).astype(out_ref.dtype)
    hbuf[0:CF_PAD, :] = hbuf[ts:ts + CF_PAD, :]


def _conformer(cf_u, dw, db, lnw, lnb, ts):
    b, s, _ = cf_u.shape
    const = lambda i, t: (0, 0)
    return pl.pallas_call(
        _conformer_kernel,
        grid=(b, s // ts),
        in_specs=[pl.BlockSpec((1, ts, 2 * CF_CH), lambda i, t: (i, t, 0)),
                  pl.BlockSpec((CF_KERNEL, CF_CH), const),
                  pl.BlockSpec((1, CF_CH), const),
                  pl.BlockSpec((1, CF_CH), const),
                  pl.BlockSpec((1, CF_CH), const)],
        out_specs=pl.BlockSpec((1, ts, CF_CH), lambda i, t: (i, t, 0)),
        out_shape=jax.ShapeDtypeStruct((b, s, CF_CH), BF16),
        scratch_shapes=[pltpu.VMEM((ts + CF_PAD, CF_CH), F32)],
        compiler_params=pltpu.CompilerParams(dimension_semantics=("parallel", "arbitrary")),
        name="conformer",
    )(cf_u, dw, db, lnw, lnb)


DN_PAD = 8


def _deltanet_kernel(qkv_ref, z_ref, sm_ref, cw_ref, alog_ref, dtb_ref, nw_ref, out_ref, xbuf, ybuf, state):
    n = SUPER
    nchunk = n // DN_CHUNK

    @pl.when(pl.program_id(1) == 0)
    def _():
        xbuf[0:DN_PAD, :] = jnp.zeros((DN_PAD, DN_QKV), F32)
        state[...] = jnp.zeros(state.shape, F32)

    xbuf[DN_PAD:DN_PAD + n, :] = qkv_ref[0]
    first = DN_PAD - (DN_CONV - 1)
    y = cw_ref[DN_CONV - 1:DN_CONV, :] * xbuf[pl.ds(DN_PAD, n), :]
    for j in range(DN_CONV - 1):
        y = y + cw_ref[j:j + 1, :] * xbuf[pl.ds(first + j, n), :]
    xbuf[0:DN_PAD, :] = xbuf[n:n + DN_PAD, :]
    ybuf[...] = _silu(y)
    q_all = ybuf[:, 0:DN_QK]
    k_all = ybuf[:, DN_QK:2 * DN_QK]
    v_all = ybuf[:, 2 * DN_QK:]

    hr = lax.broadcasted_iota(jnp.int32, (DN_QK, DN_QK), 0) // DN_DK
    hc = lax.broadcasted_iota(jnp.int32, (DN_QK, DN_QK), 1) // DN_DK
    head_ones = (hr == hc).astype(BF16)
    q_n = q_all * lax.rsqrt(_dot_x01(q_all * q_all, head_ones, 1) + EPS) * (DN_DK ** -0.5)
    k_n = k_all * lax.rsqrt(_dot_x01(k_all * k_all, head_ones, 1) + EPS)

    sm = sm_ref[0]
    beta_all = _sigmoid(sm)
    sp_in = sm + dtb_ref[...]
    softplus = jnp.maximum(sp_in, 0.0) + jnp.log1p(jnp.exp(-jnp.abs(sp_in)))
    g_all = -jnp.exp(alog_ref[...]) * softplus
    ri = lax.broadcasted_iota(jnp.int32, (n, n), 0)
    ci = lax.broadcasted_iota(jnp.int32, (n, n), 1)
    same_chunk = (ri // DN_CHUNK) == (ci // DN_CHUNK)
    causal = same_chunk & (ri >= ci)
    strict = same_chunk & (ri > ci)
    gam = _dot_01x(causal.astype(BF16), g_all, 3)
    glast = _dot_01x(same_chunk.astype(BF16), g_all, 3)
    gam_t = gam.T
    e_gam = jnp.exp(gam)
    e_rest = jnp.exp(glast - gam)
    e_tot = jnp.exp(glast)
    nfac = (DN_CHUNK - 1).bit_length() - 1
    k_t = k_n.T
    e_rest_t = e_rest.T
    lane_chunk = lax.broadcasted_iota(jnp.int32, (1, n), 1) // DN_CHUNK
    in_chunk = [lane_chunk == c for c in range(nchunk)]
    in_chunk_bf = [m.astype(BF16) for m in in_chunk]

    def to_rowform(bd):
        return sum(jnp.where(in_chunk[c], bd[c * DN_CHUNK:(c + 1) * DN_CHUNK, :], 0.0) for c in range(nchunk))

    def to_blockdiag(rowform):
        return jnp.concatenate([rowform * in_chunk_bf[c] for c in range(nchunk)], axis=0)

    outs = [None] * DN_HEADS
    for h0 in range(0, DN_HEADS, DN_GROUP):
        heads = range(h0, h0 + DN_GROUP)
        q, k, v, bcol, egcol, etcol, lmat, qkd = {}, {}, {}, {}, {}, {}, {}, {}
        for h in heads:
            sl = slice(h * DN_DK, (h + 1) * DN_DK)
            lg = LANE_A + h
            q[h], k[h], v[h] = q_n[:, sl], k_n[:, sl], v_all[:, sl]
            bcol[h] = beta_all[:, LANE_BETA + h:LANE_BETA + h + 1]
            egcol[h] = e_gam[:, lg:lg + 1]
            etcol[h] = e_tot[:, lg:lg + 1]
            kt = k_t[h * DN_DK:(h + 1) * DN_DK, :].astype(BF16)
            dec = jnp.exp(jnp.where(causal, gam[:, lg:lg + 1] - gam_t[lg:lg + 1, :], NEG))
            lmat[h] = jnp.where(strict, _dot((k[h] * bcol[h]).astype(BF16), kt) * dec, 0.0)
            qkd[h] = (_dot(q[h].astype(BF16), kt) * dec).astype(BF16)
        ps = {h: -to_rowform(lmat[h]) for h in heads}
        mp = {h: _dot((-ps[h]).astype(BF16), lmat[h].astype(BF16)) for h in heads}
        for s in range(nfac):
            mb = {h: mp[h].astype(BF16) for h in heads}
            mbd = {h: to_blockdiag(mb[h]) for h in heads}
            prod = {h: _dot(ps[h].astype(BF16), mbd[h]) for h in heads}
            ps = {h: ps[h] + mp[h] + prod[h] for h in heads}
            if s < nfac - 1:
                mp = {h: _dot(mb[h], mbd[h]) for h in heads}
        qeff, o_intra, b_in, n_in = {}, {}, {}, {}
        rhs = {h: jnp.concatenate([v[h] * bcol[h], k[h] * (bcol[h] * egcol[h])], axis=1) for h in heads}
        tx = {h: _dot(to_blockdiag(ps[h].astype(BF16)), rhs[h].astype(BF16)) for h in heads}
        x = {h: (rhs[h] + tx[h]).astype(BF16) for h in heads}
        qx = {h: _dot(qkd[h], x[h]) for h in heads}
        kdt = {h: (k_t[h * DN_DK:(h + 1) * DN_DK, :] * e_rest_t[LANE_A + h:LANE_A + h + 1, :]).astype(BF16)
               for h in heads}
        kx = {h: _dot(to_blockdiag(kdt[h]), x[h]) for h in heads}
        for h in heads:
            qeff[h] = (q[h] * egcol[h] - qx[h][:, DN_DV:]).astype(BF16)
            o_intra[h] = qx[h][:, :DN_DV]
            b_in[h], n_in[h] = kx[h][:, :DN_DV], kx[h][:, DN_DV:].astype(BF16)
        st = {h: state[h] for h in heads}
        o_chunks = {h: [] for h in heads}
        for c in range(nchunk):
            rs = slice(c * DN_CHUNK, (c + 1) * DN_CHUNK)
            ks = slice(c * DN_DK, (c + 1) * DN_DK)
            stb = {h: st[h].astype(BF16) for h in heads}
            corr = {h: _dot(n_in[h][ks], stb[h]) for h in heads}
            for h in heads:
                o_chunks[h].append(_dot(qeff[h][rs], stb[h]) + o_intra[h][rs])
                st[h] = st[h] * etcol[h][rs] + b_in[h][ks] - corr[h]
        for h in heads:
            state[h] = st[h]
            outs[h] = jnp.concatenate(o_chunks[h], axis=0)
    o = jnp.concatenate(outs, axis=1)
    ms = _dot_x01(o * o, head_ones, 1) * (1.0 / DN_DV)
    o = o * lax.rsqrt(ms + EPS) * nw_ref[...]
    out_ref[0] = (o * _silu(z_ref[0])).astype(out_ref.dtype)


def _deltanet(qkv, z, small, cw, alog_row, dtb_row, nw_row):
    b, s, _ = qkv.shape
    const = lambda i, t: (0, 0)
    tile = lambda i, t: (i, t, 0)
    return pl.pallas_call(
        _deltanet_kernel,
        grid=(b, s // SUPER),
        in_specs=[pl.BlockSpec((1, SUPER, DN_QKV), tile),
                  pl.BlockSpec((1, SUPER, DN_QK), tile),
                  pl.BlockSpec((1, SUPER, LANES), tile),
                  pl.BlockSpec((DN_CONV, DN_QKV), const),
                  pl.BlockSpec((1, LANES), const),
                  pl.BlockSpec((1, LANES), const),
                  pl.BlockSpec((1, DN_QK), const)],
        out_specs=pl.BlockSpec((1, SUPER, DN_QK), tile),
        out_shape=jax.ShapeDtypeStruct((b, s, DN_QK), BF16),
        scratch_shapes=[pltpu.VMEM((SUPER + DN_PAD, DN_QKV), F32),
                        pltpu.VMEM((SUPER, DN_QKV), F32),
                        pltpu.VMEM((DN_HEADS, DN_DK, DN_DV), F32)],
        compiler_params=pltpu.CompilerParams(dimension_semantics=("parallel", "arbitrary"),
                                             vmem_limit_bytes=VMEM_LIMIT),
        name="deltanet",
    )(qkv, z, small, cw, alog_row, dtb_row, nw_row)


def _compress(t_ref, posa_ref, posb_ref, w1a_ref, w1b_ref, w2_ref):
    t = t_ref[0]
    nrow = t.shape[0]
    p = _dot((t + posa_ref[...]).astype(BF16), w1a_ref[...])
    q = _dot((t + posb_ref[...]).astype(BF16), w1b_ref[...])
    hid = _silu(p + pltpu.roll(q, nrow - 1, 0))
    return _dot(hid.astype(BF16), w2_ref[...])


def _store_vt(dst, v_ref):
    vt = v_ref[0].T
    ones = jnp.ones((BF16_ROWS, TQ), BF16)
    for g in range(NSA_KV):
        for c in range(dst.shape[1]):
            blk = vt[g * NSA_DH:(g + 1) * NSA_DH, c * TQ:(c + 1) * TQ].astype(BF16)
            dst[g, c] = jnp.concatenate([blk, ones], axis=0)


def _attend_chunks(chunks, slots, key_chunk, queries, vt_chunk, m_s, acc_s):
    scores = [[_dot(key_chunk(j), q) for q in queries] for j, _ in chunks]
    for (j, mask), chunk_scores in zip(chunks, scores):
        probs, alphas = [], []
        for slot, s in zip(slots, chunk_scores):
            if mask is not None:
                s = jnp.where(mask, s, NEG)
            m_prev = m_s[slot]
            m_new = jnp.maximum(m_prev, jnp.max(s, axis=0, keepdims=True))
            probs.append(jnp.exp(s - m_new).astype(BF16))
            alphas.append(jnp.exp(m_prev - m_new))
            m_s[slot] = m_new
        pv = [_dot(vt_chunk(n, j), p) for n, p in enumerate(probs)]
        for slot, a, o in zip(slots, alphas, pv):
            acc_s[slot] = a * acc_s[slot] + o


def _nsa_kernel(q_ref, sm_ref, kct_ref, vct_ref, ks_ref, vs_ref, kw_ref, vw_ref,
                kposa_ref, kposb_ref, kw1a_ref, kw1b_ref, kw2_ref,
                vposa_ref, vposb_ref, vw1a_ref, vw1b_ref, vw2_ref,
                out_ref,
                kc_s, vct_s, ksb, kwb, vst_s, vwt_s, qt_s, oc_s, m_s, acc_s):
    i = pl.program_id(1)
    ncmp = kc_s.shape[1]
    nblk = ks_ref.shape[1] // SEL_BLOCK
    heads = range(NSA_HEADS)

    @pl.when(i == 0)
    def _():
        kc = _compress(kct_ref, kposa_ref, kposb_ref, kw1a_ref, kw1b_ref, kw2_ref)
        kc_hi, kc_lo = _split_terms(kc, 2)
        kc_s[0] = kc_hi
        kc_s[1] = kc_lo
        vct_s[...] = _compress(vct_ref, vposa_ref, vposb_ref, vw1a_ref, vw1b_ref, vw2_ref).T.astype(BF16)
        s_len = ks_ref.shape[1]
        key_blk = lax.broadcasted_iota(jnp.int32, (s_len, LANES), 0) // SEL_BLOCK
        ksb[:, 0:LANES] = ks_ref[0].astype(BF16)
        ksb[:, LANES:] = (lax.broadcasted_iota(jnp.int32, (s_len, LANES), 1) == key_blk).astype(BF16)
        kwb[...] = kw_ref[0].astype(BF16)
        for idx in heads:
            qt_s[idx, LANES + nblk:, :] = jnp.zeros((LANES - nblk, TQ), BF16)
        _store_vt(vst_s, vs_ref)
        _store_vt(vwt_s, vw_ref)

    trow = i * TQ + lax.broadcasted_iota(jnp.int32, (1, TQ), 1)
    krel = lax.broadcasted_iota(jnp.int32, (TQ, 1), 0)
    cmp_end = lax.broadcasted_iota(jnp.int32, (ncmp, 1), 0) * CMP_STRIDE + (CMP_LEN - 1)
    valid_c = cmp_end <= trow
    oj = lax.broadcasted_iota(jnp.int32, (nblk, ncmp), 0) * SEL_BLOCK
    on = lax.broadcasted_iota(jnp.int32, (nblk, ncmp), 1) * CMP_STRIDE
    ov_t = ((on < oj + SEL_BLOCK) & (on + CMP_LEN > oj)).astype(BF16)
    jj = lax.broadcasted_iota(jnp.int32, (nblk, TQ), 0)
    cur = trow // SEL_BLOCK
    elig = jj <= cur
    forced = (jj == 0) | (jj == cur) | (jj == cur - 1)

    q_all = q_ref[0] * (NSA_DH ** -0.5)
    kc_hi, kc_lo = kc_s[0], kc_s[1]
    s_c = []
    for idx in heads:
        qt = q_all[:, idx * LANES:(idx + 1) * LANES].T
        qt_hi, qt_lo = _split_terms(qt, 2)
        qt_s[idx, 0:LANES, :] = qt_hi
        s_c.append(_dot(kc_hi, qt_hi) + _dot(kc_hi, qt_lo) + _dot(kc_lo, qt_hi))
    probs = []
    for idx in heads:
        s = jnp.where(valid_c, s_c[idx], NEG)
        e = jnp.where(valid_c, jnp.exp(s - jnp.max(s, axis=0, keepdims=True)), 0.0)
        probs.append(e / jnp.maximum(jnp.sum(e, axis=0, keepdims=True), 1e-30))
    for idx in heads:
        g = idx // NSA_HPG
        oc_s[idx] = _dot(vct_s[g * NSA_DH:(g + 1) * NSA_DH, :], probs[idx].astype(BF16))
    for g in range(NSA_KV):
        psum = sum(probs[g * NSA_HPG:(g + 1) * NSA_HPG])
        imp_t = _dot_01x(ov_t, psum, 3)
        score = jnp.where(elig, imp_t + jnp.where(forced, SEL_BIG, 0.0), -SEL_BIG)
        rank = jnp.zeros((nblk, TQ), F32)
        for r in range(nblk):
            row = score[r:r + 1, :]
            beats = (row > score) | ((row == score) & (r < jj))
            rank = rank + beats.astype(F32)
        bias = jnp.where(rank < float(N_SEL), 0.0, NEG).astype(BF16)
        for hh in range(NSA_HPG):
            qt_s[g * NSA_HPG + hh, LANES:LANES + nblk, :] = bias

    m_s[...] = jnp.full(m_s.shape, NEG, F32)
    acc_s[...] = jnp.zeros(acc_s.shape, F32)
    qrel = lax.broadcasted_iota(jnp.int32, (1, TQ), 1)
    on_or_below = krel <= qrel

    def attend_sel(chunks):
        _attend_chunks(chunks, list(heads), lambda j: ksb[pl.ds(pl.multiple_of(j * TQ, TQ), TQ), :],
                       [qt_s[idx] for idx in heads], lambda n, j: vst_s[n // NSA_HPG, j], m_s, acc_s)

    def sel_pair(p, carry):
        attend_sel([(2 * p, None), (2 * p + 1, None)])
        return carry

    lax.fori_loop(0, i // 2, sel_pair, 0)

    @pl.when(i % 2 == 0)
    def _():
        attend_sel([(i, on_or_below)])

    @pl.when(i % 2 == 1)
    def _():
        attend_sel([(i - 1, None), (i, on_or_below)])

    def attend_win(chunks):
        _attend_chunks(chunks, [NSA_HEADS + idx for idx in heads],
                       lambda j: kwb[pl.ds(pl.multiple_of(j * TQ, TQ), TQ), :],
                       [qt_s[idx, 0:LANES, :] for idx in heads], lambda n, j: vwt_s[n // NSA_HPG, j], m_s, acc_s)

    @pl.when(i == 0)
    def _():
        attend_win([(i, on_or_below)])

    @pl.when(i >= 1)
    def _():
        attend_win([(i, on_or_below), (i - 1, None)])

    @pl.when(i >= 2)
    def _():
        attend_win([(i - 2, krel > qrel)])

    gates_t = _sigmoid(sm_ref[0]).T
    outs = []
    for idx in range(NSA_HEADS):
        acc = acc_s[idx]
        o_s = acc[:NSA_DH] * (1.0 / acc[NSA_DH:NSA_DH + 1])
        acc = acc_s[NSA_HEADS + idx]
        o_w = acc[:NSA_DH] * (1.0 / acc[NSA_DH:NSA_DH + 1])
        lg = LANE_GATE + idx
        outs.append(gates_t[lg:lg + 1] * oc_s[idx] + gates_t[lg + NSA_HEADS:lg + NSA_HEADS + 1] * o_s
                    + gates_t[lg + 2 * NSA_HEADS:lg + 2 * NSA_HEADS + 1] * o_w)
    out_ref[0] = jnp.concatenate(outs, axis=0).T.astype(out_ref.dtype)


def _nsa(nq, small, kct, vct, ks, vs, kw, vw, kparams, vparams):
    b, s, _ = nq.shape
    ncmp = kct.shape[1]
    nchunks = s // TQ
    tile = lambda bi, i: (bi, i, 0)
    full = lambda bi, i: (bi, 0, 0)
    const = lambda bi, i: (0, 0)
    wspecs = [pl.BlockSpec(p.shape, const) for p in kparams + vparams]
    return pl.pallas_call(
        _nsa_kernel,
        grid=(b, nchunks),
        in_specs=[pl.BlockSpec((1, TQ, NSA_HEADS * LANES), tile),
                  pl.BlockSpec((1, TQ, LANES), tile),
                  pl.BlockSpec((1, ncmp, kct.shape[2]), full),
                  pl.BlockSpec((1, ncmp, vct.shape[2]), full),
                  pl.BlockSpec((1, s, LANES), full),
                  pl.BlockSpec((1, s, LANES), full),
                  pl.BlockSpec((1, s, LANES), full),
                  pl.BlockSpec((1, s, LANES), full)] + wspecs,
        out_specs=pl.BlockSpec((1, TQ, NSA_HEADS * NSA_DH), tile),
        out_shape=jax.ShapeDtypeStruct((b, s, NSA_HEADS * NSA_DH), BF16),
        scratch_shapes=[pltpu.VMEM((2, ncmp, LANES), BF16),
                        pltpu.VMEM((LANES, ncmp), BF16),
                        pltpu.VMEM((s, 2 * LANES), BF16),
                        pltpu.VMEM((s, LANES), BF16),
                        pltpu.VMEM((NSA_KV, nchunks, VT_ROWS, TQ), BF16),
                        pltpu.VMEM((NSA_KV, nchunks, VT_ROWS, TQ), BF16),
                        pltpu.VMEM((NSA_HEADS, 2 * LANES, TQ), BF16),
                        pltpu.VMEM((NSA_HEADS, NSA_DH, TQ), F32),
                        pltpu.VMEM((2 * NSA_HEADS, 1, TQ), F32),
                        pltpu.VMEM((2 * NSA_HEADS, VT_ROWS, TQ), F32)],
        compiler_params=pltpu.CompilerParams(dimension_semantics=("parallel", "arbitrary"),
                                             vmem_limit_bytes=VMEM_LIMIT),
        name="nsa",
    )(nq, small, kct, vct, ks, vs, kw, vw, *kparams, *vparams)


def _source_columns():
    o = 0
    qkv = list(range(o, o + DN_QKV)); o += DN_QKV
    z = list(range(o, o + DN_HEADS * DN_DV)); o += DN_HEADS * DN_DV
    b_raw = list(range(o, o + DN_HEADS)); o += DN_HEADS
    a_raw = list(range(o, o + DN_HEADS)); o += DN_HEADS
    cf_u = list(range(o, o + 2 * CF_CH)); o += 2 * CF_CH
    n_q = list(range(o, o + NSA_HEADS * NSA_DH)); o += NSA_HEADS * NSA_DH
    kv = []
    for _ in range(6):
        kv.append(list(range(o, o + NSA_KV * NSA_DH))); o += NSA_KV * NSA_DH
    gate = list(range(o, o + 3 * NSA_HEADS)); o += 3 * NSA_HEADS
    n_kc, n_vc, n_ks, n_vs, n_kw, n_vw = kv
    small = b_raw + a_raw + gate
    small = small + [-1] * (LANES - len(small))
    nq = []
    zero = [-1] * NSA_DH
    for idx in range(NSA_HEADS):
        qh = n_q[idx * NSA_DH:(idx + 1) * NSA_DH]
        nq += (qh + zero) if idx // NSA_HPG == 0 else (zero + qh)
    cols = qkv + z + small + cf_u + nq + n_ks + n_vs + n_kw + n_vw + n_kc + n_vc
    assert len(cols) == sum(SEC_WIDTHS)
    return np.asarray(cols, np.int32), o


def _permute_w_in(w):
    cols, n_in = _source_columns()
    assert w.shape[1] == n_in
    select = (lax.broadcasted_iota(jnp.int32, (n_in, cols.shape[0]), 0) == cols[None, :]).astype(BF16)
    return jnp.dot(w.astype(BF16), select, preferred_element_type=BF16)


def _compress_params(pos, w1, w2):
    eye = jnp.eye(NSA_KV, dtype=F32)
    half = CMP_LEN // 2
    w1r = w1.reshape(CMP_LEN, NSA_DH, CMP_HID)

    def expand_w1(part):
        return jnp.einsum("ldj,gh->lgdhj", part, eye).reshape(half * NSA_KV * NSA_DH, NSA_KV * CMP_HID).astype(BF16)

    def expand_pos(part):
        return jnp.broadcast_to(part[:, None, :], (half, NSA_KV, NSA_DH)).reshape(1, half * NSA_KV * NSA_DH)

    w2e = jnp.einsum("jd,gh->gjhd", w2, eye).reshape(NSA_KV * CMP_HID, NSA_KV * NSA_DH).astype(BF16)
    return [expand_pos(pos[:half]), expand_pos(pos[half:]), expand_w1(w1r[:half]), expand_w1(w1r[half:]), w2e]


def _lane_row(vals, first_lane):
    pad = jnp.zeros((LANES - first_lane - vals.shape[0],), F32)
    return jnp.concatenate([jnp.zeros((first_lane,), F32), vals.astype(F32), pad])[None, :]


def kernel(x, norm_w, w_in, dn_conv_w, dn_a_log, dn_dt_bias, dn_norm_w, cf_dw_w, cf_dw_b, cf_ln_w, cf_ln_b,
           nsa_k_pos, nsa_v_pos, nsa_k_w1, nsa_k_w2, nsa_v_w1, nsa_v_w2, w_out, ffn_w_gate, ffn_w_up, ffn_w_down):
    b, s, d = x.shape
    depth = w_in.shape[0]
    assert s % SUPER == 0 and s % TQ == 0 and s // CMP_STRIDE == LANES and s // SEL_BLOCK <= LANES
    assert WINDOW == 2 * TQ and TQ % SEL_BLOCK == 0
    m = b * s
    tm = 512 if m % 512 == 0 else 256
    ts_cf = 512 if s % 512 == 0 else 256
    x2 = x.reshape(m, d)
    ndn = DN_HEADS * DN_DV
    rows16 = s // CMP_STRIDE
    for l in range(depth):
        secs = _in_proj(x2, norm_w[l, 0:1], _permute_w_in(w_in[l]), tm)
        qkv, z, small, cf_u, nq, n_ks, n_vs, n_kw, n_vw = [t.reshape(b, s, t.shape[1]) for t in secs[:N_ROWMAJOR]]
        n_kct, n_vct = [t.reshape(b, rows16, t.shape[1]) for t in secs[N_ROWMAJOR:]]
        o_dn = _deltanet(qkv, z, small, dn_conv_w[l], _lane_row(dn_a_log[l], LANE_A), _lane_row(dn_dt_bias[l], LANE_A),
                         jnp.tile(dn_norm_w[l], DN_HEADS)[None, :])
        o_cf = _conformer(cf_u, cf_dw_w[l], cf_dw_b[l][None, :], cf_ln_w[l][None, :], cf_ln_b[l][None, :], ts_cf)
        o_nsa = _nsa(nq, small, n_kct, n_vct, n_ks, n_vs, n_kw, n_vw,
                     _compress_params(nsa_k_pos[l], nsa_k_w1[l], nsa_k_w2[l]),
                     _compress_params(nsa_v_pos[l], nsa_v_w1[l], nsa_v_w2[l]))
        wo = w_out[l].astype(BF16)
        x2 = _mix_ffn(x2, o_dn.reshape(m, ndn), o_cf.reshape(m, CF_CH), o_nsa.reshape(m, NSA_HEADS * NSA_DH),
                      wo[:ndn], wo[ndn:ndn + CF_CH], wo[ndn + CF_CH:], norm_w[l],
                      ffn_w_gate[l].astype(BF16), ffn_w_up[l].astype(BF16), ffn_w_down[l].astype(BF16), tm)
    return x2.reshape(b, s, d)
```

```python
import jax
import jax.numpy as jnp
import numpy as np
from jax import lax
from jax.experimental import pallas as pl
from jax.experimental.pallas import tpu as pltpu

F32 = jnp.float32
BF16 = jnp.bfloat16

DN_HEADS = 6
DN_DK = 64
DN_DV = 64
DN_CONV = 4
DN_CHUNK = 64
CF_CH = 256
CF_KERNEL = 31
NSA_HEADS = 6
NSA_KV = 2
NSA_HPG = NSA_HEADS // NSA_KV
NSA_DH = 64
CMP_LEN = 32
CMP_STRIDE = 16
CMP_HID = 2 * NSA_DH
SEL_BLOCK = 64
N_SEL = 8
WINDOW = 512
EPS = 1e-6
NEG = -1e30
SEL_BIG = 1e4

LANES = 128
SUBLANES = 8
BF16_ROWS = 16
DN_QK = DN_HEADS * DN_DK
DN_QKV = 2 * DN_QK + DN_HEADS * DN_DV
SUPER = 256
DN_GROUP = 6
TQ = 256
VT_ROWS = NSA_DH + BF16_ROWS
VMEM_LIMIT = 56 * 1024 * 1024

SEC_WIDTHS = (DN_QKV, DN_HEADS * DN_DV, LANES, 2 * CF_CH, NSA_HEADS * LANES,
              LANES, LANES, LANES, LANES, LANES, LANES)
N_ROWMAJOR = 9
SEC_CF = 3
CF_ROWS = 64
CF_PAD = 32
LANE_BETA = 0
LANE_A = DN_HEADS
LANE_GATE = 2 * DN_HEADS


def _dot(a, b, precision=None):
    return jnp.dot(a, b, preferred_element_type=F32, precision=precision)


def _split_terms(x, terms):
    out = []
    for _ in range(terms):
        hi = x.astype(BF16)
        out.append(hi)
        x = x - hi.astype(F32)
    return out


def _dot_x01(x, mat01, terms):
    return sum(_dot(p, mat01) for p in _split_terms(x, terms))


def _dot_01x(mat01, x, terms):
    return sum(_dot(mat01, p) for p in _split_terms(x, terms))


def _sigmoid(x):
    return 1.0 / (1.0 + jnp.exp(-x))


def _silu(x):
    return x * _sigmoid(x)


def _rms(x, w):
    return x * lax.rsqrt(jnp.mean(x * x, axis=-1, keepdims=True) + EPS) * w


IN_PROJ_COLS = 1280


def _merged_sections():
    runs, first, width = [], 0, 0
    for k, wd in enumerate(SEC_WIDTHS):
        if width and width + wd > IN_PROJ_COLS:
            runs.append((first, k))
            first, width = k, 0
        width += wd
    runs.append((first, len(SEC_WIDTHS)))
    return runs


def _conformer_rows(r, dw_ref, db_ref, lnw_ref, lnb_ref, out_ref, hbuf):
    first = CF_PAD - (CF_KERNEL - 1)
    acc = jnp.zeros((CF_ROWS, CF_CH), F32) + db_ref[...]
    for phase in range(SUBLANES):
        part = None
        for j in range(CF_KERNEL):
            if (first + j) % SUBLANES != phase:
                continue
            base = r * CF_ROWS + first + j - phase
            rows = CF_ROWS + (SUBLANES if phase else 0)
            term = dw_ref[j:j + 1, :] * hbuf[pl.ds(base, rows), :]
            part = term if part is None else part + term
        if part is not None:
            acc = acc + part[phase:phase + CF_ROWS, :]
    mu = jnp.mean(acc, axis=-1, keepdims=True)
    cen = acc - mu
    var = jnp.mean(cen * cen, axis=-1, keepdims=True)
    y = cen * lax.rsqrt(var + EPS) * lnw_ref[...] + lnb_ref[...]
    out_ref[r * CF_ROWS:(r + 1) * CF_ROWS, :] = _silu(y).astype(out_ref.dtype)


def _in_proj_kernel(x_ref, nw_ref, w_ref, dw_ref, db_ref, lnw_ref, lnb_ref, *refs):
    out_refs, stage, hbuf = refs[:-2], refs[-2], refs[-1]
    tm = x_ref.shape[0]

    @pl.when(pl.program_id(1) == 0)
    def _():
        hbuf[0:CF_PAD, :] = jnp.zeros((CF_PAD, CF_CH), F32)

    h = _rms(x_ref[...], nw_ref[...]).astype(BF16)
    rows16 = tm // CMP_STRIDE

    def emit(first, last):
        off = sum(SEC_WIDTHS[:first])
        wide = _dot(h, w_ref[:, off:off + sum(SEC_WIDTHS[first:last])])
        col = 0
        for k in range(first, last):
            y = wide[:, col:col + SEC_WIDTHS[k]]
            col += SEC_WIDTHS[k]
            if k == SEC_CF:
                hbuf[CF_PAD:CF_PAD + tm, :] = y[:, :CF_CH] * _sigmoid(y[:, CF_CH:])
            elif k < N_ROWMAJOR:
                out_refs[k][...] = y
            else:
                stage[...] = y
                for t in range(CMP_STRIDE):
                    out_refs[k][:, t * LANES:(t + 1) * LANES] = stage[pl.ds(t, rows16, stride=CMP_STRIDE), :]

    runs = sorted(_merged_sections(), key=lambda r: not (r[0] <= SEC_CF < r[1]))
    emit(*runs[0])
    nblocks = tm // CF_ROWS
    done = 0
    for n, run in enumerate(runs[1:], start=1):
        upto = nblocks * n // (len(runs) - 1)
        for r in range(done, upto):
            _conformer_rows(r, dw_ref, db_ref, lnw_ref, lnb_ref, out_refs[SEC_CF], hbuf)
        done = upto
        emit(*run)
    hbuf[0:CF_PAD, :] = hbuf[tm:tm + CF_PAD, :]


def _in_proj(x2, nw, w_perm, dw, db, lnw, lnb, seq, tm):
    m, d = x2.shape
    n = w_perm.shape[1]
    nt = seq // tm
    shapes, blocks, dtypes = [], [], []
    for k, wd in enumerate(SEC_WIDTHS):
        if k == SEC_CF:
            shapes.append((m, CF_CH)); blocks.append((tm, CF_CH)); dtypes.append(BF16)
        elif k < N_ROWMAJOR:
            shapes.append((m, wd)); blocks.append((tm, wd)); dtypes.append(F32)
        else:
            shapes.append((m // CMP_STRIDE, CMP_STRIDE * wd)); blocks.append((tm // CMP_STRIDE, CMP_STRIDE * wd))
            dtypes.append(F32)
    row = lambda bi, t: (bi * nt + t, 0)
    const = lambda bi, t: (0, 0)
    return pl.pallas_call(
        _in_proj_kernel,
        grid=(m // seq, nt),
        in_specs=[pl.BlockSpec((tm, d), row),
                  pl.BlockSpec((1, d), const),
                  pl.BlockSpec((d, n), const),
                  pl.BlockSpec((CF_KERNEL, CF_CH), const),
                  pl.BlockSpec((1, CF_CH), const),
                  pl.BlockSpec((1, CF_CH), const),
                  pl.BlockSpec((1, CF_CH), const)],
        out_specs=[pl.BlockSpec(blk, row) for blk in blocks],
        out_shape=[jax.ShapeDtypeStruct(shp, dt) for shp, dt in zip(shapes, dtypes)],
        scratch_shapes=[pltpu.VMEM((tm, LANES), F32),
                        pltpu.VMEM((tm + CF_PAD, CF_CH), F32)],
        compiler_params=pltpu.CompilerParams(dimension_semantics=("parallel", "arbitrary"),
                                             vmem_limit_bytes=VMEM_LIMIT),
        name="in_proj",
    )(x2, nw, w_perm, dw, db, lnw, lnb)


def _mix_ffn_kernel(x_ref, odn_ref, ocf_ref, onsa_ref, wodn_ref, wocf_ref, wonsa_ref, nw_ref,
                    wg_ref, wu_ref, wd_ref, out_ref):
    mix = (_dot(odn_ref[...], wodn_ref[...]) + _dot(ocf_ref[...], wocf_ref[...])
           + _dot(onsa_ref[...], wonsa_ref[...]))
    x1 = x_ref[...] + _rms(mix, nw_ref[1:2, :])
    h = _rms(x1, nw_ref[2:3, :]).astype(BF16)
    g = _dot(h, wg_ref[...])
    u = _dot(h, wu_ref[...])
    a = (_silu(g) * u).astype(BF16)
    f = _dot(a, wd_ref[...])
    out_ref[...] = x1 + _rms(f, nw_ref[3:4, :])


def _mix_ffn(x2, odn, ocf, onsa, wodn, wocf, wonsa, nw, wg, wu, wd, tm):
    m, d = x2.shape
    dff = wg.shape[1]
    const = lambda i: (0, 0)
    row = lambda i: (i, 0)
    return pl.pallas_call(
        _mix_ffn_kernel,
        grid=(m // tm,),
        in_specs=[pl.BlockSpec((tm, d), row),
                  pl.BlockSpec((tm, odn.shape[1]), row),
                  pl.BlockSpec((tm, ocf.shape[1]), row),
                  pl.BlockSpec((tm, onsa.shape[1]), row),
                  pl.BlockSpec(wodn.shape, const),
                  pl.BlockSpec(wocf.shape, const),
                  pl.BlockSpec(wonsa.shape, const),
                  pl.BlockSpec(nw.shape, const),
                  pl.BlockSpec((d, dff), const),
                  pl.BlockSpec((d, dff), const),
                  pl.BlockSpec((dff, d), const)],
        out_specs=pl.BlockSpec((tm, d), row),
        out_shape=jax.ShapeDtypeStruct((m, d), F32),
        compiler_params=pltpu.CompilerParams(dimension_semantics=("parallel",), vmem_limit_bytes=VMEM_LIMIT),
        name="mix_ffn",
    )(x2, odn, ocf, onsa, wodn, wocf, wonsa, nw, wg, wu, wd)


DN_PAD = 8


def _deltanet_kernel(qkv_ref, z_ref, sm_ref, cw_ref, alog_ref, dtb_ref, nw_ref, out_ref, xbuf, ybuf, state):
    n = SUPER
    nchunk = n // DN_CHUNK

    @pl.when(pl.program_id(1) == 0)
    def _():
        xbuf[0:DN_PAD, :] = jnp.zeros((DN_PAD, DN_QKV), F32)
        state[...] = jnp.zeros(state.shape, F32)

    xbuf[DN_PAD:DN_PAD + n, :] = qkv_ref[0]
    first = DN_PAD - (DN_CONV - 1)
    y = cw_ref[DN_CONV - 1:DN_CONV, :] * xbuf[pl.ds(DN_PAD, n), :]
    for j in range(DN_CONV - 1):
        y = y + cw_ref[j:j + 1, :] * xbuf[pl.ds(first + j, n), :]
    xbuf[0:DN_PAD, :] = xbuf[n:n + DN_PAD, :]
    ybuf[...] = _silu(y)
    q_all = ybuf[:, 0:DN_QK]
    k_all = ybuf[:, DN_QK:2 * DN_QK]
    v_all = ybuf[:, 2 * DN_QK:]

    hr = lax.broadcasted_iota(jnp.int32, (DN_QK, DN_QK), 0) // DN_DK
    hc = lax.broadcasted_iota(jnp.int32, (DN_QK, DN_QK), 1) // DN_DK
    head_ones = (hr == hc).astype(BF16)
    q_n = q_all * lax.rsqrt(_dot_x01(q_all * q_all, head_ones, 1) + EPS) * (DN_DK ** -0.5)
    k_n = k_all * lax.rsqrt(_dot_x01(k_all * k_all, head_ones, 1) + EPS)

    sm = sm_ref[0]
    beta_all = _sigmoid(sm)
    sp_in = sm + dtb_ref[...]
    softplus = jnp.maximum(sp_in, 0.0) + jnp.log1p(jnp.exp(-jnp.abs(sp_in)))
    g_all = -jnp.exp(alog_ref[...]) * softplus
    ri = lax.broadcasted_iota(jnp.int32, (n, n), 0)
    ci = lax.broadcasted_iota(jnp.int32, (n, n), 1)
    same_chunk = (ri // DN_CHUNK) == (ci // DN_CHUNK)
    causal = same_chunk & (ri >= ci)
    strict = same_chunk & (ri > ci)
    gam = _dot_01x(causal.astype(BF16), g_all, 3)
    glast = _dot_01x(same_chunk.astype(BF16), g_all, 3)
    gam_t = gam.T
    e_gam = jnp.exp(gam)
    e_rest = jnp.exp(glast - gam)
    e_tot = jnp.exp(glast)
    nfac = (DN_CHUNK - 1).bit_length() - 1
    k_t = k_n.T
    e_rest_t = e_rest.T
    lane_chunk = lax.broadcasted_iota(jnp.int32, (1, n), 1) // DN_CHUNK
    in_chunk = [lane_chunk == c for c in range(nchunk)]
    in_chunk_bf = [m.astype(BF16) for m in in_chunk]

    def to_rowform(bd):
        return sum(jnp.where(in_chunk[c], bd[c * DN_CHUNK:(c + 1) * DN_CHUNK, :], 0.0) for c in range(nchunk))

    def to_blockdiag(rowform):
        return jnp.concatenate([rowform * in_chunk_bf[c] for c in range(nchunk)], axis=0)

    outs = [None] * DN_HEADS
    for h0 in range(0, DN_HEADS, DN_GROUP):
        heads = range(h0, h0 + DN_GROUP)
        q, k, v, bcol, egcol, etcol, lmat, qkd = {}, {}, {}, {}, {}, {}, {}, {}
        for h in heads:
            sl = slice(h * DN_DK, (h + 1) * DN_DK)
            lg = LANE_A + h
            q[h], k[h], v[h] = q_n[:, sl], k_n[:, sl], v_all[:, sl]
            bcol[h] = beta_all[:, LANE_BETA + h:LANE_BETA + h + 1]
            egcol[h] = e_gam[:, lg:lg + 1]
            etcol[h] = e_tot[:, lg:lg + 1]
            kt = k_t[h * DN_DK:(h + 1) * DN_DK, :].astype(BF16)
            dec = jnp.exp(jnp.where(causal, gam[:, lg:lg + 1] - gam_t[lg:lg + 1, :], NEG))
            lmat[h] = jnp.where(strict, _dot((k[h] * bcol[h]).astype(BF16), kt) * dec, 0.0)
            qkd[h] = (_dot(q[h].astype(BF16), kt) * dec).astype(BF16)
        ps = {h: -to_rowform(lmat[h]) for h in heads}
        mp = {h: _dot((-ps[h]).astype(BF16), lmat[h].astype(BF16)) for h in heads}
        for s in range(nfac):
            mb = {h: mp[h].astype(BF16) for h in heads}
            mbd = {h: to_blockdiag(mb[h]) for h in heads}
            prod = {h: _dot(ps[h].astype(BF16), mbd[h]) for h in heads}
            ps = {h: ps[h] + mp[h] + prod[h] for h in heads}
            if s < nfac - 1:
                mp = {h: _dot(mb[h], mbd[h]) for h in heads}
        qeff, o_intra, b_in, n_in = {}, {}, {}, {}
        rhs = {h: jnp.concatenate([v[h] * bcol[h], k[h] * (bcol[h] * egcol[h])], axis=1) for h in heads}
        tx = {h: _dot(to_blockdiag(ps[h].astype(BF16)), rhs[h].astype(BF16)) for h in heads}
        x = {h: (rhs[h] + tx[h]).astype(BF16) for h in heads}
        qx = {h: _dot(qkd[h], x[h]) for h in heads}
        kdt = {h: (k_t[h * DN_DK:(h + 1) * DN_DK, :] * e_rest_t[LANE_A + h:LANE_A + h + 1, :]).astype(BF16)
               for h in heads}
        kx = {h: _dot(to_blockdiag(kdt[h]), x[h]) for h in heads}
        for h in heads:
            qeff[h] = (q[h] * egcol[h] - qx[h][:, DN_DV:]).astype(BF16)
            o_intra[h] = qx[h][:, :DN_DV]
            b_in[h], n_in[h] = kx[h][:, :DN_DV], kx[h][:, DN_DV:].astype(BF16)
        st = {h: state[h] for h in heads}
        o_chunks = {h: [] for h in heads}
        for c in range(nchunk):
            rs = slice(c * DN_CHUNK, (c + 1) * DN_CHUNK)
            ks = slice(c * DN_DK, (c + 1) * DN_DK)
            stb = {h: st[h].astype(BF16) for h in heads}
            corr = {h: _dot(n_in[h][ks], stb[h]) for h in heads}
            for h in heads:
                o_chunks[h].append(_dot(qeff[h][rs], stb[h]) + o_intra[h][rs])
                st[h] = st[h] * etcol[h][rs] + b_in[h][ks] - corr[h]
        for h in heads:
            state[h] = st[h]
            outs[h] = jnp.concatenate(o_chunks[h], axis=0)
    o = jnp.concatenate(outs, axis=1)
    ms = _dot_x01(o * o, head_ones, 1) * (1.0 / DN_DV)
    o = o * lax.rsqrt(ms + EPS) * nw_ref[...]
    out_ref[0] = (o * _silu(z_ref[0])).astype(out_ref.dtype)


def _deltanet(qkv, z, small, cw, alog_row, dtb_row, nw_row):
    b, s, _ = qkv.shape
    const = lambda i, t: (0, 0)
    tile = lambda i, t: (i, t, 0)
    return pl.pallas_call(
        _deltanet_kernel,
        grid=(b, s // SUPER),
        in_specs=[pl.BlockSpec((1, SUPER, DN_QKV), tile),
                  pl.BlockSpec((1, SUPER, DN_QK), tile),
                  pl.BlockSpec((1, SUPER, LANES), tile),
                  pl.BlockSpec((DN_CONV, DN_QKV), const),
                  pl.BlockSpec((1, LANES), const),
                  pl.BlockSpec((1, LANES), const),
                  pl.BlockSpec((1, DN_QK), const)],
        out_specs=pl.BlockSpec((1, SUPER, DN_QK), tile),
        out_shape=jax.ShapeDtypeStruct((b, s, DN_QK), BF16),
        scratch_shapes=[pltpu.VMEM((SUPER + DN_PAD, DN_QKV), F32),
                        pltpu.VMEM((SUPER, DN_QKV), F32),
                        pltpu.VMEM((DN_HEADS, DN_DK, DN_DV), F32)],
        compiler_params=pltpu.CompilerParams(dimension_semantics=("parallel", "arbitrary"),
                                             vmem_limit_bytes=VMEM_LIMIT),
        name="deltanet",
    )(qkv, z, small, cw, alog_row, dtb_row, nw_row)


def _compress(t_ref, posa_ref, posb_ref, w1a_ref, w1b_ref, w2_ref):
    t = t_ref[0]
    nrow = t.shape[0]
    p = _dot((t + posa_ref[...]).astype(BF16), w1a_ref[...])
    q = _dot((t + posb_ref[...]).astype(BF16), w1b_ref[...])
    hid = _silu(p + pltpu.roll(q, nrow - 1, 0))
    return _dot(hid.astype(BF16), w2_ref[...])


def _store_vt(dst, v_ref):
    vt = v_ref[0].T
    ones = jnp.ones((BF16_ROWS, TQ), BF16)
    for g in range(NSA_KV):
        for c in range(dst.shape[1]):
            blk = vt[g * NSA_DH:(g + 1) * NSA_DH, c * TQ:(c + 1) * TQ].astype(BF16)
            dst[g, c] = jnp.concatenate([blk, ones], axis=0)


def _attend_chunks(chunks, slots, key_chunk, queries, vt_chunk, m_s, acc_s):
    scores = [[_dot(key_chunk(j), q) for q in queries] for j, _ in chunks]
    for (j, mask), chunk_scores in zip(chunks, scores):
        probs, alphas = [], []
        for slot, s in zip(slots, chunk_scores):
            if mask is not None:
                s = jnp.where(mask, s, NEG)
            m_prev = m_s[slot]
            m_new = jnp.maximum(m_prev, jnp.max(s, axis=0, keepdims=True))
            probs.append(jnp.exp(s - m_new).astype(BF16))
            alphas.append(jnp.exp(m_prev - m_new))
            m_s[slot] = m_new
        pv = [_dot(vt_chunk(n, j), p) for n, p in enumerate(probs)]
        for slot, a, o in zip(slots, alphas, pv):
            acc_s[slot] = a * acc_s[slot] + o


def _nsa_kernel(q_ref, sm_ref, kct_ref, vct_ref, ks_ref, vs_ref, kw_ref, vw_ref,
                kposa_ref, kposb_ref, kw1a_ref, kw1b_ref, kw2_ref,
                vposa_ref, vposb_ref, vw1a_ref, vw1b_ref, vw2_ref,
                out_ref,
                kc_s, vct_s, ksb, kwb, vst_s, vwt_s, qt_s, oc_s, m_s, acc_s):
    i = pl.program_id(1)
    ncmp = kc_s.shape[1]
    nblk = ks_ref.shape[1] // SEL_BLOCK
    heads = range(NSA_HEADS)

    @pl.when(i == 0)
    def _():
        kc = _compress(kct_ref, kposa_ref, kposb_ref, kw1a_ref, kw1b_ref, kw2_ref)
        kc_hi, kc_lo = _split_terms(kc, 2)
        kc_s[0] = kc_hi
        kc_s[1] = kc_lo
        vct_s[...] = _compress(vct_ref, vposa_ref, vposb_ref, vw1a_ref, vw1b_ref, vw2_ref).T.astype(BF16)
        s_len = ks_ref.shape[1]
        key_blk = lax.broadcasted_iota(jnp.int32, (s_len, LANES), 0) // SEL_BLOCK
        ksb[:, 0:LANES] = ks_ref[0].astype(BF16)
        ksb[:, LANES:] = (lax.broadcasted_iota(jnp.int32, (s_len, LANES), 1) == key_blk).astype(BF16)
        kwb[...] = kw_ref[0].astype(BF16)
        for idx in heads:
            qt_s[idx, LANES + nblk:, :] = jnp.zeros((LANES - nblk, TQ), BF16)
        _store_vt(vst_s, vs_ref)
        _store_vt(vwt_s, vw_ref)

    trow = i * TQ + lax.broadcasted_iota(jnp.int32, (1, TQ), 1)
    krel = lax.broadcasted_iota(jnp.int32, (TQ, 1), 0)
    cmp_end = lax.broadcasted_iota(jnp.int32, (ncmp, 1), 0) * CMP_STRIDE + (CMP_LEN - 1)
    valid_c = cmp_end <= trow
    oj = lax.broadcasted_iota(jnp.int32, (nblk, ncmp), 0) * SEL_BLOCK
    on = lax.broadcasted_iota(jnp.int32, (nblk, ncmp), 1) * CMP_STRIDE
    ov_t = ((on < oj + SEL_BLOCK) & (on + CMP_LEN > oj)).astype(BF16)
    jj = lax.broadcasted_iota(jnp.int32, (nblk, TQ), 0)
    cur = trow // SEL_BLOCK
    elig = jj <= cur
    forced = (jj == 0) | (jj == cur) | (jj == cur - 1)

    q_all = q_ref[0] * (NSA_DH ** -0.5)
    kc_hi, kc_lo = kc_s[0], kc_s[1]
    s_c = []
    for idx in heads:
        qt = q_all[:, idx * LANES:(idx + 1) * LANES].T
        qt_hi, qt_lo = _split_terms(qt, 2)
        qt_s[idx, 0:LANES, :] = qt_hi
        s_c.append(_dot(kc_hi, qt_hi) + _dot(kc_hi, qt_lo) + _dot(kc_lo, qt_hi))
    probs = []
    for idx in heads:
        s = jnp.where(valid_c, s_c[idx], NEG)
        e = jnp.where(valid_c, jnp.exp(s - jnp.max(s, axis=0, keepdims=True)), 0.0)
        probs.append(e / jnp.maximum(jnp.sum(e, axis=0, keepdims=True), 1e-30))
    for idx in heads:
        g = idx // NSA_HPG
        oc_s[idx] = _dot(vct_s[g * NSA_DH:(g + 1) * NSA_DH, :], probs[idx].astype(BF16))
    for g in range(NSA_KV):
        psum = sum(probs[g * NSA_HPG:(g + 1) * NSA_HPG])
        imp_t = _dot_01x(ov_t, psum, 3)
        score = jnp.where(elig, imp_t + jnp.where(forced, SEL_BIG, 0.0), -SEL_BIG)
        rank = jnp.zeros((nblk, TQ), F32)
        for r in range(nblk):
            row = score[r:r + 1, :]
            beats = (row > score) | ((row == score) & (r < jj))
            rank = rank + beats.astype(F32)
        bias = jnp.where(rank < float(N_SEL), 0.0, NEG).astype(BF16)
        for hh in range(NSA_HPG):
            qt_s[g * NSA_HPG + hh, LANES:LANES + nblk, :] = bias

    m_s[...] = jnp.full(m_s.shape, NEG, F32)
    acc_s[...] = jnp.zeros(acc_s.shape, F32)
    qrel = lax.broadcasted_iota(jnp.int32, (1, TQ), 1)
    on_or_below = krel <= qrel

    def attend_sel(chunks):
        _attend_chunks(chunks, list(heads), lambda j: ksb[pl.ds(pl.multiple_of(j * TQ, TQ), TQ), :],
                       [qt_s[idx] for idx in heads], lambda n, j: vst_s[n // NSA_HPG, j], m_s, acc_s)

    def sel_pair(p, carry):
        attend_sel([(2 * p, None), (2 * p + 1, None)])
        return carry

    lax.fori_loop(0, i // 2, sel_pair, 0)

    @pl.when(i % 2 == 0)
    def _():
        attend_sel([(i, on_or_below)])

    @pl.when(i % 2 == 1)
    def _():
        attend_sel([(i - 1, None), (i, on_or_below)])

    def attend_win(chunks):
        _attend_chunks(chunks, [NSA_HEADS + idx for idx in heads],
                       lambda j: kwb[pl.ds(pl.multiple_of(j * TQ, TQ), TQ), :],
                       [qt_s[idx, 0:LANES, :] for idx in heads], lambda n, j: vwt_s[n // NSA_HPG, j], m_s, acc_s)

    @pl.when(i == 0)
    def _():
        attend_win([(i, on_or_below)])

    @pl.when(i >= 1)
    def _():
        attend_win([(i, on_or_below), (i - 1, None)])

    @pl.when(i >= 2)
    def _():
        attend_win([(i - 2, krel > qrel)])

    gates_t = _sigmoid(sm_ref[0]).T
    outs = []
    for idx in range(NSA_HEADS):
        acc = acc_s[idx]
        o_s = acc[:NSA_DH] * (1.0 / acc[NSA_DH:NSA_DH + 1])
        acc = acc_s[NSA_HEADS + idx]
        o_w = acc[:NSA_DH] * (1.0 / acc[NSA_DH:NSA_DH + 1])
        lg = LANE_GATE + idx
        outs.append(gates_t[lg:lg + 1] * oc_s[idx] + gates_t[lg + NSA_HEADS:lg + NSA_HEADS + 1] * o_s
                    + gates_t[lg + 2 * NSA_HEADS:lg + 2 * NSA_HEADS + 1] * o_w)
    out_ref[0] = jnp.concatenate(outs, axis=0).T.astype(out_ref.dtype)


def _nsa(nq, small, kct, vct, ks, vs, kw, vw, kparams, vparams):
    b, s, _ = nq.shape
    ncmp = kct.shape[1]
    nchunks = s // TQ
    tile = lambda bi, i: (bi, i, 0)
    full = lambda bi, i: (bi, 0, 0)
    const = lambda bi, i: (0, 0)
    wspecs = [pl.BlockSpec(p.shape, const) for p in kparams + vparams]
    return pl.pallas_call(
        _nsa_kernel,
        grid=(b, nchunks),
        in_specs=[pl.BlockSpec((1, TQ, NSA_HEADS * LANES), tile),
                  pl.BlockSpec((1, TQ, LANES), tile),
                  pl.BlockSpec((1, ncmp, kct.shape[2]), full),
                  pl.BlockSpec((1, ncmp, vct.shape[2]), full),
                  pl.BlockSpec((1, s, LANES), full),
                  pl.BlockSpec((1, s, LANES), full),
                  pl.BlockSpec((1, s, LANES), full),
                  pl.BlockSpec((1, s, LANES), full)] + wspecs,
        out_specs=pl.BlockSpec((1, TQ, NSA_HEADS * NSA_DH), tile),
        out_shape=jax.ShapeDtypeStruct((b, s, NSA_HEADS * NSA_DH), BF16),
        scratch_shapes=[pltpu.VMEM((2, ncmp, LANES), BF16),
                        pltpu.VMEM((LANES, ncmp), BF16),
                        pltpu.VMEM((s, 2 * LANES), BF16),
                        pltpu.VMEM((s, LANES), BF16),
                        pltpu.VMEM((NSA_KV, nchunks, VT_ROWS, TQ), BF16),
                        pltpu.VMEM((NSA_KV, nchunks, VT_ROWS, TQ), BF16),
                        pltpu.VMEM((NSA_HEADS, 2 * LANES, TQ), BF16),
                        pltpu.VMEM((NSA_HEADS, NSA_DH, TQ), F32),
                        pltpu.VMEM((2 * NSA_HEADS, 1, TQ), F32),
                        pltpu.VMEM((2 * NSA_HEADS, VT_ROWS, TQ), F32)],
        compiler_params=pltpu.CompilerParams(dimension_semantics=("parallel", "arbitrary"),
                                             vmem_limit_bytes=VMEM_LIMIT),
        name="nsa",
    )(nq, small, kct, vct, ks, vs, kw, vw, *kparams, *vparams)


def _source_columns():
    o = 0
    qkv = list(range(o, o + DN_QKV)); o += DN_QKV
    z = list(range(o, o + DN_HEADS * DN_DV)); o += DN_HEADS * DN_DV
    b_raw = list(range(o, o + DN_HEADS)); o += DN_HEADS
    a_raw = list(range(o, o + DN_HEADS)); o += DN_HEADS
    cf_u = list(range(o, o + 2 * CF_CH)); o += 2 * CF_CH
    n_q = list(range(o, o + NSA_HEADS * NSA_DH)); o += NSA_HEADS * NSA_DH
    kv = []
    for _ in range(6):
        kv.append(list(range(o, o + NSA_KV * NSA_DH))); o += NSA_KV * NSA_DH
    gate = list(range(o, o + 3 * NSA_HEADS)); o += 3 * NSA_HEADS
    n_kc, n_vc, n_ks, n_vs, n_kw, n_vw = kv
    small = b_raw + a_raw + gate
    small = small + [-1] * (LANES - len(small))
    nq = []
    zero = [-1] * NSA_DH
    for idx in range(NSA_HEADS):
        qh = n_q[idx * NSA_DH:(idx + 1) * NSA_DH]
        nq += (qh + zero) if idx // NSA_HPG == 0 else (zero + qh)
    cols = qkv + z + small + cf_u + nq + n_ks + n_vs + n_kw + n_vw + n_kc + n_vc
    assert len(cols) == sum(SEC_WIDTHS)
    return np.asarray(cols, np.int32), o


def _permute_w_in(w):
    cols, n_in = _source_columns()
    assert w.shape[1] == n_in
    select = (lax.broadcasted_iota(jnp.int32, (n_in, cols.shape[0]), 0) == cols[None, :]).astype(BF16)
    return jnp.dot(w.astype(BF16), select, preferred_element_type=BF16)


def _compress_params(pos, w1, w2):
    eye = jnp.eye(NSA_KV, dtype=F32)
    half = CMP_LEN // 2
    w1r = w1.reshape(CMP_LEN, NSA_DH, CMP_HID)

    def expand_w1(part):
        return jnp.einsum("ldj,gh->lgdhj", part, eye).reshape(half * NSA_KV * NSA_DH, NSA_KV * CMP_HID).astype(BF16)

    def expand_pos(part):
        return jnp.broadcast_to(part[:, None, :], (half, NSA_KV, NSA_DH)).reshape(1, half * NSA_KV * NSA_DH)

    w2e = jnp.einsum("jd,gh->gjhd", w2, eye).reshape(NSA_KV * CMP_HID, NSA_KV * NSA_DH).astype(BF16)
    return [expand_pos(pos[:half]), expand_pos(pos[half:]), expand_w1(w1r[:half]), expand_w1(w1r[half:]), w2e]


def _lane_row(vals, first_lane):
    pad = jnp.zeros((LANES - first_lane - vals.shape[0],), F32)
    return jnp.concatenate([jnp.zeros((first_lane,), F32), vals.astype(F32), pad])[None, :]


def kernel(x, norm_w, w_in, dn_conv_w, dn_a_log, dn_dt_bias, dn_norm_w, cf_dw_w, cf_dw_b, cf_ln_w, cf_ln_b,
           nsa_k_pos, nsa_v_pos, nsa_k_w1, nsa_k_w2, nsa_v_w1, nsa_v_w2, w_out, ffn_w_gate, ffn_w_up, ffn_w_down):
    b, s, d = x.shape
    depth = w_in.shape[0]
    assert s % SUPER == 0 and s % TQ == 0 and s // CMP_STRIDE == LANES and s // SEL_BLOCK <= LANES
    assert WINDOW == 2 * TQ and TQ % SEL_BLOCK == 0
    m = b * s
    tm = 512 if m % 512 == 0 else 256
    x2 = x.reshape(m, d)
    ndn = DN_HEADS * DN_DV
    rows16 = s // CMP_STRIDE
    for l in range(depth):
        secs = _in_proj(x2, norm_w[l, 0:1], _permute_w_in(w_in[l]), cf_dw_w[l], cf_dw_b[l][None, :],
                        cf_ln_w[l][None, :], cf_ln_b[l][None, :], s, tm)
        qkv, z, small, o_cf, nq, n_ks, n_vs, n_kw, n_vw = [t.reshape(b, s, t.shape[1]) for t in secs[:N_ROWMAJOR]]
        n_kct, n_vct = [t.reshape(b, rows16, t.shape[1]) for t in secs[N_ROWMAJOR:]]
        o_dn = _deltanet(qkv, z, small, dn_conv_w[l], _lane_row(dn_a_log[l], LANE_A), _lane_row(dn_dt_bias[l], LANE_A),
                         jnp.tile(dn_norm_w[l], DN_HEADS)[None, :])
        o_nsa = _nsa(nq, small, n_kct, n_vct, n_ks, n_vs, n_kw, n_vw,
                     _compress_params(nsa_k_pos[l], nsa_k_w1[l], nsa_k_w2[l]),
                     _compress_params(nsa_v_pos[l], nsa_v_w1[l], nsa_v_w2[l]))
        wo = w_out[l].astype(BF16)
        x2 = _mix_ffn(x2, o_dn.reshape(m, ndn), o_cf.reshape(m, CF_CH), o_nsa.reshape(m, NSA_HEADS * NSA_DH),
                      wo[:ndn], wo[ndn:ndn + CF_CH], wo[ndn + CF_CH:], norm_w[l],
                      ffn_w_gate[l].astype(BF16), ffn_w_up[l].astype(BF16), ffn_w_down[l].astype(BF16), tm)
    return x2.reshape(b, s, d)
```

```python
import jax
import jax.numpy as jnp
import numpy as np
from jax import lax
from jax.experimental import pallas as pl
from jax.experimental.pallas import tpu as pltpu

F32 = jnp.float32
BF16 = jnp.bfloat16

DN_HEADS = 6
DN_DK = 64
DN_DV = 64
DN_CONV = 4
DN_CHUNK = 64
CF_CH = 256
CF_KERNEL = 31
NSA_HEADS = 6
NSA_KV = 2
NSA_HPG = NSA_HEADS // NSA_KV
NSA_DH = 64
CMP_LEN = 32
CMP_STRIDE = 16
CMP_HID = 2 * NSA_DH
SEL_BLOCK = 64
N_SEL = 8
WINDOW = 512
EPS = 1e-6
NEG = -1e30
SEL_BIG = 1e4

LANES = 128
SUBLANES = 8
BF16_ROWS = 16
DN_QK = DN_HEADS * DN_DK
DN_QKV = 2 * DN_QK + DN_HEADS * DN_DV
SUPER = 256
DN_BATCH = 2
TQ = 256
VT_ROWS = NSA_DH + BF16_ROWS
VMEM_LIMIT = 56 * 1024 * 1024

SEC_WIDTHS = (DN_QKV, DN_HEADS * DN_DV, LANES, 2 * CF_CH, NSA_HEADS * LANES,
              LANES, LANES, LANES, LANES, LANES, LANES)
N_ROWMAJOR = 9
SEC_CF = 3
CF_ROWS = 64
CF_PAD = 32
LANE_BETA = 0
LANE_A = DN_HEADS
LANE_GATE = 2 * DN_HEADS


def _dot(a, b, precision=None):
    return jnp.dot(a, b, preferred_element_type=F32, precision=precision)


def _split_terms(x, terms):
    out = []
    for _ in range(terms):
        hi = x.astype(BF16)
        out.append(hi)
        x = x - hi.astype(F32)
    return out


def _dot_x01(x, mat01, terms):
    return sum(_dot(p, mat01) for p in _split_terms(x, terms))


def _dot_01x(mat01, x, terms):
    return sum(_dot(mat01, p) for p in _split_terms(x, terms))


def _sigmoid(x):
    return 1.0 / (1.0 + jnp.exp(-x))


def _silu(x):
    return x * _sigmoid(x)


def _rms(x, w):
    return x * lax.rsqrt(jnp.mean(x * x, axis=-1, keepdims=True) + EPS) * w


IN_PROJ_COLS = 1280


def _merged_sections():
    runs, first, width = [], 0, 0
    for k, wd in enumerate(SEC_WIDTHS):
        if width and width + wd > IN_PROJ_COLS:
            runs.append((first, k))
            first, width = k, 0
        width += wd
    runs.append((first, len(SEC_WIDTHS)))
    return runs


def _conformer_rows(r, dw_ref, db_ref, lnw_ref, lnb_ref, out_ref, hbuf):
    first = CF_PAD - (CF_KERNEL - 1)
    acc = jnp.zeros((CF_ROWS, CF_CH), F32) + db_ref[...]
    for phase in range(SUBLANES):
        part = None
        for j in range(CF_KERNEL):
            if (first + j) % SUBLANES != phase:
                continue
            base = r * CF_ROWS + first + j - phase
            rows = CF_ROWS + (SUBLANES if phase else 0)
            term = dw_ref[j:j + 1, :] * hbuf[pl.ds(base, rows), :]
            part = term if part is None else part + term
        if part is not None:
            acc = acc + part[phase:phase + CF_ROWS, :]
    mu = jnp.mean(acc, axis=-1, keepdims=True)
    cen = acc - mu
    var = jnp.mean(cen * cen, axis=-1, keepdims=True)
    y = cen * lax.rsqrt(var + EPS) * lnw_ref[...] + lnb_ref[...]
    out_ref[r * CF_ROWS:(r + 1) * CF_ROWS, :] = _silu(y).astype(out_ref.dtype)


def _in_proj_kernel(x_ref, nw_ref, w_ref, dw_ref, db_ref, lnw_ref, lnb_ref, *refs):
    out_refs, stage, hbuf = refs[:-2], refs[-2], refs[-1]
    tm = x_ref.shape[0]

    @pl.when(pl.program_id(1) == 0)
    def _():
        hbuf[0:CF_PAD, :] = jnp.zeros((CF_PAD, CF_CH), F32)

    h = _rms(x_ref[...], nw_ref[...]).astype(BF16)
    rows16 = tm // CMP_STRIDE

    def emit(first, last):
        off = sum(SEC_WIDTHS[:first])
        wide = _dot(h, w_ref[:, off:off + sum(SEC_WIDTHS[first:last])])
        col = 0
        for k in range(first, last):
            y = wide[:, col:col + SEC_WIDTHS[k]]
            col += SEC_WIDTHS[k]
            if k == SEC_CF:
                hbuf[CF_PAD:CF_PAD + tm, :] = y[:, :CF_CH] * _sigmoid(y[:, CF_CH:])
            elif k < N_ROWMAJOR:
                out_refs[k][...] = y
            else:
                stage[...] = y
                for t in range(CMP_STRIDE):
                    out_refs[k][:, t * LANES:(t + 1) * LANES] = stage[pl.ds(t, rows16, stride=CMP_STRIDE), :]

    runs = sorted(_merged_sections(), key=lambda r: not (r[0] <= SEC_CF < r[1]))
    emit(*runs[0])
    nblocks = tm // CF_ROWS
    done = 0
    for n, run in enumerate(runs[1:], start=1):
        upto = nblocks * n // (len(runs) - 1)
        for r in range(done, upto):
            _conformer_rows(r, dw_ref, db_ref, lnw_ref, lnb_ref, out_refs[SEC_CF], hbuf)
        done = upto
        emit(*run)
    hbuf[0:CF_PAD, :] = hbuf[tm:tm + CF_PAD, :]


def _in_proj(x2, nw, w_perm, dw, db, lnw, lnb, seq, tm):
    m, d = x2.shape
    n = w_perm.shape[1]
    nt = seq // tm
    shapes, blocks, dtypes = [], [], []
    for k, wd in enumerate(SEC_WIDTHS):
        if k == SEC_CF:
            shapes.append((m, CF_CH)); blocks.append((tm, CF_CH)); dtypes.append(BF16)
        elif k < N_ROWMAJOR:
            shapes.append((m, wd)); blocks.append((tm, wd)); dtypes.append(F32)
        else:
            shapes.append((m // CMP_STRIDE, CMP_STRIDE * wd)); blocks.append((tm // CMP_STRIDE, CMP_STRIDE * wd))
            dtypes.append(F32)
    row = lambda bi, t: (bi * nt + t, 0)
    const = lambda bi, t: (0, 0)
    return pl.pallas_call(
        _in_proj_kernel,
        grid=(m // seq, nt),
        in_specs=[pl.BlockSpec((tm, d), row),
                  pl.BlockSpec((1, d), const),
                  pl.BlockSpec((d, n), const),
                  pl.BlockSpec((CF_KERNEL, CF_CH), const),
                  pl.BlockSpec((1, CF_CH), const),
                  pl.BlockSpec((1, CF_CH), const),
                  pl.BlockSpec((1, CF_CH), const)],
        out_specs=[pl.BlockSpec(blk, row) for blk in blocks],
        out_shape=[jax.ShapeDtypeStruct(shp, dt) for shp, dt in zip(shapes, dtypes)],
        scratch_shapes=[pltpu.VMEM((tm, LANES), F32),
                        pltpu.VMEM((tm + CF_PAD, CF_CH), F32)],
        compiler_params=pltpu.CompilerParams(dimension_semantics=("parallel", "arbitrary"),
                                             vmem_limit_bytes=VMEM_LIMIT),
        name="in_proj",
    )(x2, nw, w_perm, dw, db, lnw, lnb)


def _mix_ffn_kernel(x_ref, odn_ref, ocf_ref, onsa_ref, wodn_ref, wocf_ref, wonsa_ref, nw_ref,
                    wg_ref, wu_ref, wd_ref, out_ref):
    mix = (_dot(odn_ref[...], wodn_ref[...]) + _dot(ocf_ref[...], wocf_ref[...])
           + _dot(onsa_ref[...], wonsa_ref[...]))
    x1 = x_ref[...] + _rms(mix, nw_ref[1:2, :])
    h = _rms(x1, nw_ref[2:3, :]).astype(BF16)
    g = _dot(h, wg_ref[...])
    u = _dot(h, wu_ref[...])
    a = (_silu(g) * u).astype(BF16)
    f = _dot(a, wd_ref[...])
    out_ref[...] = x1 + _rms(f, nw_ref[3:4, :])


def _mix_ffn(x2, odn, ocf, onsa, wodn, wocf, wonsa, nw, wg, wu, wd, tm):
    m, d = x2.shape
    dff = wg.shape[1]
    const = lambda i: (0, 0)
    row = lambda i: (i, 0)
    return pl.pallas_call(
        _mix_ffn_kernel,
        grid=(m // tm,),
        in_specs=[pl.BlockSpec((tm, d), row),
                  pl.BlockSpec((tm, odn.shape[1]), row),
                  pl.BlockSpec((tm, ocf.shape[1]), row),
                  pl.BlockSpec((tm, onsa.shape[1]), row),
                  pl.BlockSpec(wodn.shape, const),
                  pl.BlockSpec(wocf.shape, const),
                  pl.BlockSpec(wonsa.shape, const),
                  pl.BlockSpec(nw.shape, const),
                  pl.BlockSpec((d, dff), const),
                  pl.BlockSpec((d, dff), const),
                  pl.BlockSpec((dff, d), const)],
        out_specs=pl.BlockSpec((tm, d), row),
        out_shape=jax.ShapeDtypeStruct((m, d), F32),
        compiler_params=pltpu.CompilerParams(dimension_semantics=("parallel",), vmem_limit_bytes=VMEM_LIMIT),
        name="mix_ffn",
    )(x2, odn, ocf, onsa, wodn, wocf, wonsa, nw, wg, wu, wd)


DN_PAD = 8


def _deltanet_kernel(qkv_ref, z_ref, sm_ref, cw_ref, alog_ref, dtb_ref, nw_ref, out_ref, xbuf, ybuf, state):
    n = SUPER
    nchunk = n // DN_CHUNK
    nb = qkv_ref.shape[0]

    @pl.when(pl.program_id(1) == 0)
    def _():
        xbuf[:, 0:DN_PAD, :] = jnp.zeros((nb, DN_PAD, DN_QKV), F32)
        state[...] = jnp.zeros(state.shape, F32)

    hr = lax.broadcasted_iota(jnp.int32, (DN_QK, DN_QK), 0) // DN_DK
    hc = lax.broadcasted_iota(jnp.int32, (DN_QK, DN_QK), 1) // DN_DK
    head_ones = (hr == hc).astype(BF16)
    ri = lax.broadcasted_iota(jnp.int32, (n, n), 0)
    ci = lax.broadcasted_iota(jnp.int32, (n, n), 1)
    same_chunk = (ri // DN_CHUNK) == (ci // DN_CHUNK)
    causal = same_chunk & (ri >= ci)
    strict = same_chunk & (ri > ci)
    causal_bf = causal.astype(BF16)
    same_chunk_bf = same_chunk.astype(BF16)
    nfac = (DN_CHUNK - 1).bit_length() - 1
    lane_chunk = lax.broadcasted_iota(jnp.int32, (1, n), 1) // DN_CHUNK
    in_chunk = [lane_chunk == c for c in range(nchunk)]
    in_chunk_bf = [m.astype(BF16) for m in in_chunk]

    def to_rowform(bd):
        return sum(jnp.where(in_chunk[c], bd[c * DN_CHUNK:(c + 1) * DN_CHUNK, :], 0.0) for c in range(nchunk))

    def to_blockdiag(rowform):
        return jnp.concatenate([rowform * in_chunk_bf[c] for c in range(nchunk)], axis=0)

    first = DN_PAD - (DN_CONV - 1)
    rows = []
    for bi in range(nb):
        xbuf[bi, DN_PAD:DN_PAD + n, :] = qkv_ref[bi]
        y = cw_ref[DN_CONV - 1:DN_CONV, :] * xbuf[bi, pl.ds(DN_PAD, n), :]
        for j in range(DN_CONV - 1):
            y = y + cw_ref[j:j + 1, :] * xbuf[bi, pl.ds(first + j, n), :]
        xbuf[bi, 0:DN_PAD, :] = xbuf[bi, n:n + DN_PAD, :]
        ybuf[bi] = _silu(y)
        q_all = ybuf[bi, :, 0:DN_QK]
        k_all = ybuf[bi, :, DN_QK:2 * DN_QK]
        v_all = ybuf[bi, :, 2 * DN_QK:]
        q_n = q_all * lax.rsqrt(_dot_x01(q_all * q_all, head_ones, 1) + EPS) * (DN_DK ** -0.5)
        k_n = k_all * lax.rsqrt(_dot_x01(k_all * k_all, head_ones, 1) + EPS)
        sm = sm_ref[bi]
        beta_all = _sigmoid(sm)
        sp_in = sm + dtb_ref[...]
        softplus = jnp.maximum(sp_in, 0.0) + jnp.log1p(jnp.exp(-jnp.abs(sp_in)))
        g_all = -jnp.exp(alog_ref[...]) * softplus
        gam = _dot_01x(causal_bf, g_all, 3)
        glast = _dot_01x(same_chunk_bf, g_all, 3)
        rows.append(dict(q_n=q_n, k_n=k_n, v_all=v_all, beta_all=beta_all, gam=gam, gam_t=gam.T,
                         e_gam=jnp.exp(gam), e_tot=jnp.exp(glast), e_rest_t=jnp.exp(glast - gam).T,
                         k_t=k_n.T))

    heads = [(bi, h) for bi in range(nb) for h in range(DN_HEADS)]
    q, k, v, bcol, egcol, etcol, lmat, qkd = {}, {}, {}, {}, {}, {}, {}, {}
    for key in heads:
        bi, h = key
        r = rows[bi]
        sl = slice(h * DN_DK, (h + 1) * DN_DK)
        lg = LANE_A + h
        q[key], k[key], v[key] = r["q_n"][:, sl], r["k_n"][:, sl], r["v_all"][:, sl]
        bcol[key] = r["beta_all"][:, LANE_BETA + h:LANE_BETA + h + 1]
        egcol[key] = r["e_gam"][:, lg:lg + 1]
        etcol[key] = r["e_tot"][:, lg:lg + 1]
        kt = r["k_t"][h * DN_DK:(h + 1) * DN_DK, :].astype(BF16)
        dec = jnp.exp(jnp.where(causal, r["gam"][:, lg:lg + 1] - r["gam_t"][lg:lg + 1, :], NEG))
        lmat[key] = jnp.where(strict, _dot((k[key] * bcol[key]).astype(BF16), kt) * dec, 0.0)
        qkd[key] = (_dot(q[key].astype(BF16), kt) * dec).astype(BF16)
    ps = {key: -to_rowform(lmat[key]) for key in heads}
    mp = {key: _dot((-ps[key]).astype(BF16), lmat[key].astype(BF16)) for key in heads}
    for s in range(nfac):
        mb = {key: mp[key].astype(BF16) for key in heads}
        mbd = {key: to_blockdiag(mb[key]) for key in heads}
        prod = {key: _dot(ps[key].astype(BF16), mbd[key]) for key in heads}
        ps = {key: ps[key] + mp[key] + prod[key] for key in heads}
        if s < nfac - 1:
            mp = {key: _dot(mb[key], mbd[key]) for key in heads}
    qeff, o_intra, b_in, n_in = {}, {}, {}, {}
    rhs = {key: jnp.concatenate([v[key] * bcol[key], k[key] * (bcol[key] * egcol[key])], axis=1) for key in heads}
    tx = {key: _dot(to_blockdiag(ps[key].astype(BF16)), rhs[key].astype(BF16)) for key in heads}
    x = {key: (rhs[key] + tx[key]).astype(BF16) for key in heads}
    qx = {key: _dot(qkd[key], x[key]) for key in heads}
    kdt = {(bi, h): (rows[bi]["k_t"][h * DN_DK:(h + 1) * DN_DK, :]
                     * rows[bi]["e_rest_t"][LANE_A + h:LANE_A + h + 1, :]).astype(BF16) for bi, h in heads}
    kx = {key: _dot(to_blockdiag(kdt[key]), x[key]) for key in heads}
    for key in heads:
        qeff[key] = (q[key] * egcol[key] - qx[key][:, DN_DV:]).astype(BF16)
        o_intra[key] = qx[key][:, :DN_DV]
        b_in[key], n_in[key] = kx[key][:, :DN_DV], kx[key][:, DN_DV:].astype(BF16)
    st = {(bi, h): state[bi * DN_HEADS + h] for bi, h in heads}
    o_chunks = {key: [] for key in heads}
    for c in range(nchunk):
        rs = slice(c * DN_CHUNK, (c + 1) * DN_CHUNK)
        ks = slice(c * DN_DK, (c + 1) * DN_DK)
        stb = {key: st[key].astype(BF16) for key in heads}
        corr = {key: _dot(n_in[key][ks], stb[key]) for key in heads}
        for key in heads:
            o_chunks[key].append(_dot(qeff[key][rs], stb[key]) + o_intra[key][rs])
            st[key] = st[key] * etcol[key][rs] + b_in[key][ks] - corr[key]
    for bi in range(nb):
        for h in range(DN_HEADS):
            state[bi * DN_HEADS + h] = st[(bi, h)]
        o = jnp.concatenate([jnp.concatenate(o_chunks[(bi, h)], axis=0) for h in range(DN_HEADS)], axis=1)
        ms = _dot_x01(o * o, head_ones, 1) * (1.0 / DN_DV)
        o = o * lax.rsqrt(ms + EPS) * nw_ref[...]
        out_ref[bi] = (o * _silu(z_ref[bi])).astype(out_ref.dtype)


def _deltanet(qkv, z, small, cw, alog_row, dtb_row, nw_row):
    b, s, _ = qkv.shape
    nb = DN_BATCH if b % DN_BATCH == 0 else 1
    const = lambda i, t: (0, 0)
    tile = lambda i, t: (i, t, 0)
    return pl.pallas_call(
        _deltanet_kernel,
        grid=(b // nb, s // SUPER),
        in_specs=[pl.BlockSpec((nb, SUPER, DN_QKV), tile),
                  pl.BlockSpec((nb, SUPER, DN_QK), tile),
                  pl.BlockSpec((nb, SUPER, LANES), tile),
                  pl.BlockSpec((DN_CONV, DN_QKV), const),
                  pl.BlockSpec((1, LANES), const),
                  pl.BlockSpec((1, LANES), const),
                  pl.BlockSpec((1, DN_QK), const)],
        out_specs=pl.BlockSpec((nb, SUPER, DN_QK), tile),
        out_shape=jax.ShapeDtypeStruct((b, s, DN_QK), BF16),
        scratch_shapes=[pltpu.VMEM((nb, SUPER + DN_PAD, DN_QKV), F32),
                        pltpu.VMEM((nb, SUPER, DN_QKV), F32),
                        pltpu.VMEM((nb * DN_HEADS, DN_DK, DN_DV), F32)],
        compiler_params=pltpu.CompilerParams(dimension_semantics=("parallel", "arbitrary"),
                                             vmem_limit_bytes=VMEM_LIMIT),
        name="deltanet",
    )(qkv, z, small, cw, alog_row, dtb_row, nw_row)


def _compress(t_ref, posa_ref, posb_ref, w1a_ref, w1b_ref, w2_ref):
    t = t_ref[0]
    nrow = t.shape[0]
    p = _dot((t + posa_ref[...]).astype(BF16), w1a_ref[...])
    q = _dot((t + posb_ref[...]).astype(BF16), w1b_ref[...])
    hid = _silu(p + pltpu.roll(q, nrow - 1, 0))
    return _dot(hid.astype(BF16), w2_ref[...])


def _store_vt(dst, v_ref):
    vt = v_ref[0].T
    ones = jnp.ones((BF16_ROWS, TQ), BF16)
    for g in range(NSA_KV):
        for c in range(dst.shape[1]):
            blk = vt[g * NSA_DH:(g + 1) * NSA_DH, c * TQ:(c + 1) * TQ].astype(BF16)
            dst[g, c] = jnp.concatenate([blk, ones], axis=0)


def _attend_chunks(chunks, slots, key_chunk, queries, vt_chunk, m_s, acc_s):
    scores = []
    for j, mask in chunks:
        kch = key_chunk(j)
        row = []
        for q in queries:
            s = _dot(kch, q)
            if mask is not None:
                s = jnp.where(mask, s, NEG)
            row.append(s.astype(BF16))
        scores.append(row)
    for (j, _), chunk_scores in zip(chunks, scores):
        probs, alphas = [], []
        for slot, s in zip(slots, chunk_scores):
            m_prev = m_s[slot]
            m_new = jnp.maximum(m_prev, jnp.max(s, axis=0, keepdims=True).astype(F32))
            probs.append(jnp.exp(s - m_new.astype(BF16)))
            alphas.append(jnp.exp(m_prev - m_new))
            m_s[slot] = m_new
        pv = [_dot(vt_chunk(n, j), p) for n, p in enumerate(probs)]
        for slot, a, o in zip(slots, alphas, pv):
            acc_s[slot] = a * acc_s[slot] + o


def _nsa_kernel(q_ref, sm_ref, kct_ref, vct_ref, ks_ref, vs_ref, kw_ref, vw_ref,
                kposa_ref, kposb_ref, kw1a_ref, kw1b_ref, kw2_ref,
                vposa_ref, vposb_ref, vw1a_ref, vw1b_ref, vw2_ref,
                out_ref,
                kc_s, vct_s, ksb, kwb, vst_s, vwt_s, qt_s, oc_s, m_s, acc_s):
    i = pl.program_id(1)
    ncmp = kc_s.shape[1]
    nblk = ks_ref.shape[1] // SEL_BLOCK
    heads = range(NSA_HEADS)

    @pl.when(i == 0)
    def _():
        kc = _compress(kct_ref, kposa_ref, kposb_ref, kw1a_ref, kw1b_ref, kw2_ref)
        kc_hi, kc_lo = _split_terms(kc, 2)
        kc_s[0] = kc_hi
        kc_s[1] = kc_lo
        vct_s[...] = _compress(vct_ref, vposa_ref, vposb_ref, vw1a_ref, vw1b_ref, vw2_ref).T.astype(BF16)
        s_len = ks_ref.shape[1]
        key_blk = lax.broadcasted_iota(jnp.int32, (s_len, LANES), 0) // SEL_BLOCK
        ksb[:, 0:LANES] = ks_ref[0].astype(BF16)
        ksb[:, LANES:] = (lax.broadcasted_iota(jnp.int32, (s_len, LANES), 1) == key_blk).astype(BF16)
        kwb[...] = kw_ref[0].astype(BF16)
        for idx in heads:
            qt_s[idx, LANES + nblk:, :] = jnp.zeros((LANES - nblk, TQ), BF16)
        _store_vt(vst_s, vs_ref)
        _store_vt(vwt_s, vw_ref)

    trow = i * TQ + lax.broadcasted_iota(jnp.int32, (1, TQ), 1)
    krel = lax.broadcasted_iota(jnp.int32, (TQ, 1), 0)
    cmp_end = lax.broadcasted_iota(jnp.int32, (ncmp, 1), 0) * CMP_STRIDE + (CMP_LEN - 1)
    valid_c = cmp_end <= trow
    oj = lax.broadcasted_iota(jnp.int32, (nblk, ncmp), 0) * SEL_BLOCK
    on = lax.broadcasted_iota(jnp.int32, (nblk, ncmp), 1) * CMP_STRIDE
    ov_t = ((on < oj + SEL_BLOCK) & (on + CMP_LEN > oj)).astype(BF16)
    jj = lax.broadcasted_iota(jnp.int32, (nblk, TQ), 0)
    cur = trow // SEL_BLOCK
    elig = jj <= cur
    forced = (jj == 0) | (jj == cur) | (jj == cur - 1)

    q_all = q_ref[0] * (NSA_DH ** -0.5)
    kc_hi, kc_lo = kc_s[0], kc_s[1]
    s_c = []
    for idx in heads:
        qt = q_all[:, idx * LANES:(idx + 1) * LANES].T
        qt_hi, qt_lo = _split_terms(qt, 2)
        qt_s[idx, 0:LANES, :] = qt_hi
        s_c.append(_dot(kc_hi, qt_hi) + _dot(kc_hi, qt_lo) + _dot(kc_lo, qt_hi))
    probs = []
    for idx in heads:
        s = jnp.where(valid_c, s_c[idx], NEG)
        e = jnp.where(valid_c, jnp.exp(s - jnp.max(s, axis=0, keepdims=True)), 0.0)
        probs.append(e / jnp.maximum(jnp.sum(e, axis=0, keepdims=True), 1e-30))
    for idx in heads:
        g = idx // NSA_HPG
        oc_s[idx] = _dot(vct_s[g * NSA_DH:(g + 1) * NSA_DH, :], probs[idx].astype(BF16))
    for g in range(NSA_KV):
        psum = sum(probs[g * NSA_HPG:(g + 1) * NSA_HPG])
        imp_t = _dot_01x(ov_t, psum, 3)
        score = jnp.where(elig, imp_t + jnp.where(forced, SEL_BIG, 0.0), -SEL_BIG)
        rank = jnp.zeros((nblk, TQ), F32)
        for r in range(nblk):
            row = score[r:r + 1, :]
            beats = (row > score) | ((row == score) & (r < jj))
            rank = rank + beats.astype(F32)
        bias = jnp.where(rank < float(N_SEL), 0.0, NEG).astype(BF16)
        for hh in range(NSA_HPG):
            qt_s[g * NSA_HPG + hh, LANES:LANES + nblk, :] = bias

    m_s[...] = jnp.full(m_s.shape, NEG, BF16).astype(F32)
    acc_s[...] = jnp.zeros(acc_s.shape, F32)
    qrel = lax.broadcasted_iota(jnp.int32, (1, TQ), 1)
    on_or_below = krel <= qrel

    def attend_sel(chunks):
        _attend_chunks(chunks, list(heads), lambda j: ksb[pl.ds(pl.multiple_of(j * TQ, TQ), TQ), :],
                       [qt_s[idx] for idx in heads], lambda n, j: vst_s[n // NSA_HPG, j], m_s, acc_s)

    def sel_pair(p, carry):
        attend_sel([(2 * p, None), (2 * p + 1, None)])
        return carry

    lax.fori_loop(0, i // 2, sel_pair, 0)

    @pl.when(i % 2 == 0)
    def _():
        attend_sel([(i, on_or_below)])

    @pl.when(i % 2 == 1)
    def _():
        attend_sel([(i - 1, None), (i, on_or_below)])

    def attend_win(chunks):
        _attend_chunks(chunks, [NSA_HEADS + idx for idx in heads],
                       lambda j: kwb[pl.ds(pl.multiple_of(j * TQ, TQ), TQ), :],
                       [qt_s[idx, 0:LANES, :] for idx in heads], lambda n, j: vwt_s[n // NSA_HPG, j], m_s, acc_s)

    @pl.when(i == 0)
    def _():
        attend_win([(i, on_or_below)])

    @pl.when(i >= 1)
    def _():
        attend_win([(i, on_or_below), (i - 1, None)])

    @pl.when(i >= 2)
    def _():
        attend_win([(i - 2, krel > qrel)])

    gates_t = _sigmoid(sm_ref[0]).T
    outs = []
    for idx in range(NSA_HEADS):
        acc = acc_s[idx]
        o_s = acc[:NSA_DH] * (1.0 / acc[NSA_DH:NSA_DH + 1])
        acc = acc_s[NSA_HEADS + idx]
        o_w = acc[:NSA_DH] * (1.0 / acc[NSA_DH:NSA_DH + 1])
        lg = LANE_GATE + idx
        outs.append(gates_t[lg:lg + 1] * oc_s[idx] + gates_t[lg + NSA_HEADS:lg + NSA_HEADS + 1] * o_s
                    + gates_t[lg + 2 * NSA_HEADS:lg + 2 * NSA_HEADS + 1] * o_w)
    out_ref[0] = jnp.concatenate(outs, axis=0).T.astype(out_ref.dtype)


def _nsa(nq, small, kct, vct, ks, vs, kw, vw, kparams, vparams):
    b, s, _ = nq.shape
    ncmp = kct.shape[1]
    nchunks = s // TQ
    tile = lambda bi, i: (bi, i, 0)
    full = lambda bi, i: (bi, 0, 0)
    const = lambda bi, i: (0, 0)
    wspecs = [pl.BlockSpec(p.shape, const) for p in kparams + vparams]
    return pl.pallas_call(
        _nsa_kernel,
        grid=(b, nchunks),
        in_specs=[pl.BlockSpec((1, TQ, NSA_HEADS * LANES), tile),
                  pl.BlockSpec((1, TQ, LANES), tile),
                  pl.BlockSpec((1, ncmp, kct.shape[2]), full),
                  pl.BlockSpec((1, ncmp, vct.shape[2]), full),
                  pl.BlockSpec((1, s, LANES), full),
                  pl.BlockSpec((1, s, LANES), full),
                  pl.BlockSpec((1, s, LANES), full),
                  pl.BlockSpec((1, s, LANES), full)] + wspecs,
        out_specs=pl.BlockSpec((1, TQ, NSA_HEADS * NSA_DH), tile),
        out_shape=jax.ShapeDtypeStruct((b, s, NSA_HEADS * NSA_DH), BF16),
        scratch_shapes=[pltpu.VMEM((2, ncmp, LANES), BF16),
                        pltpu.VMEM((LANES, ncmp), BF16),
                        pltpu.VMEM((s, 2 * LANES), BF16),
                        pltpu.VMEM((s, LANES), BF16),
                        pltpu.VMEM((NSA_KV, nchunks, VT_ROWS, TQ), BF16),
                        pltpu.VMEM((NSA_KV, nchunks, VT_ROWS, TQ), BF16),
                        pltpu.VMEM((NSA_HEADS, 2 * LANES, TQ), BF16),
                        pltpu.VMEM((NSA_HEADS, NSA_DH, TQ), F32),
                        pltpu.VMEM((2 * NSA_HEADS, 1, TQ), F32),
                        pltpu.VMEM((2 * NSA_HEADS, VT_ROWS, TQ), F32)],
        compiler_params=pltpu.CompilerParams(dimension_semantics=("parallel", "arbitrary"),
                                             vmem_limit_bytes=VMEM_LIMIT),
        name="nsa",
    )(nq, small, kct, vct, ks, vs, kw, vw, *kparams, *vparams)


def _source_columns():
    o = 0
    qkv = list(range(o, o + DN_QKV)); o += DN_QKV
    z = list(range(o, o + DN_HEADS * DN_DV)); o += DN_HEADS * DN_DV
    b_raw = list(range(o, o + DN_HEADS)); o += DN_HEADS
    a_raw = list(range(o, o + DN_HEADS)); o += DN_HEADS
    cf_u = list(range(o, o + 2 * CF_CH)); o += 2 * CF_CH
    n_q = list(range(o, o + NSA_HEADS * NSA_DH)); o += NSA_HEADS * NSA_DH
    kv = []
    for _ in range(6):
        kv.append(list(range(o, o + NSA_KV * NSA_DH))); o += NSA_KV * NSA_DH
    gate = list(range(o, o + 3 * NSA_HEADS)); o += 3 * NSA_HEADS
    n_kc, n_vc, n_ks, n_vs, n_kw, n_vw = kv
    small = b_raw + a_raw + gate
    small = small + [-1] * (LANES - len(small))
    nq = []
    zero = [-1] * NSA_DH
    for idx in range(NSA_HEADS):
        qh = n_q[idx * NSA_DH:(idx + 1) * NSA_DH]
        nq += (qh + zero) if idx // NSA_HPG == 0 else (zero + qh)
    cols = qkv + z + small + cf_u + nq + n_ks + n_vs + n_kw + n_vw + n_kc + n_vc
    assert len(cols) == sum(SEC_WIDTHS)
    return np.asarray(cols, np.int32), o


def _permute_w_in(w):
    cols, n_in = _source_columns()
    assert w.shape[1] == n_in
    select = (lax.broadcasted_iota(jnp.int32, (n_in, cols.shape[0]), 0) == cols[None, :]).astype(BF16)
    return jnp.dot(w.astype(BF16), select, preferred_element_type=BF16)


def _compress_params(pos, w1, w2):
    eye = jnp.eye(NSA_KV, dtype=F32)
    half = CMP_LEN // 2
    w1r = w1.reshape(CMP_LEN, NSA_DH, CMP_HID)

    def expand_w1(part):
        return jnp.einsum("ldj,gh->lgdhj", part, eye).reshape(half * NSA_KV * NSA_DH, NSA_KV * CMP_HID).astype(BF16)

    def expand_pos(part):
        return jnp.broadcast_to(part[:, None, :], (half, NSA_KV, NSA_DH)).reshape(1, half * NSA_KV * NSA_DH)

    w2e = jnp.einsum("jd,gh->gjhd", w2, eye).reshape(NSA_KV * CMP_HID, NSA_KV * NSA_DH).astype(BF16)
    return [expand_pos(pos[:half]), expand_pos(pos[half:]), expand_w1(w1r[:half]), expand_w1(w1r[half:]), w2e]


def _lane_row(vals, first_lane):
    pad = jnp.zeros((LANES - first_lane - vals.shape[0],), F32)
    return jnp.concatenate([jnp.zeros((first_lane,), F32), vals.astype(F32), pad])[None, :]


def kernel(x, norm_w, w_in, dn_conv_w, dn_a_log, dn_dt_bias, dn_norm_w, cf_dw_w, cf_dw_b, cf_ln_w, cf_ln_b,
           nsa_k_pos, nsa_v_pos, nsa_k_w1, nsa_k_w2, nsa_v_w1, nsa_v_w2, w_out, ffn_w_gate, ffn_w_up, ffn_w_down):
    b, s, d = x.shape
    depth = w_in.shape[0]
    assert s % SUPER == 0 and s % TQ == 0 and s // CMP_STRIDE == LANES and s // SEL_BLOCK <= LANES
    assert WINDOW == 2 * TQ and TQ % SEL_BLOCK == 0
    m = b * s
    tm = 512 if m % 512 == 0 else 256
    x2 = x.reshape(m, d)
    ndn = DN_HEADS * DN_DV
    rows16 = s // CMP_STRIDE
    for l in range(depth):
        secs = _in_proj(x2, norm_w[l, 0:1], _permute_w_in(w_in[l]), cf_dw_w[l], cf_dw_b[l][None, :],
                        cf_ln_w[l][None, :], cf_ln_b[l][None, :], s, tm)
        qkv, z, small, o_cf, nq, n_ks, n_vs, n_kw, n_vw = [t.reshape(b, s, t.shape[1]) for t in secs[:N_ROWMAJOR]]
        n_kct, n_vct = [t.reshape(b, rows16, t.shape[1]) for t in secs[N_ROWMAJOR:]]
        o_dn = _deltanet(qkv, z, small, dn_conv_w[l], _lane_row(dn_a_log[l], LANE_A), _lane_row(dn_dt_bias[l], LANE_A),
                         jnp.tile(dn_norm_w[l], DN_HEADS)[None, :])
        o_nsa = _nsa(nq, small, n_kct, n_vct, n_ks, n_vs, n_kw, n_vw,
                     _compress_params(nsa_k_pos[l], nsa_k_w1[l], nsa_k_w2[l]),
                     _compress_params(nsa_v_pos[l], nsa_v_w1[l], nsa_v_w2[l]))
        wo = w_out[l].astype(BF16)
        x2 = _mix_ffn(x2, o_dn.reshape(m, ndn), o_cf.reshape(m, CF_CH), o_nsa.reshape(m, NSA_HEADS * NSA_DH),
                      wo[:ndn], wo[ndn:ndn + CF_CH], wo[ndn + CF_CH:], norm_w[l],
                      ffn_w_gate[l].astype(BF16), ffn_w_up[l].astype(BF16), ffn_w_down[l].astype(BF16), tm)
    return x2.reshape(b, s, d)
```

```python
import jax
import jax.numpy as jnp
import numpy as np
from jax import lax
from jax.experimental import pallas as pl
from jax.experimental.pallas import tpu as pltpu

F32 = jnp.float32
BF16 = jnp.bfloat16

DN_HEADS = 6
DN_DK = 64
DN_DV = 64
DN_CONV = 4
DN_CHUNK = 64
CF_CH = 256
CF_KERNEL = 31
NSA_HEADS = 6
NSA_KV = 2
NSA_HPG = NSA_HEADS // NSA_KV
NSA_DH = 64
CMP_LEN = 32
CMP_STRIDE = 16
CMP_HID = 2 * NSA_DH
SEL_BLOCK = 64
N_SEL = 8
WINDOW = 512
EPS = 1e-6
NEG = -1e30
SEL_BIG = 1e4

LANES = 128
SUBLANES = 8
BF16_ROWS = 16
DN_QK = DN_HEADS * DN_DK
DN_QKV = 2 * DN_QK + DN_HEADS * DN_DV
SUPER = 256
DN_BATCH = 2
NSA_BATCH = 2
TQ = 256
VT_ROWS = NSA_DH + BF16_ROWS
VMEM_LIMIT = 56 * 1024 * 1024

SEC_WIDTHS = (DN_QKV, DN_HEADS * DN_DV, LANES, 2 * CF_CH, NSA_HEADS * LANES,
              LANES, LANES, LANES, LANES, LANES, LANES)
N_ROWMAJOR = 9
SEC_CF = 3
SEC_BF16 = (5, 6, 7, 8)
CF_ROWS = 64
CF_PAD = 32
LANE_BETA = 0
LANE_A = DN_HEADS
LANE_GATE = 2 * DN_HEADS


def _dot(a, b, precision=None):
    return jnp.dot(a, b, preferred_element_type=F32, precision=precision)


def _split_terms(x, terms):
    out = []
    for _ in range(terms):
        hi = x.astype(BF16)
        out.append(hi)
        x = x - hi.astype(F32)
    return out


def _dot_x01(x, mat01, terms):
    return sum(_dot(p, mat01) for p in _split_terms(x, terms))


def _dot_01x(mat01, x, terms):
    return sum(_dot(mat01, p) for p in _split_terms(x, terms))


def _sigmoid(x):
    return 1.0 / (1.0 + jnp.exp(-x))


def _silu(x):
    return x * _sigmoid(x)


def _rms(x, w):
    return x * lax.rsqrt(jnp.mean(x * x, axis=-1, keepdims=True) + EPS) * w


IN_PROJ_COLS = 1280


def _merged_sections():
    runs, first, width = [], 0, 0
    for k, wd in enumerate(SEC_WIDTHS):
        if width and width + wd > IN_PROJ_COLS:
            runs.append((first, k))
            first, width = k, 0
        width += wd
    runs.append((first, len(SEC_WIDTHS)))
    return runs


def _conformer_rows(r, dw_ref, db_ref, lnw_ref, lnb_ref, out_ref, hbuf):
    first = CF_PAD - (CF_KERNEL - 1)
    acc = jnp.zeros((CF_ROWS, CF_CH), F32) + db_ref[...]
    for phase in range(SUBLANES):
        part = None
        for j in range(CF_KERNEL):
            if (first + j) % SUBLANES != phase:
                continue
            base = r * CF_ROWS + first + j - phase
            rows = CF_ROWS + (SUBLANES if phase else 0)
            term = dw_ref[j:j + 1, :] * hbuf[pl.ds(base, rows), :]
            part = term if part is None else part + term
        if part is not None:
            acc = acc + part[phase:phase + CF_ROWS, :]
    mu = jnp.mean(acc, axis=-1, keepdims=True)
    cen = acc - mu
    var = jnp.mean(cen * cen, axis=-1, keepdims=True)
    y = cen * lax.rsqrt(var + EPS) * lnw_ref[...] + lnb_ref[...]
    out_ref[r * CF_ROWS:(r + 1) * CF_ROWS, :] = _silu(y).astype(out_ref.dtype)


def _in_proj_kernel(x_ref, nw_ref, w_ref, dw_ref, db_ref, lnw_ref, lnb_ref, *refs):
    out_refs, stage, hbuf = refs[:-2], refs[-2], refs[-1]
    tm = x_ref.shape[0]

    @pl.when(pl.program_id(1) == 0)
    def _():
        hbuf[0:CF_PAD, :] = jnp.zeros((CF_PAD, CF_CH), F32)

    h = _rms(x_ref[...], nw_ref[...]).astype(BF16)
    rows16 = tm // CMP_STRIDE

    def emit(first, last):
        off = sum(SEC_WIDTHS[:first])
        wide = _dot(h, w_ref[:, off:off + sum(SEC_WIDTHS[first:last])])
        col = 0
        for k in range(first, last):
            y = wide[:, col:col + SEC_WIDTHS[k]]
            col += SEC_WIDTHS[k]
            if k == SEC_CF:
                hbuf[CF_PAD:CF_PAD + tm, :] = y[:, :CF_CH] * _sigmoid(y[:, CF_CH:])
            elif k < N_ROWMAJOR:
                out_refs[k][...] = y.astype(out_refs[k].dtype)
            else:
                stage[...] = y
                for t in range(CMP_STRIDE):
                    out_refs[k][:, t * LANES:(t + 1) * LANES] = stage[pl.ds(t, rows16, stride=CMP_STRIDE), :]

    runs = sorted(_merged_sections(), key=lambda r: not (r[0] <= SEC_CF < r[1]))
    emit(*runs[0])
    nblocks = tm // CF_ROWS
    done = 0
    for n, run in enumerate(runs[1:], start=1):
        upto = nblocks * n // (len(runs) - 1)
        for r in range(done, upto):
            _conformer_rows(r, dw_ref, db_ref, lnw_ref, lnb_ref, out_refs[SEC_CF], hbuf)
        done = upto
        emit(*run)
    hbuf[0:CF_PAD, :] = hbuf[tm:tm + CF_PAD, :]


def _in_proj(x2, nw, w_perm, dw, db, lnw, lnb, seq, tm):
    m, d = x2.shape
    n = w_perm.shape[1]
    nt = seq // tm
    shapes, blocks, dtypes = [], [], []
    for k, wd in enumerate(SEC_WIDTHS):
        if k == SEC_CF:
            shapes.append((m, CF_CH)); blocks.append((tm, CF_CH)); dtypes.append(BF16)
        elif k < N_ROWMAJOR:
            shapes.append((m, wd)); blocks.append((tm, wd)); dtypes.append(BF16 if k in SEC_BF16 else F32)
        else:
            shapes.append((m // CMP_STRIDE, CMP_STRIDE * wd)); blocks.append((tm // CMP_STRIDE, CMP_STRIDE * wd))
            dtypes.append(F32)
    row = lambda bi, t: (bi * nt + t, 0)
    const = lambda bi, t: (0, 0)
    return pl.pallas_call(
        _in_proj_kernel,
        grid=(m // seq, nt),
        in_specs=[pl.BlockSpec((tm, d), row),
                  pl.BlockSpec((1, d), const),
                  pl.BlockSpec((d, n), const),
                  pl.BlockSpec((CF_KERNEL, CF_CH), const),
                  pl.BlockSpec((1, CF_CH), const),
                  pl.BlockSpec((1, CF_CH), const),
                  pl.BlockSpec((1, CF_CH), const)],
        out_specs=[pl.BlockSpec(blk, row) for blk in blocks],
        out_shape=[jax.ShapeDtypeStruct(shp, dt) for shp, dt in zip(shapes, dtypes)],
        scratch_shapes=[pltpu.VMEM((tm, LANES), F32),
                        pltpu.VMEM((tm + CF_PAD, CF_CH), F32)],
        compiler_params=pltpu.CompilerParams(dimension_semantics=("parallel", "arbitrary"),
                                             vmem_limit_bytes=VMEM_LIMIT),
        name="in_proj",
    )(x2, nw, w_perm, dw, db, lnw, lnb)


def _mix_ffn_kernel(x_ref, odn_ref, ocf_ref, onsa_ref, wodn_ref, wocf_ref, wonsa_ref, nw_ref,
                    wg_ref, wu_ref, wd_ref, out_ref):
    mix = (_dot(odn_ref[...], wodn_ref[...]) + _dot(ocf_ref[...], wocf_ref[...])
           + _dot(onsa_ref[...], wonsa_ref[...]))
    x1 = x_ref[...] + _rms(mix, nw_ref[1:2, :])
    h = _rms(x1, nw_ref[2:3, :]).astype(BF16)
    g = _dot(h, wg_ref[...])
    u = _dot(h, wu_ref[...])
    a = (_silu(g) * u).astype(BF16)
    f = _dot(a, wd_ref[...])
    out_ref[...] = x1 + _rms(f, nw_ref[3:4, :])


def _mix_ffn(x2, odn, ocf, onsa, wodn, wocf, wonsa, nw, wg, wu, wd, tm):
    m, d = x2.shape
    dff = wg.shape[1]
    const = lambda i: (0, 0)
    row = lambda i: (i, 0)
    return pl.pallas_call(
        _mix_ffn_kernel,
        grid=(m // tm,),
        in_specs=[pl.BlockSpec((tm, d), row),
                  pl.BlockSpec((tm, odn.shape[1]), row),
                  pl.BlockSpec((tm, ocf.shape[1]), row),
                  pl.BlockSpec((tm, onsa.shape[1]), row),
                  pl.BlockSpec(wodn.shape, const),
                  pl.BlockSpec(wocf.shape, const),
                  pl.BlockSpec(wonsa.shape, const),
                  pl.BlockSpec(nw.shape, const),
                  pl.BlockSpec((d, dff), const),
                  pl.BlockSpec((d, dff), const),
                  pl.BlockSpec((dff, d), const)],
        out_specs=pl.BlockSpec((tm, d), row),
        out_shape=jax.ShapeDtypeStruct((m, d), F32),
        compiler_params=pltpu.CompilerParams(dimension_semantics=("parallel",), vmem_limit_bytes=VMEM_LIMIT),
        name="mix_ffn",
    )(x2, odn, ocf, onsa, wodn, wocf, wonsa, nw, wg, wu, wd)


DN_PAD = 8


def _deltanet_kernel(qkv_ref, z_ref, sm_ref, cw_ref, alog_ref, dtb_ref, nw_ref, out_ref, xbuf, ybuf, state):
    n = SUPER
    nchunk = n // DN_CHUNK
    nb = qkv_ref.shape[0]

    @pl.when(pl.program_id(1) == 0)
    def _():
        xbuf[:, 0:DN_PAD, :] = jnp.zeros((nb, DN_PAD, DN_QKV), F32)
        state[...] = jnp.zeros(state.shape, F32)

    hr = lax.broadcasted_iota(jnp.int32, (DN_QK, DN_QK), 0) // DN_DK
    hc = lax.broadcasted_iota(jnp.int32, (DN_QK, DN_QK), 1) // DN_DK
    head_ones = (hr == hc).astype(BF16)
    ri = lax.broadcasted_iota(jnp.int32, (n, n), 0)
    ci = lax.broadcasted_iota(jnp.int32, (n, n), 1)
    same_chunk = (ri // DN_CHUNK) == (ci // DN_CHUNK)
    causal = same_chunk & (ri >= ci)
    strict = same_chunk & (ri > ci)
    causal_bf = causal.astype(BF16)
    same_chunk_bf = same_chunk.astype(BF16)
    nfac = (DN_CHUNK - 1).bit_length() - 1
    lane_chunk = lax.broadcasted_iota(jnp.int32, (1, n), 1) // DN_CHUNK
    in_chunk = [lane_chunk == c for c in range(nchunk)]
    in_chunk_bf = [m.astype(BF16) for m in in_chunk]

    def to_rowform(bd):
        return sum(jnp.where(in_chunk[c], bd[c * DN_CHUNK:(c + 1) * DN_CHUNK, :], 0.0) for c in range(nchunk))

    def to_blockdiag(rowform):
        return jnp.concatenate([rowform * in_chunk_bf[c] for c in range(nchunk)], axis=0)

    first = DN_PAD - (DN_CONV - 1)
    rows = []
    for bi in range(nb):
        xbuf[bi, DN_PAD:DN_PAD + n, :] = qkv_ref[bi]
        y = cw_ref[DN_CONV - 1:DN_CONV, :] * xbuf[bi, pl.ds(DN_PAD, n), :]
        for j in range(DN_CONV - 1):
            y = y + cw_ref[j:j + 1, :] * xbuf[bi, pl.ds(first + j, n), :]
        xbuf[bi, 0:DN_PAD, :] = xbuf[bi, n:n + DN_PAD, :]
        ybuf[bi] = _silu(y)
        q_all = ybuf[bi, :, 0:DN_QK]
        k_all = ybuf[bi, :, DN_QK:2 * DN_QK]
        v_all = ybuf[bi, :, 2 * DN_QK:]
        q_n = q_all * lax.rsqrt(_dot_x01(q_all * q_all, head_ones, 1) + EPS) * (DN_DK ** -0.5)
        k_n = k_all * lax.rsqrt(_dot_x01(k_all * k_all, head_ones, 1) + EPS)
        sm = sm_ref[bi]
        beta_all = _sigmoid(sm)
        sp_in = sm + dtb_ref[...]
        softplus = jnp.maximum(sp_in, 0.0) + jnp.log1p(jnp.exp(-jnp.abs(sp_in)))
        g_all = -jnp.exp(alog_ref[...]) * softplus
        gam = _dot_01x(causal_bf, g_all, 3)
        glast = _dot_01x(same_chunk_bf, g_all, 3)
        rows.append(dict(q_n=q_n, k_n=k_n, v_all=v_all, beta_all=beta_all, gam=gam, gam_t=gam.T,
                         e_gam=jnp.exp(gam), e_tot=jnp.exp(glast), e_rest_t=jnp.exp(glast - gam).T,
                         k_t=k_n.T))

    heads = [(bi, h) for bi in range(nb) for h in range(DN_HEADS)]
    q, k, v, bcol, egcol, etcol, lmat, qkd = {}, {}, {}, {}, {}, {}, {}, {}
    for key in heads:
        bi, h = key
        r = rows[bi]
        sl = slice(h * DN_DK, (h + 1) * DN_DK)
        lg = LANE_A + h
        q[key], k[key], v[key] = r["q_n"][:, sl], r["k_n"][:, sl], r["v_all"][:, sl]
        bcol[key] = r["beta_all"][:, LANE_BETA + h:LANE_BETA + h + 1]
        egcol[key] = r["e_gam"][:, lg:lg + 1]
        etcol[key] = r["e_tot"][:, lg:lg + 1]
        kt = r["k_t"][h * DN_DK:(h + 1) * DN_DK, :].astype(BF16)
        dec = jnp.exp(jnp.where(causal, r["gam"][:, lg:lg + 1] - r["gam_t"][lg:lg + 1, :], NEG))
        lmat[key] = jnp.where(strict, _dot((k[key] * bcol[key]).astype(BF16), kt) * dec, 0.0)
        qkd[key] = (_dot(q[key].astype(BF16), kt) * dec).astype(BF16)
    ps = {key: -to_rowform(lmat[key]) for key in heads}
    mp = {key: _dot((-ps[key]).astype(BF16), lmat[key].astype(BF16)) for key in heads}
    for s in range(nfac):
        mb = {key: mp[key].astype(BF16) for key in heads}
        mbd = {key: to_blockdiag(mb[key]) for key in heads}
        prod = {key: _dot(ps[key].astype(BF16), mbd[key]) for key in heads}
        ps = {key: ps[key] + mp[key] + prod[key] for key in heads}
        if s < nfac - 1:
            mp = {key: _dot(mb[key], mbd[key]) for key in heads}
    qeff, o_intra, b_in, n_in = {}, {}, {}, {}
    rhs = {key: jnp.concatenate([v[key] * bcol[key], k[key] * (bcol[key] * egcol[key])], axis=1) for key in heads}
    tx = {key: _dot(to_blockdiag(ps[key].astype(BF16)), rhs[key].astype(BF16)) for key in heads}
    x = {key: (rhs[key] + tx[key]).astype(BF16) for key in heads}
    qx = {key: _dot(qkd[key], x[key]) for key in heads}
    kdt = {(bi, h): (rows[bi]["k_t"][h * DN_DK:(h + 1) * DN_DK, :]
                     * rows[bi]["e_rest_t"][LANE_A + h:LANE_A + h + 1, :]).astype(BF16) for bi, h in heads}
    kx = {key: _dot(to_blockdiag(kdt[key]), x[key]) for key in heads}
    for key in heads:
        qeff[key] = (q[key] * egcol[key] - qx[key][:, DN_DV:]).astype(BF16)
        o_intra[key] = qx[key][:, :DN_DV]
        b_in[key], n_in[key] = kx[key][:, :DN_DV], kx[key][:, DN_DV:].astype(BF16)
    st = {(bi, h): state[bi * DN_HEADS + h] for bi, h in heads}
    o_chunks = {key: [] for key in heads}
    for c in range(nchunk):
        rs = slice(c * DN_CHUNK, (c + 1) * DN_CHUNK)
        ks = slice(c * DN_DK, (c + 1) * DN_DK)
        stb = {key: st[key].astype(BF16) for key in heads}
        corr = {key: _dot(n_in[key][ks], stb[key]) for key in heads}
        for key in heads:
            o_chunks[key].append(_dot(qeff[key][rs], stb[key]) + o_intra[key][rs])
            st[key] = st[key] * etcol[key][rs] + b_in[key][ks] - corr[key]
    for bi in range(nb):
        for h in range(DN_HEADS):
            state[bi * DN_HEADS + h] = st[(bi, h)]
        o = jnp.concatenate([jnp.concatenate(o_chunks[(bi, h)], axis=0) for h in range(DN_HEADS)], axis=1)
        ms = _dot_x01(o * o, head_ones, 1) * (1.0 / DN_DV)
        o = o * lax.rsqrt(ms + EPS) * nw_ref[...]
        out_ref[bi] = (o * _silu(z_ref[bi])).astype(out_ref.dtype)


def _deltanet(qkv, z, small, cw, alog_row, dtb_row, nw_row):
    b, s, _ = qkv.shape
    nb = DN_BATCH if b % DN_BATCH == 0 else 1
    const = lambda i, t: (0, 0)
    tile = lambda i, t: (i, t, 0)
    return pl.pallas_call(
        _deltanet_kernel,
        grid=(b // nb, s // SUPER),
        in_specs=[pl.BlockSpec((nb, SUPER, DN_QKV), tile),
                  pl.BlockSpec((nb, SUPER, DN_QK), tile),
                  pl.BlockSpec((nb, SUPER, LANES), tile),
                  pl.BlockSpec((DN_CONV, DN_QKV), const),
                  pl.BlockSpec((1, LANES), const),
                  pl.BlockSpec((1, LANES), const),
                  pl.BlockSpec((1, DN_QK), const)],
        out_specs=pl.BlockSpec((nb, SUPER, DN_QK), tile),
        out_shape=jax.ShapeDtypeStruct((b, s, DN_QK), BF16),
        scratch_shapes=[pltpu.VMEM((nb, SUPER + DN_PAD, DN_QKV), F32),
                        pltpu.VMEM((nb, SUPER, DN_QKV), F32),
                        pltpu.VMEM((nb * DN_HEADS, DN_DK, DN_DV), F32)],
        compiler_params=pltpu.CompilerParams(dimension_semantics=("parallel", "arbitrary"),
                                             vmem_limit_bytes=VMEM_LIMIT),
        name="deltanet",
    )(qkv, z, small, cw, alog_row, dtb_row, nw_row)


def _compress(t_ref, row, posa_ref, posb_ref, w1a_ref, w1b_ref, w2_ref):
    t = t_ref[row]
    nrow = t.shape[0]
    p = _dot((t + posa_ref[...]).astype(BF16), w1a_ref[...])
    q = _dot((t + posb_ref[...]).astype(BF16), w1b_ref[...])
    hid = _silu(p + pltpu.roll(q, nrow - 1, 0))
    return _dot(hid.astype(BF16), w2_ref[...])


def _store_vt(dst, row, v_ref):
    vt = v_ref[row].astype(F32).T
    ones = jnp.ones((BF16_ROWS, TQ), BF16)
    for g in range(NSA_KV):
        for c in range(dst.shape[1]):
            blk = vt[g * NSA_DH:(g + 1) * NSA_DH, c * TQ:(c + 1) * TQ].astype(BF16)
            dst[row * NSA_KV + g, c] = jnp.concatenate([blk, ones], axis=0)


def _attend_chunks(chunks, items, m_s, acc_s):
    scores = []
    for j, mask in chunks:
        row = []
        for _, q, key_chunk, _ in items:
            s = _dot(key_chunk(j), q)
            if mask is not None:
                s = jnp.where(mask, s, NEG)
            row.append(s.astype(BF16))
        scores.append(row)
    for (j, _), chunk_scores in zip(chunks, scores):
        probs, alphas = [], []
        for (slot, _, _, _), s in zip(items, chunk_scores):
            m_prev = m_s[slot]
            m_new = jnp.maximum(m_prev, jnp.max(s, axis=0, keepdims=True).astype(F32))
            probs.append(jnp.exp(s - m_new.astype(BF16)))
            alphas.append(jnp.exp(m_prev - m_new))
            m_s[slot] = m_new
        pv = [_dot(vt_chunk(j), p) for (_, _, _, vt_chunk), p in zip(items, probs)]
        for (slot, _, _, _), a, o in zip(items, alphas, pv):
            acc_s[slot] = a * acc_s[slot] + o


def _nsa_kernel(q_ref, sm_ref, kct_ref, vct_ref, ks_ref, vs_ref, kw_ref, vw_ref,
                kposa_ref, kposb_ref, kw1a_ref, kw1b_ref, kw2_ref,
                vposa_ref, vposb_ref, vw1a_ref, vw1b_ref, vw2_ref,
                out_ref,
                kc_s, vct_s, ksb, kwb, vst_s, vwt_s, qt_s, oc_s, m_s, acc_s):
    i = pl.program_id(1)
    nb = q_ref.shape[0]
    ncmp = kc_s.shape[1]
    nblk = ks_ref.shape[1] // SEL_BLOCK
    heads = [(bi, idx) for bi in range(nb) for idx in range(NSA_HEADS)]
    slot = lambda bi, idx: bi * NSA_HEADS + idx

    @pl.when(i == 0)
    def _():
        s_len = ks_ref.shape[1]
        key_blk = lax.broadcasted_iota(jnp.int32, (s_len, LANES), 0) // SEL_BLOCK
        blk_onehot = (lax.broadcasted_iota(jnp.int32, (s_len, LANES), 1) == key_blk).astype(BF16)
        for bi in range(nb):
            kc = _compress(kct_ref, bi, kposa_ref, kposb_ref, kw1a_ref, kw1b_ref, kw2_ref)
            kc_hi, kc_lo = _split_terms(kc, 2)
            kc_s[2 * bi] = kc_hi
            kc_s[2 * bi + 1] = kc_lo
            vct_s[bi] = _compress(vct_ref, bi, vposa_ref, vposb_ref, vw1a_ref, vw1b_ref, vw2_ref).T.astype(BF16)
            ksb[bi, :, 0:LANES] = ks_ref[bi]
            ksb[bi, :, LANES:] = blk_onehot
            kwb[bi] = kw_ref[bi]
            for idx in range(NSA_HEADS):
                qt_s[slot(bi, idx), LANES + nblk:, :] = jnp.zeros((LANES - nblk, TQ), BF16)
            _store_vt(vst_s, bi, vs_ref)
            _store_vt(vwt_s, bi, vw_ref)

    trow = i * TQ + lax.broadcasted_iota(jnp.int32, (1, TQ), 1)
    krel = lax.broadcasted_iota(jnp.int32, (TQ, 1), 0)
    cmp_end = lax.broadcasted_iota(jnp.int32, (ncmp, 1), 0) * CMP_STRIDE + (CMP_LEN - 1)
    valid_c = cmp_end <= trow
    oj = lax.broadcasted_iota(jnp.int32, (nblk, ncmp), 0) * SEL_BLOCK
    on = lax.broadcasted_iota(jnp.int32, (nblk, ncmp), 1) * CMP_STRIDE
    ov_t = ((on < oj + SEL_BLOCK) & (on + CMP_LEN > oj)).astype(BF16)
    jj = lax.broadcasted_iota(jnp.int32, (nblk, TQ), 0)
    cur = trow // SEL_BLOCK
    elig = jj <= cur
    forced = (jj == 0) | (jj == cur) | (jj == cur - 1)

    s_c = {}
    for bi in range(nb):
        q_all = q_ref[bi] * (NSA_DH ** -0.5)
        kc_hi, kc_lo = kc_s[2 * bi], kc_s[2 * bi + 1]
        for idx in range(NSA_HEADS):
            qt = q_all[:, idx * LANES:(idx + 1) * LANES].T
            qt_hi, qt_lo = _split_terms(qt, 2)
            qt_s[slot(bi, idx), 0:LANES, :] = qt_hi
            s_c[(bi, idx)] = _dot(kc_hi, qt_hi) + _dot(kc_hi, qt_lo) + _dot(kc_lo, qt_hi)
    probs = {}
    for key in heads:
        s = jnp.where(valid_c, s_c[key], NEG)
        e = jnp.where(valid_c, jnp.exp(s - jnp.max(s, axis=0, keepdims=True)), 0.0)
        probs[key] = e * (1.0 / jnp.maximum(jnp.sum(e, axis=0, keepdims=True), 1e-30))
    for bi, idx in heads:
        g = idx // NSA_HPG
        oc_s[slot(bi, idx)] = _dot(vct_s[bi, g * NSA_DH:(g + 1) * NSA_DH, :], probs[(bi, idx)].astype(BF16))
    for bi in range(nb):
        for g in range(NSA_KV):
            psum = sum(probs[(bi, idx)] for idx in range(g * NSA_HPG, (g + 1) * NSA_HPG))
            imp_t = _dot_01x(ov_t, psum, 3)
            score = jnp.where(elig, imp_t + jnp.where(forced, SEL_BIG, 0.0), -SEL_BIG)
            rank = jnp.zeros((nblk, TQ), F32)
            for r in range(nblk):
                row = score[r:r + 1, :]
                beats = (row > score) | ((row == score) & (r < jj))
                rank = rank + beats.astype(F32)
            bias = jnp.where(rank < float(N_SEL), 0.0, NEG).astype(BF16)
            for hh in range(NSA_HPG):
                qt_s[slot(bi, g * NSA_HPG + hh), LANES:LANES + nblk, :] = bias

    m_s[...] = jnp.full(m_s.shape, NEG, BF16).astype(F32)
    acc_s[...] = jnp.zeros(acc_s.shape, F32)
    qrel = lax.broadcasted_iota(jnp.int32, (1, TQ), 1)
    on_or_below = krel <= qrel
    nslots = nb * NSA_HEADS

    sel_items = [(slot(bi, idx), qt_s[slot(bi, idx)],
                  lambda j, bi=bi: ksb[bi, pl.ds(pl.multiple_of(j * TQ, TQ), TQ), :],
                  lambda j, bi=bi, idx=idx: vst_s[bi * NSA_KV + idx // NSA_HPG, j]) for bi, idx in heads]

    def sel_pair(p, carry):
        _attend_chunks([(2 * p, None), (2 * p + 1, None)], sel_items, m_s, acc_s)
        return carry

    lax.fori_loop(0, i // 2, sel_pair, 0)

    @pl.when(i % 2 == 0)
    def _():
        _attend_chunks([(i, on_or_below)], sel_items, m_s, acc_s)

    @pl.when(i % 2 == 1)
    def _():
        _attend_chunks([(i - 1, None), (i, on_or_below)], sel_items, m_s, acc_s)

    win_items = [(nslots + slot(bi, idx), qt_s[slot(bi, idx), 0:LANES, :],
                  lambda j, bi=bi: kwb[bi, pl.ds(pl.multiple_of(j * TQ, TQ), TQ), :],
                  lambda j, bi=bi, idx=idx: vwt_s[bi * NSA_KV + idx // NSA_HPG, j]) for bi, idx in heads]

    @pl.when(i == 0)
    def _():
        _attend_chunks([(i, on_or_below)], win_items, m_s, acc_s)

    @pl.when(i >= 1)
    def _():
        _attend_chunks([(i, on_or_below), (i - 1, None)], win_items, m_s, acc_s)

    @pl.when(i >= 2)
    def _():
        _attend_chunks([(i - 2, krel > qrel)], win_items, m_s, acc_s)

    for bi in range(nb):
        gates_t = _sigmoid(sm_ref[bi]).T
        outs = []
        for idx in range(NSA_HEADS):
            acc = acc_s[slot(bi, idx)]
            o_s = acc[:NSA_DH] * (1.0 / acc[NSA_DH:NSA_DH + 1])
            acc = acc_s[nslots + slot(bi, idx)]
            o_w = acc[:NSA_DH] * (1.0 / acc[NSA_DH:NSA_DH + 1])
            lg = LANE_GATE + idx
            outs.append(gates_t[lg:lg + 1] * oc_s[slot(bi, idx)] + gates_t[lg + NSA_HEADS:lg + NSA_HEADS + 1] * o_s
                        + gates_t[lg + 2 * NSA_HEADS:lg + 2 * NSA_HEADS + 1] * o_w)
        out_ref[bi] = jnp.concatenate(outs, axis=0).T.astype(out_ref.dtype)


def _nsa(nq, small, kct, vct, ks, vs, kw, vw, kparams, vparams):
    b, s, _ = nq.shape
    nb = NSA_BATCH if b % NSA_BATCH == 0 else 1
    ncmp = kct.shape[1]
    nchunks = s // TQ
    tile = lambda bi, i: (bi, i, 0)
    full = lambda bi, i: (bi, 0, 0)
    const = lambda bi, i: (0, 0)
    wspecs = [pl.BlockSpec(p.shape, const, pipeline_mode=pl.Buffered(1)) for p in kparams + vparams]
    return pl.pallas_call(
        _nsa_kernel,
        grid=(b // nb, nchunks),
        in_specs=[pl.BlockSpec((nb, TQ, NSA_HEADS * LANES), tile),
                  pl.BlockSpec((nb, TQ, LANES), tile),
                  pl.BlockSpec((nb, ncmp, kct.shape[2]), full),
                  pl.BlockSpec((nb, ncmp, vct.shape[2]), full),
                  pl.BlockSpec((nb, s, LANES), full),
                  pl.BlockSpec((nb, s, LANES), full),
                  pl.BlockSpec((nb, s, LANES), full),
                  pl.BlockSpec((nb, s, LANES), full)] + wspecs,
        out_specs=pl.BlockSpec((nb, TQ, NSA_HEADS * NSA_DH), tile),
        out_shape=jax.ShapeDtypeStruct((b, s, NSA_HEADS * NSA_DH), BF16),
        scratch_shapes=[pltpu.VMEM((2 * nb, ncmp, LANES), BF16),
                        pltpu.VMEM((nb, LANES, ncmp), BF16),
                        pltpu.VMEM((nb, s, 2 * LANES), BF16),
                        pltpu.VMEM((nb, s, LANES), BF16),
                        pltpu.VMEM((nb * NSA_KV, nchunks, VT_ROWS, TQ), BF16),
                        pltpu.VMEM((nb * NSA_KV, nchunks, VT_ROWS, TQ), BF16),
                        pltpu.VMEM((nb * NSA_HEADS, 2 * LANES, TQ), BF16),
                        pltpu.VMEM((nb * NSA_HEADS, NSA_DH, TQ), F32),
                        pltpu.VMEM((2 * nb * NSA_HEADS, 1, TQ), F32),
                        pltpu.VMEM((2 * nb * NSA_HEADS, VT_ROWS, TQ), F32)],
        compiler_params=pltpu.CompilerParams(dimension_semantics=("parallel", "arbitrary"),
                                             vmem_limit_bytes=VMEM_LIMIT),
        name="nsa",
    )(nq, small, kct, vct, ks, vs, kw, vw, *kparams, *vparams)


def _source_columns():
    o = 0
    qkv = list(range(o, o + DN_QKV)); o += DN_QKV
    z = list(range(o, o + DN_HEADS * DN_DV)); o += DN_HEADS * DN_DV
    b_raw = list(range(o, o + DN_HEADS)); o += DN_HEADS
    a_raw = list(range(o, o + DN_HEADS)); o += DN_HEADS
    cf_u = list(range(o, o + 2 * CF_CH)); o += 2 * CF_CH
    n_q = list(range(o, o + NSA_HEADS * NSA_DH)); o += NSA_HEADS * NSA_DH
    kv = []
    for _ in range(6):
        kv.append(list(range(o, o + NSA_KV * NSA_DH))); o += NSA_KV * NSA_DH
    gate = list(range(o, o + 3 * NSA_HEADS)); o += 3 * NSA_HEADS
    n_kc, n_vc, n_ks, n_vs, n_kw, n_vw = kv
    small = b_raw + a_raw + gate
    small = small + [-1] * (LANES - len(small))
    nq = []
    zero = [-1] * NSA_DH
    for idx in range(NSA_HEADS):
        qh = n_q[idx * NSA_DH:(idx + 1) * NSA_DH]
        nq += (qh + zero) if idx // NSA_HPG == 0 else (zero + qh)
    cols = qkv + z + small + cf_u + nq + n_ks + n_vs + n_kw + n_vw + n_kc + n_vc
    assert len(cols) == sum(SEC_WIDTHS)
    return np.asarray(cols, np.int32), o


def _permute_w_in(w):
    cols, n_in = _source_columns()
    assert w.shape[1] == n_in
    select = (lax.broadcasted_iota(jnp.int32, (n_in, cols.shape[0]), 0) == cols[None, :]).astype(BF16)
    return jnp.dot(w.astype(BF16), select, preferred_element_type=BF16)


def _compress_params(pos, w1, w2):
    eye = jnp.eye(NSA_KV, dtype=F32)
    half = CMP_LEN // 2
    w1r = w1.reshape(CMP_LEN, NSA_DH, CMP_HID)

    def expand_w1(part):
        return jnp.einsum("ldj,gh->lgdhj", part, eye).reshape(half * NSA_KV * NSA_DH, NSA_KV * CMP_HID).astype(BF16)

    def expand_pos(part):
        return jnp.broadcast_to(part[:, None, :], (half, NSA_KV, NSA_DH)).reshape(1, half * NSA_KV * NSA_DH)

    w2e = jnp.einsum("jd,gh->gjhd", w2, eye).reshape(NSA_KV * CMP_HID, NSA_KV * NSA_DH).astype(BF16)
    return [expand_pos(pos[:half]), expand_pos(pos[half:]), expand_w1(w1r[:half]), expand_w1(w1r[half:]), w2e]


def _lane_row(vals, first_lane):
    pad = jnp.zeros((LANES - first_lane - vals.shape[0],), F32)
    return jnp.concatenate([jnp.zeros((first_lane,), F32), vals.astype(F32), pad])[None, :]


def kernel(x, norm_w, w_in, dn_conv_w, dn_a_log, dn_dt_bias, dn_norm_w, cf_dw_w, cf_dw_b, cf_ln_w, cf_ln_b,
           nsa_k_pos, nsa_v_pos, nsa_k_w1, nsa_k_w2, nsa_v_w1, nsa_v_w2, w_out, ffn_w_gate, ffn_w_up, ffn_w_down):
    b, s, d = x.shape
    depth = w_in.shape[0]
    assert s % SUPER == 0 and s % TQ == 0 and s // CMP_STRIDE == LANES and s // SEL_BLOCK <= LANES
    assert WINDOW == 2 * TQ and TQ % SEL_BLOCK == 0
    m = b * s
    tm = 512 if m % 512 == 0 else 256
    x2 = x.reshape(m, d)
    ndn = DN_HEADS * DN_DV
    rows16 = s // CMP_STRIDE
    for l in range(depth):
        secs = _in_proj(x2, norm_w[l, 0:1], _permute_w_in(w_in[l]), cf_dw_w[l], cf_dw_b[l][None, :],
                        cf_ln_w[l][None, :], cf_ln_b[l][None, :], s, tm)
        qkv, z, small, o_cf, nq, n_ks, n_vs, n_kw, n_vw = [t.reshape(b, s, t.shape[1]) for t in secs[:N_ROWMAJOR]]
        n_kct, n_vct = [t.reshape(b, rows16, t.shape[1]) for t in secs[N_ROWMAJOR:]]
        o_dn = _deltanet(qkv, z, small, dn_conv_w[l], _lane_row(dn_a_log[l], LANE_A), _lane_row(dn_dt_bias[l], LANE_A),
                         jnp.tile(dn_norm_w[l], DN_HEADS)[None, :])
        o_nsa = _nsa(nq, small, n_kct, n_vct, n_ks, n_vs, n_kw, n_vw,
                     _compress_params(nsa_k_pos[l], nsa_k_w1[l], nsa_k_w2[l]),
                     _compress_params(nsa_v_pos[l], nsa_v_w1[l], nsa_v_w2[l]))
        wo = w_out[l].astype(BF16)
        x2 = _mix_ffn(x2, o_dn.reshape(m, ndn), o_cf.reshape(m, CF_CH), o_nsa.reshape(m, NSA_HEADS * NSA_DH),
                      wo[:ndn], wo[ndn:ndn + CF_CH], wo[ndn + CF_CH:], norm_w[l],
                      ffn_w_gate[l].astype(BF16), ffn_w_up[l].astype(BF16), ffn_w_down[l].astype(BF16), tm)
    return x2.reshape(b, s, d)
```

```python
import jax
import jax.numpy as jnp
import numpy as np
from jax import lax
from jax.experimental import pallas as pl
from jax.experimental.pallas import tpu as pltpu

F32 = jnp.float32
BF16 = jnp.bfloat16

DN_HEADS = 6
DN_DK = 64
DN_DV = 64
DN_CONV = 4
DN_CHUNK = 64
CF_CH = 256
CF_KERNEL = 31
NSA_HEADS = 6
NSA_KV = 2
NSA_HPG = NSA_HEADS // NSA_KV
NSA_DH = 64
CMP_LEN = 32
CMP_STRIDE = 16
CMP_HID = 2 * NSA_DH
SEL_BLOCK = 64
N_SEL = 8
WINDOW = 512
EPS = 1e-6
NEG = -1e30
SEL_BIG = 1e4

LANES = 128
SUBLANES = 8
BF16_ROWS = 16
DN_QK = DN_HEADS * DN_DK
DN_QKV = 2 * DN_QK + DN_HEADS * DN_DV
SUPER = 256
DN_BATCH = 2
NSA_BATCH = 2
TQ = 256
VT_ROWS = NSA_DH + BF16_ROWS
VMEM_LIMIT = 56 * 1024 * 1024

SEC_WIDTHS = (DN_QKV, DN_HEADS * DN_DV, LANES, 2 * CF_CH, NSA_HEADS * NSA_DH,
              LANES, LANES, LANES, LANES, LANES, LANES)
N_ROWMAJOR = 9
SEC_CF = 3
SEC_BF16 = (5, 6, 7, 8)
CF_ROWS = 64
CF_PAD = 32
LANE_BETA = 0
LANE_A = DN_HEADS
LANE_GATE = 2 * DN_HEADS


def _dot(a, b, precision=None):
    return jnp.dot(a, b, preferred_element_type=F32, precision=precision)


def _split_terms(x, terms):
    out = []
    for _ in range(terms):
        hi = x.astype(BF16)
        out.append(hi)
        x = x - hi.astype(F32)
    return out


def _dot_x01(x, mat01, terms):
    return sum(_dot(p, mat01) for p in _split_terms(x, terms))


def _dot_01x(mat01, x, terms):
    return sum(_dot(mat01, p) for p in _split_terms(x, terms))


def _sigmoid(x):
    return 1.0 / (1.0 + jnp.exp(-x))


def _silu(x):
    return x * _sigmoid(x)


def _rms(x, w):
    return x * lax.rsqrt(jnp.mean(x * x, axis=-1, keepdims=True) + EPS) * w


IN_PROJ_COLS = 1280


def _merged_sections():
    runs, first, width = [], 0, 0
    for k, wd in enumerate(SEC_WIDTHS):
        if width and width + wd > IN_PROJ_COLS:
            runs.append((first, k))
            first, width = k, 0
        width += wd
    runs.append((first, len(SEC_WIDTHS)))
    return runs


def _conformer_rows(r, dw_ref, db_ref, lnw_ref, lnb_ref, out_ref, hbuf):
    first = CF_PAD - (CF_KERNEL - 1)
    acc = jnp.zeros((CF_ROWS, CF_CH), F32) + db_ref[...]
    for phase in range(SUBLANES):
        part = None
        for j in range(CF_KERNEL):
            if (first + j) % SUBLANES != phase:
                continue
            base = r * CF_ROWS + first + j - phase
            rows = CF_ROWS + (SUBLANES if phase else 0)
            term = dw_ref[j:j + 1, :] * hbuf[pl.ds(base, rows), :]
            part = term if part is None else part + term
        if part is not None:
            acc = acc + part[phase:phase + CF_ROWS, :]
    mu = jnp.mean(acc, axis=-1, keepdims=True)
    cen = acc - mu
    var = jnp.mean(cen * cen, axis=-1, keepdims=True)
    y = cen * lax.rsqrt(var + EPS) * lnw_ref[...] + lnb_ref[...]
    out_ref[r * CF_ROWS:(r + 1) * CF_ROWS, :] = _silu(y).astype(out_ref.dtype)


def _in_proj_kernel(x_ref, nw_ref, w_ref, dw_ref, db_ref, lnw_ref, lnb_ref, *refs):
    out_refs, stage, hbuf = refs[:-2], refs[-2], refs[-1]
    tm = x_ref.shape[0]

    @pl.when(pl.program_id(1) == 0)
    def _():
        hbuf[0:CF_PAD, :] = jnp.zeros((CF_PAD, CF_CH), F32)

    h = _rms(x_ref[...], nw_ref[...]).astype(BF16)
    rows16 = tm // CMP_STRIDE

    def emit(first, last):
        off = sum(SEC_WIDTHS[:first])
        wide = _dot(h, w_ref[:, off:off + sum(SEC_WIDTHS[first:last])])
        col = 0
        for k in range(first, last):
            y = wide[:, col:col + SEC_WIDTHS[k]]
            col += SEC_WIDTHS[k]
            if k == SEC_CF:
                hbuf[CF_PAD:CF_PAD + tm, :] = y[:, :CF_CH] * _sigmoid(y[:, CF_CH:])
            elif k < N_ROWMAJOR:
                out_refs[k][...] = y.astype(out_refs[k].dtype)
            else:
                stage[...] = y
                for t in range(CMP_STRIDE):
                    out_refs[k][:, t * LANES:(t + 1) * LANES] = stage[pl.ds(t, rows16, stride=CMP_STRIDE), :]

    runs = sorted(_merged_sections(), key=lambda r: not (r[0] <= SEC_CF < r[1]))
    emit(*runs[0])
    nblocks = tm // CF_ROWS
    done = 0
    for n, run in enumerate(runs[1:], start=1):
        upto = nblocks * n // (len(runs) - 1)
        for r in range(done, upto):
            _conformer_rows(r, dw_ref, db_ref, lnw_ref, lnb_ref, out_refs[SEC_CF], hbuf)
        done = upto
        emit(*run)
    hbuf[0:CF_PAD, :] = hbuf[tm:tm + CF_PAD, :]


def _in_proj(x2, nw, w_perm, dw, db, lnw, lnb, seq, tm):
    m, d = x2.shape
    n = w_perm.shape[1]
    nt = seq // tm
    shapes, blocks, dtypes = [], [], []
    for k, wd in enumerate(SEC_WIDTHS):
        if k == SEC_CF:
            shapes.append((m, CF_CH)); blocks.append((tm, CF_CH)); dtypes.append(BF16)
        elif k < N_ROWMAJOR:
            shapes.append((m, wd)); blocks.append((tm, wd)); dtypes.append(BF16 if k in SEC_BF16 else F32)
        else:
            shapes.append((m // CMP_STRIDE, CMP_STRIDE * wd)); blocks.append((tm // CMP_STRIDE, CMP_STRIDE * wd))
            dtypes.append(F32)
    row = lambda bi, t: (bi * nt + t, 0)
    const = lambda bi, t: (0, 0)
    return pl.pallas_call(
        _in_proj_kernel,
        grid=(m // seq, nt),
        in_specs=[pl.BlockSpec((tm, d), row),
                  pl.BlockSpec((1, d), const),
                  pl.BlockSpec((d, n), const),
                  pl.BlockSpec((CF_KERNEL, CF_CH), const),
                  pl.BlockSpec((1, CF_CH), const),
                  pl.BlockSpec((1, CF_CH), const),
                  pl.BlockSpec((1, CF_CH), const)],
        out_specs=[pl.BlockSpec(blk, row) for blk in blocks],
        out_shape=[jax.ShapeDtypeStruct(shp, dt) for shp, dt in zip(shapes, dtypes)],
        scratch_shapes=[pltpu.VMEM((tm, LANES), F32),
                        pltpu.VMEM((tm + CF_PAD, CF_CH), F32)],
        compiler_params=pltpu.CompilerParams(dimension_semantics=("parallel", "arbitrary"),
                                             vmem_limit_bytes=VMEM_LIMIT),
        name="in_proj",
    )(x2, nw, w_perm, dw, db, lnw, lnb)


def _mix_ffn_kernel(x_ref, odn_ref, ocf_ref, onsa_ref, wo_ref, nw_ref, wg_ref, wu_ref, wd_ref, out_ref):
    mix = _dot(jnp.concatenate([odn_ref[...], ocf_ref[...], onsa_ref[...]], axis=1), wo_ref[...])
    x1 = x_ref[...] + _rms(mix, nw_ref[1:2, :])
    h = _rms(x1, nw_ref[2:3, :]).astype(BF16)
    g = _dot(h, wg_ref[...])
    u = _dot(h, wu_ref[...])
    a = (_silu(g) * u).astype(BF16)
    f = _dot(a, wd_ref[...])
    out_ref[...] = x1 + _rms(f, nw_ref[3:4, :])


def _mix_ffn(x2, odn, ocf, onsa, wo, nw, wg, wu, wd, tm):
    m, d = x2.shape
    dff = wg.shape[1]
    const = lambda i: (0, 0)
    row = lambda i: (i, 0)
    return pl.pallas_call(
        _mix_ffn_kernel,
        grid=(m // tm,),
        in_specs=[pl.BlockSpec((tm, d), row),
                  pl.BlockSpec((tm, odn.shape[1]), row),
                  pl.BlockSpec((tm, ocf.shape[1]), row),
                  pl.BlockSpec((tm, onsa.shape[1]), row),
                  pl.BlockSpec(wo.shape, const),
                  pl.BlockSpec(nw.shape, const),
                  pl.BlockSpec((d, dff), const),
                  pl.BlockSpec((d, dff), const),
                  pl.BlockSpec((dff, d), const)],
        out_specs=pl.BlockSpec((tm, d), row),
        out_shape=jax.ShapeDtypeStruct((m, d), F32),
        compiler_params=pltpu.CompilerParams(dimension_semantics=("parallel",), vmem_limit_bytes=VMEM_LIMIT),
        name="mix_ffn",
    )(x2, odn, ocf, onsa, wo, nw, wg, wu, wd)


DN_PAD = 8


def _deltanet_kernel(qkv_ref, z_ref, sm_ref, cw_ref, alog_ref, dtb_ref, nw_ref, out_ref, xbuf, ybuf, state):
    n = SUPER
    nchunk = n // DN_CHUNK
    nb = qkv_ref.shape[0]

    @pl.when(pl.program_id(1) == 0)
    def _():
        xbuf[:, 0:DN_PAD, :] = jnp.zeros((nb, DN_PAD, DN_QKV), F32)
        state[...] = jnp.zeros(state.shape, F32)

    hr = lax.broadcasted_iota(jnp.int32, (DN_QK, DN_QK), 0) // DN_DK
    hc = lax.broadcasted_iota(jnp.int32, (DN_QK, DN_QK), 1) // DN_DK
    head_ones = (hr == hc).astype(BF16)
    ri = lax.broadcasted_iota(jnp.int32, (n, n), 0)
    ci = lax.broadcasted_iota(jnp.int32, (n, n), 1)
    same_chunk = (ri // DN_CHUNK) == (ci // DN_CHUNK)
    causal = same_chunk & (ri >= ci)
    strict = same_chunk & (ri > ci)
    causal_bf = causal.astype(BF16)
    same_chunk_bf = same_chunk.astype(BF16)
    nfac = (DN_CHUNK - 1).bit_length() - 1
    lane_chunk = lax.broadcasted_iota(jnp.int32, (1, n), 1) // DN_CHUNK
    in_chunk = [lane_chunk == c for c in range(nchunk)]
    in_chunk_bf = [m.astype(BF16) for m in in_chunk]

    def to_rowform(bd):
        return sum(jnp.where(in_chunk[c], bd[c * DN_CHUNK:(c + 1) * DN_CHUNK, :], 0.0) for c in range(nchunk))

    def to_blockdiag(rowform):
        return jnp.concatenate([rowform * in_chunk_bf[c] for c in range(nchunk)], axis=0)

    first = DN_PAD - (DN_CONV - 1)
    rows = []
    for bi in range(nb):
        xbuf[bi, DN_PAD:DN_PAD + n, :] = qkv_ref[bi]
        y = cw_ref[DN_CONV - 1:DN_CONV, :] * xbuf[bi, pl.ds(DN_PAD, n), :]
        for j in range(DN_CONV - 1):
            y = y + cw_ref[j:j + 1, :] * xbuf[bi, pl.ds(first + j, n), :]
        xbuf[bi, 0:DN_PAD, :] = xbuf[bi, n:n + DN_PAD, :]
        ybuf[bi] = _silu(y)
        q_all = ybuf[bi, :, 0:DN_QK]
        k_all = ybuf[bi, :, DN_QK:2 * DN_QK]
        v_all = ybuf[bi, :, 2 * DN_QK:]
        q_n = q_all * lax.rsqrt(_dot_x01(q_all * q_all, head_ones, 1) + EPS) * (DN_DK ** -0.5)
        k_n = k_all * lax.rsqrt(_dot_x01(k_all * k_all, head_ones, 1) + EPS)
        sm = sm_ref[bi]
        beta_all = _sigmoid(sm)
        sp_in = sm + dtb_ref[...]
        softplus = jnp.maximum(sp_in, 0.0) + jnp.log1p(jnp.exp(-jnp.abs(sp_in)))
        g_all = -jnp.exp(alog_ref[...]) * softplus
        gam = _dot_01x(causal_bf, g_all, 3)
        glast = _dot_01x(same_chunk_bf, g_all, 3)
        rows.append(dict(q_n=q_n, k_n=k_n, v_all=v_all, beta_all=beta_all, gam=gam, gam_t=gam.T,
                         e_gam=jnp.exp(gam), e_tot=jnp.exp(glast), e_rest_t=jnp.exp(glast - gam).T,
                         k_t=k_n.T))

    heads = [(bi, h) for bi in range(nb) for h in range(DN_HEADS)]
    q, k, v, bcol, egcol, etcol, lmat, qkd = {}, {}, {}, {}, {}, {}, {}, {}
    for key in heads:
        bi, h = key
        r = rows[bi]
        sl = slice(h * DN_DK, (h + 1) * DN_DK)
        lg = LANE_A + h
        q[key], k[key], v[key] = r["q_n"][:, sl], r["k_n"][:, sl], r["v_all"][:, sl]
        bcol[key] = r["beta_all"][:, LANE_BETA + h:LANE_BETA + h + 1]
        egcol[key] = r["e_gam"][:, lg:lg + 1]
        etcol[key] = r["e_tot"][:, lg:lg + 1]
        kt = r["k_t"][h * DN_DK:(h + 1) * DN_DK, :].astype(BF16)
        dec = jnp.exp(jnp.where(causal, r["gam"][:, lg:lg + 1] - r["gam_t"][lg:lg + 1, :], NEG))
        lmat[key] = jnp.where(strict, _dot((k[key] * bcol[key]).astype(BF16), kt) * dec, 0.0)
        qkd[key] = (_dot(q[key].astype(BF16), kt) * dec).astype(BF16)
    ps = {key: -to_rowform(lmat[key]) for key in heads}
    mp = {key: _dot((-ps[key]).astype(BF16), lmat[key].astype(BF16)) for key in heads}
    for s in range(nfac):
        mb = {key: mp[key].astype(BF16) for key in heads}
        mbd = {key: to_blockdiag(mb[key]) for key in heads}
        prod = {key: _dot(ps[key].astype(BF16), mbd[key]) for key in heads}
        ps = {key: ps[key] + mp[key] + prod[key] for key in heads}
        if s < nfac - 1:
            mp = {key: _dot(mb[key], mbd[key]) for key in heads}
    qeff, o_intra, b_in, n_in = {}, {}, {}, {}
    rhs = {key: jnp.concatenate([v[key] * bcol[key], k[key] * (bcol[key] * egcol[key])], axis=1) for key in heads}
    tx = {key: _dot(to_blockdiag(ps[key].astype(BF16)), rhs[key].astype(BF16)) for key in heads}
    x = {key: (rhs[key] + tx[key]).astype(BF16) for key in heads}
    qx = {key: _dot(qkd[key], x[key]) for key in heads}
    kdt = {(bi, h): (rows[bi]["k_t"][h * DN_DK:(h + 1) * DN_DK, :]
                     * rows[bi]["e_rest_t"][LANE_A + h:LANE_A + h + 1, :]).astype(BF16) for bi, h in heads}
    kx = {key: _dot(to_blockdiag(kdt[key]), x[key]) for key in heads}
    for key in heads:
        qeff[key] = (q[key] * egcol[key] - qx[key][:, DN_DV:]).astype(BF16)
        o_intra[key] = qx[key][:, :DN_DV]
        b_in[key], n_in[key] = kx[key][:, :DN_DV], kx[key][:, DN_DV:].astype(BF16)
    st = {(bi, h): state[bi * DN_HEADS + h] for bi, h in heads}
    o_chunks = {key: [] for key in heads}
    for c in range(nchunk):
        rs = slice(c * DN_CHUNK, (c + 1) * DN_CHUNK)
        ks = slice(c * DN_DK, (c + 1) * DN_DK)
        stb = {key: st[key].astype(BF16) for key in heads}
        corr = {key: _dot(n_in[key][ks], stb[key]) for key in heads}
        for key in heads:
            o_chunks[key].append(_dot(qeff[key][rs], stb[key]) + o_intra[key][rs])
            st[key] = st[key] * etcol[key][rs] + b_in[key][ks] - corr[key]
    for bi in range(nb):
        for h in range(DN_HEADS):
            state[bi * DN_HEADS + h] = st[(bi, h)]
        o = jnp.concatenate([jnp.concatenate(o_chunks[(bi, h)], axis=0) for h in range(DN_HEADS)], axis=1)
        ms = _dot_x01(o * o, head_ones, 1) * (1.0 / DN_DV)
        o = o * lax.rsqrt(ms + EPS) * nw_ref[...]
        out_ref[bi] = (o * _silu(z_ref[bi])).astype(out_ref.dtype)


def _deltanet(qkv, z, small, cw, alog_row, dtb_row, nw_row):
    b, s, _ = qkv.shape
    nb = DN_BATCH if b % DN_BATCH == 0 else 1
    const = lambda i, t: (0, 0)
    tile = lambda i, t: (i, t, 0)
    return pl.pallas_call(
        _deltanet_kernel,
        grid=(b // nb, s // SUPER),
        in_specs=[pl.BlockSpec((nb, SUPER, DN_QKV), tile),
                  pl.BlockSpec((nb, SUPER, DN_QK), tile),
                  pl.BlockSpec((nb, SUPER, LANES), tile),
                  pl.BlockSpec((DN_CONV, DN_QKV), const),
                  pl.BlockSpec((1, LANES), const),
                  pl.BlockSpec((1, LANES), const),
                  pl.BlockSpec((1, DN_QK), const)],
        out_specs=pl.BlockSpec((nb, SUPER, DN_QK), tile),
        out_shape=jax.ShapeDtypeStruct((b, s, DN_QK), BF16),
        scratch_shapes=[pltpu.VMEM((nb, SUPER + DN_PAD, DN_QKV), F32),
                        pltpu.VMEM((nb, SUPER, DN_QKV), F32),
                        pltpu.VMEM((nb * DN_HEADS, DN_DK, DN_DV), F32)],
        compiler_params=pltpu.CompilerParams(dimension_semantics=("parallel", "arbitrary"),
                                             vmem_limit_bytes=VMEM_LIMIT),
        name="deltanet",
    )(qkv, z, small, cw, alog_row, dtb_row, nw_row)


def _compress(t_ref, row, posa_ref, posb_ref, w1a_ref, w1b_ref, w2_ref):
    t = t_ref[row]
    nrow = t.shape[0]
    p = _dot((t + posa_ref[...]).astype(BF16), w1a_ref[...])
    q = _dot((t + posb_ref[...]).astype(BF16), w1b_ref[...])
    hid = _silu(p + pltpu.roll(q, nrow - 1, 0))
    return _dot(hid.astype(BF16), w2_ref[...])


def _store_vt(dst, row, v_ref):
    vt = v_ref[row].astype(F32).T
    ones = jnp.ones((BF16_ROWS, TQ), BF16)
    for g in range(NSA_KV):
        for c in range(dst.shape[1]):
            blk = vt[g * NSA_DH:(g + 1) * NSA_DH, c * TQ:(c + 1) * TQ].astype(BF16)
            dst[row * NSA_KV + g, c] = jnp.concatenate([blk, ones], axis=0)


def _attend_chunks(chunks, items, m_s, acc_s):
    scores = []
    for j, mask in chunks:
        row = []
        for _, q, key_chunk, _ in items:
            s = _dot(key_chunk(j), q)
            if mask is not None:
                s = jnp.where(mask, s, NEG)
            row.append(s.astype(BF16))
        scores.append(row)
    for (j, _), chunk_scores in zip(chunks, scores):
        probs, alphas = [], []
        for (slot, _, _, _), s in zip(items, chunk_scores):
            m_prev = m_s[slot]
            m_new = jnp.maximum(m_prev, jnp.max(s, axis=0, keepdims=True).astype(F32))
            probs.append(jnp.exp(s - m_new.astype(BF16)))
            alphas.append(jnp.exp(m_prev - m_new))
            m_s[slot] = m_new
        pv = [_dot(vt_chunk(j), p) for (_, _, _, vt_chunk), p in zip(items, probs)]
        for (slot, _, _, _), a, o in zip(items, alphas, pv):
            acc_s[slot] = a * acc_s[slot] + o


def _nsa_kernel(q_ref, sm_ref, kct_ref, vct_ref, ks_ref, vs_ref, kw_ref, vw_ref,
                kposa_ref, kposb_ref, kw1a_ref, kw1b_ref, kw2_ref,
                vposa_ref, vposb_ref, vw1a_ref, vw1b_ref, vw2_ref,
                out_ref,
                kc_s, vct_s, ksb, kwb, vst_s, vwt_s, qt_s, oc_s, m_s, acc_s):
    i = pl.program_id(1)
    nb = q_ref.shape[0]
    ncmp = kc_s.shape[1]
    nblk = ks_ref.shape[1] // SEL_BLOCK
    heads = [(bi, idx) for bi in range(nb) for idx in range(NSA_HEADS)]
    slot = lambda bi, idx: bi * NSA_HEADS + idx

    @pl.when(i == 0)
    def _():
        s_len = ks_ref.shape[1]
        key_blk = lax.broadcasted_iota(jnp.int32, (s_len, LANES), 0) // SEL_BLOCK
        blk_onehot = (lax.broadcasted_iota(jnp.int32, (s_len, LANES), 1) == key_blk).astype(BF16)
        for bi in range(nb):
            kc = _compress(kct_ref, bi, kposa_ref, kposb_ref, kw1a_ref, kw1b_ref, kw2_ref)
            kc_hi, kc_lo = _split_terms(kc, 2)
            kc_s[2 * bi] = kc_hi
            kc_s[2 * bi + 1] = kc_lo
            vct_s[bi] = _compress(vct_ref, bi, vposa_ref, vposb_ref, vw1a_ref, vw1b_ref, vw2_ref).T.astype(BF16)
            ksb[bi, :, 0:LANES] = ks_ref[bi]
            ksb[bi, :, LANES:] = blk_onehot
            kwb[bi] = kw_ref[bi]
            for idx in range(NSA_HEADS):
                qt_s[slot(bi, idx), LANES + nblk:, :] = jnp.zeros((LANES - nblk, TQ), BF16)
            _store_vt(vst_s, bi, vs_ref)
            _store_vt(vwt_s, bi, vw_ref)

    trow = i * TQ + lax.broadcasted_iota(jnp.int32, (1, TQ), 1)
    krel = lax.broadcasted_iota(jnp.int32, (TQ, 1), 0)
    cmp_end = lax.broadcasted_iota(jnp.int32, (ncmp, 1), 0) * CMP_STRIDE + (CMP_LEN - 1)
    valid_c = cmp_end <= trow
    oj = lax.broadcasted_iota(jnp.int32, (nblk, ncmp), 0) * SEL_BLOCK
    on = lax.broadcasted_iota(jnp.int32, (nblk, ncmp), 1) * CMP_STRIDE
    ov_t = ((on < oj + SEL_BLOCK) & (on + CMP_LEN > oj)).astype(BF16)
    jj = lax.broadcasted_iota(jnp.int32, (nblk, TQ), 0)
    cur = trow // SEL_BLOCK
    elig = jj <= cur
    forced = (jj == 0) | (jj == cur) | (jj == cur - 1)

    s_c = {}
    for bi in range(nb):
        qt_all = (q_ref[bi] * (NSA_DH ** -0.5)).T
        kc_hi, kc_lo = kc_s[2 * bi], kc_s[2 * bi + 1]
        zero_half = jnp.zeros((NSA_DH, TQ), F32)
        for idx in range(NSA_HEADS):
            qh = qt_all[idx * NSA_DH:(idx + 1) * NSA_DH]
            qt = jnp.concatenate([qh, zero_half] if idx // NSA_HPG == 0 else [zero_half, qh], axis=0)
            qt_hi, qt_lo = _split_terms(qt, 2)
            qt_s[slot(bi, idx), 0:LANES, :] = qt_hi
            s_c[(bi, idx)] = _dot(kc_hi, qt_hi) + _dot(kc_hi, qt_lo) + _dot(kc_lo, qt_hi)
    probs = {}
    for key in heads:
        s = jnp.where(valid_c, s_c[key], NEG)
        e = jnp.where(valid_c, jnp.exp(s - jnp.max(s, axis=0, keepdims=True)), 0.0)
        probs[key] = e * (1.0 / jnp.maximum(jnp.sum(e, axis=0, keepdims=True), 1e-30))
    for bi, idx in heads:
        g = idx // NSA_HPG
        oc_s[slot(bi, idx)] = _dot(vct_s[bi, g * NSA_DH:(g + 1) * NSA_DH, :], probs[(bi, idx)].astype(BF16))
    for bi in range(nb):
        for g in range(NSA_KV):
            psum = sum(probs[(bi, idx)] for idx in range(g * NSA_HPG, (g + 1) * NSA_HPG))
            imp_t = _dot_01x(ov_t, psum, 3)
            score = jnp.where(elig, imp_t + jnp.where(forced, SEL_BIG, 0.0), -SEL_BIG)
            rank = jnp.zeros((nblk, TQ), F32)
            for r in range(nblk):
                row = score[r:r + 1, :]
                beats = (row > score) | ((row == score) & (r < jj))
                rank = rank + beats.astype(F32)
            bias = jnp.where(rank < float(N_SEL), 0.0, NEG).astype(BF16)
            for hh in range(NSA_HPG):
                qt_s[slot(bi, g * NSA_HPG + hh), LANES:LANES + nblk, :] = bias

    m_s[...] = jnp.full(m_s.shape, NEG, BF16).astype(F32)
    acc_s[...] = jnp.zeros(acc_s.shape, F32)
    qrel = lax.broadcasted_iota(jnp.int32, (1, TQ), 1)
    on_or_below = krel <= qrel
    nslots = nb * NSA_HEADS

    sel_items = [(slot(bi, idx), qt_s[slot(bi, idx)],
                  lambda j, bi=bi: ksb[bi, pl.ds(pl.multiple_of(j * TQ, TQ), TQ), :],
                  lambda j, bi=bi, idx=idx: vst_s[bi * NSA_KV + idx // NSA_HPG, j]) for bi, idx in heads]

    def sel_pair(p, carry):
        _attend_chunks([(2 * p, None), (2 * p + 1, None)], sel_items, m_s, acc_s)
        return carry

    lax.fori_loop(0, i // 2, sel_pair, 0)

    @pl.when(i % 2 == 0)
    def _():
        _attend_chunks([(i, on_or_below)], sel_items, m_s, acc_s)

    @pl.when(i % 2 == 1)
    def _():
        _attend_chunks([(i - 1, None), (i, on_or_below)], sel_items, m_s, acc_s)

    win_items = [(nslots + slot(bi, idx), qt_s[slot(bi, idx), 0:LANES, :],
                  lambda j, bi=bi: kwb[bi, pl.ds(pl.multiple_of(j * TQ, TQ), TQ), :],
                  lambda j, bi=bi, idx=idx: vwt_s[bi * NSA_KV + idx // NSA_HPG, j]) for bi, idx in heads]

    @pl.when(i == 0)
    def _():
        _attend_chunks([(i, on_or_below)], win_items, m_s, acc_s)

    @pl.when(i >= 1)
    def _():
        _attend_chunks([(i, on_or_below), (i - 1, None)], win_items, m_s, acc_s)

    @pl.when(i >= 2)
    def _():
        _attend_chunks([(i - 2, krel > qrel)], win_items, m_s, acc_s)

    for bi in range(nb):
        gates_t = _sigmoid(sm_ref[bi]).T
        outs = []
        for idx in range(NSA_HEADS):
            acc = acc_s[slot(bi, idx)]
            o_s = acc[:NSA_DH] * (1.0 / acc[NSA_DH:NSA_DH + 1])
            acc = acc_s[nslots + slot(bi, idx)]
            o_w = acc[:NSA_DH] * (1.0 / acc[NSA_DH:NSA_DH + 1])
            lg = LANE_GATE + idx
            outs.append(gates_t[lg:lg + 1] * oc_s[slot(bi, idx)] + gates_t[lg + NSA_HEADS:lg + NSA_HEADS + 1] * o_s
                        + gates_t[lg + 2 * NSA_HEADS:lg + 2 * NSA_HEADS + 1] * o_w)
        out_ref[bi] = jnp.concatenate(outs, axis=0).T.astype(out_ref.dtype)


def _nsa(nq, small, kct, vct, ks, vs, kw, vw, kparams, vparams):
    b, s, _ = nq.shape
    nb = NSA_BATCH if b % NSA_BATCH == 0 else 1
    ncmp = kct.shape[1]
    nchunks = s // TQ
    tile = lambda bi, i: (bi, i, 0)
    full = lambda bi, i: (bi, 0, 0)
    const = lambda bi, i: (0, 0)
    wspecs = [pl.BlockSpec(p.shape, const, pipeline_mode=pl.Buffered(1)) for p in kparams + vparams]
    return pl.pallas_call(
        _nsa_kernel,
        grid=(b // nb, nchunks),
        in_specs=[pl.BlockSpec((nb, TQ, NSA_HEADS * NSA_DH), tile),
                  pl.BlockSpec((nb, TQ, LANES), tile),
                  pl.BlockSpec((nb, ncmp, kct.shape[2]), full),
                  pl.BlockSpec((nb, ncmp, vct.shape[2]), full),
                  pl.BlockSpec((nb, s, LANES), full),
                  pl.BlockSpec((nb, s, LANES), full),
                  pl.BlockSpec((nb, s, LANES), full),
                  pl.BlockSpec((nb, s, LANES), full)] + wspecs,
        out_specs=pl.BlockSpec((nb, TQ, NSA_HEADS * NSA_DH), tile),
        out_shape=jax.ShapeDtypeStruct((b, s, NSA_HEADS * NSA_DH), BF16),
        scratch_shapes=[pltpu.VMEM((2 * nb, ncmp, LANES), BF16),
                        pltpu.VMEM((nb, LANES, ncmp), BF16),
                        pltpu.VMEM((nb, s, 2 * LANES), BF16),
                        pltpu.VMEM((nb, s, LANES), BF16),
                        pltpu.VMEM((nb * NSA_KV, nchunks, VT_ROWS, TQ), BF16),
                        pltpu.VMEM((nb * NSA_KV, nchunks, VT_ROWS, TQ), BF16),
                        pltpu.VMEM((nb * NSA_HEADS, 2 * LANES, TQ), BF16),
                        pltpu.VMEM((nb * NSA_HEADS, NSA_DH, TQ), F32),
                        pltpu.VMEM((2 * nb * NSA_HEADS, 1, TQ), F32),
                        pltpu.VMEM((2 * nb * NSA_HEADS, VT_ROWS, TQ), F32)],
        compiler_params=pltpu.CompilerParams(dimension_semantics=("parallel", "arbitrary"),
                                             vmem_limit_bytes=VMEM_LIMIT),
        name="nsa",
    )(nq, small, kct, vct, ks, vs, kw, vw, *kparams, *vparams)


def _source_columns():
    o = 0
    qkv = list(range(o, o + DN_QKV)); o += DN_QKV
    z = list(range(o, o + DN_HEADS * DN_DV)); o += DN_HEADS * DN_DV
    b_raw = list(range(o, o + DN_HEADS)); o += DN_HEADS
    a_raw = list(range(o, o + DN_HEADS)); o += DN_HEADS
    cf_u = list(range(o, o + 2 * CF_CH)); o += 2 * CF_CH
    n_q = list(range(o, o + NSA_HEADS * NSA_DH)); o += NSA_HEADS * NSA_DH
    kv = []
    for _ in range(6):
        kv.append(list(range(o, o + NSA_KV * NSA_DH))); o += NSA_KV * NSA_DH
    gate = list(range(o, o + 3 * NSA_HEADS)); o += 3 * NSA_HEADS
    n_kc, n_vc, n_ks, n_vs, n_kw, n_vw = kv
    small = b_raw + a_raw + gate
    small = small + [-1] * (LANES - len(small))
    cols = qkv + z + small + cf_u + n_q + n_ks + n_vs + n_kw + n_vw + n_kc + n_vc
    assert len(cols) == sum(SEC_WIDTHS)
    return np.asarray(cols, np.int32), o


def _permute_w_in(w):
    cols, n_in = _source_columns()
    assert w.shape[1] == n_in
    select = (lax.broadcasted_iota(jnp.int32, (n_in, cols.shape[0]), 0) == cols[None, :]).astype(BF16)
    return jnp.dot(w.astype(BF16), select, preferred_element_type=BF16)


def _compress_params(pos, w1, w2):
    eye = jnp.eye(NSA_KV, dtype=F32)
    half = CMP_LEN // 2
    w1r = w1.reshape(CMP_LEN, NSA_DH, CMP_HID)

    def expand_w1(part):
        return jnp.einsum("ldj,gh->lgdhj", part, eye).reshape(half * NSA_KV * NSA_DH, NSA_KV * CMP_HID).astype(BF16)

    def expand_pos(part):
        return jnp.broadcast_to(part[:, None, :], (half, NSA_KV, NSA_DH)).reshape(1, half * NSA_KV * NSA_DH)

    w2e = jnp.einsum("jd,gh->gjhd", w2, eye).reshape(NSA_KV * CMP_HID, NSA_KV * NSA_DH).astype(BF16)
    return [expand_pos(pos[:half]), expand_pos(pos[half:]), expand_w1(w1r[:half]), expand_w1(w1r[half:]), w2e]


def _lane_row(vals, first_lane):
    pad = jnp.zeros((LANES - first_lane - vals.shape[0],), F32)
    return jnp.concatenate([jnp.zeros((first_lane,), F32), vals.astype(F32), pad])[None, :]


def kernel(x, norm_w, w_in, dn_conv_w, dn_a_log, dn_dt_bias, dn_norm_w, cf_dw_w, cf_dw_b, cf_ln_w, cf_ln_b,
           nsa_k_pos, nsa_v_pos, nsa_k_w1, nsa_k_w2, nsa_v_w1, nsa_v_w2, w_out, ffn_w_gate, ffn_w_up, ffn_w_down):
    b, s, d = x.shape
    depth = w_in.shape[0]
    assert s % SUPER == 0 and s % TQ == 0 and s // CMP_STRIDE == LANES and s // SEL_BLOCK <= LANES
    assert WINDOW == 2 * TQ and TQ % SEL_BLOCK == 0
    m = b * s
    tm = 512 if m % 512 == 0 else 256
    x2 = x.reshape(m, d)
    ndn = DN_HEADS * DN_DV
    rows16 = s // CMP_STRIDE
    for l in range(depth):
        secs = _in_proj(x2, norm_w[l, 0:1], _permute_w_in(w_in[l]), cf_dw_w[l], cf_dw_b[l][None, :],
                        cf_ln_w[l][None, :], cf_ln_b[l][None, :], s, tm)
        qkv, z, small, o_cf, nq, n_ks, n_vs, n_kw, n_vw = [t.reshape(b, s, t.shape[1]) for t in secs[:N_ROWMAJOR]]
        n_kct, n_vct = [t.reshape(b, rows16, t.shape[1]) for t in secs[N_ROWMAJOR:]]
        o_dn = _deltanet(qkv, z, small, dn_conv_w[l], _lane_row(dn_a_log[l], LANE_A), _lane_row(dn_dt_bias[l], LANE_A),
                         jnp.tile(dn_norm_w[l], DN_HEADS)[None, :])
        o_nsa = _nsa(nq, small, n_kct, n_vct, n_ks, n_vs, n_kw, n_vw,
                     _compress_params(nsa_k_pos[l], nsa_k_w1[l], nsa_k_w2[l]),
                     _compress_params(nsa_v_pos[l], nsa_v_w1[l], nsa_v_w2[l]))
        x2 = _mix_ffn(x2, o_dn.reshape(m, ndn), o_cf.reshape(m, CF_CH), o_nsa.reshape(m, NSA_HEADS * NSA_DH),
                      w_out[l].astype(BF16), norm_w[l],
                      ffn_w_gate[l].astype(BF16), ffn_w_up[l].astype(BF16), ffn_w_down[l].astype(BF16), tm)
    return x2.reshape(b, s, d)
```

```python
import jax
import jax.numpy as jnp
import numpy as np
from jax import lax
from jax.experimental import pallas as pl
from jax.experimental.pallas import tpu as pltpu

F32 = jnp.float32
BF16 = jnp.bfloat16

DN_HEADS = 6
DN_DK = 64
DN_DV = 64
DN_CONV = 4
DN_CHUNK = 64
CF_CH = 256
CF_KERNEL = 31
NSA_HEADS = 6
NSA_KV = 2
NSA_HPG = NSA_HEADS // NSA_KV
NSA_DH = 64
CMP_LEN = 32
CMP_STRIDE = 16
CMP_HID = 2 * NSA_DH
SEL_BLOCK = 64
N_SEL = 8
WINDOW = 512
EPS = 1e-6
NEG = -1e30
SEL_BIG = 1e4

LANES = 128
SUBLANES = 8
BF16_ROWS = 16
DN_QK = DN_HEADS * DN_DK
DN_QKV = 2 * DN_QK + DN_HEADS * DN_DV
SUPER = 256
DN_BATCH = 2
NSA_BATCH = 2
TQ = 256
VT_ROWS = NSA_DH + BF16_ROWS
VMEM_LIMIT = 56 * 1024 * 1024

SEC_WIDTHS = (DN_QKV, DN_HEADS * DN_DV, LANES, 2 * CF_CH, NSA_HEADS * NSA_DH,
              LANES, LANES, LANES, LANES, LANES, LANES)
N_ROWMAJOR = 9
SEC_CF = 3
SEC_BF16 = (5, 6, 7, 8)
CF_ROWS = 64
CF_PAD = 32
LANE_BETA = 0
LANE_A = DN_HEADS
LANE_GATE = 2 * DN_HEADS


def _dot(a, b, precision=None):
    return jnp.dot(a, b, preferred_element_type=F32, precision=precision)


def _split_terms(x, terms):
    out = []
    for _ in range(terms):
        hi = x.astype(BF16)
        out.append(hi)
        x = x - hi.astype(F32)
    return out


def _dot_x01(x, mat01, terms):
    return sum(_dot(p, mat01) for p in _split_terms(x, terms))


def _dot_01x(mat01, x, terms):
    return sum(_dot(mat01, p) for p in _split_terms(x, terms))


def _sigmoid(x):
    return 1.0 / (1.0 + jnp.exp(-x))


def _silu(x):
    return x * _sigmoid(x)


def _rms(x, w):
    return x * lax.rsqrt(jnp.mean(x * x, axis=-1, keepdims=True) + EPS) * w


IN_PROJ_COLS = 1280


def _merged_sections():
    runs, first, width = [], 0, 0
    for k, wd in enumerate(SEC_WIDTHS):
        if width and width + wd > IN_PROJ_COLS:
            runs.append((first, k))
            first, width = k, 0
        width += wd
    runs.append((first, len(SEC_WIDTHS)))
    return runs


def _conformer_rows(r, dw_ref, db_ref, lnw_ref, lnb_ref, out_ref, hbuf):
    first = CF_PAD - (CF_KERNEL - 1)
    acc = jnp.zeros((CF_ROWS, CF_CH), F32) + db_ref[...]
    for phase in range(SUBLANES):
        part = None
        for j in range(CF_KERNEL):
            if (first + j) % SUBLANES != phase:
                continue
            base = r * CF_ROWS + first + j - phase
            rows = CF_ROWS + (SUBLANES if phase else 0)
            term = dw_ref[j:j + 1, :] * hbuf[pl.ds(base, rows), :]
            part = term if part is None else part + term
        if part is not None:
            acc = acc + part[phase:phase + CF_ROWS, :]
    mu = jnp.mean(acc, axis=-1, keepdims=True)
    cen = acc - mu
    var = jnp.mean(cen * cen, axis=-1, keepdims=True)
    y = cen * lax.rsqrt(var + EPS) * lnw_ref[...] + lnb_ref[...]
    out_ref[r * CF_ROWS:(r + 1) * CF_ROWS, :] = _silu(y).astype(out_ref.dtype)


def _in_proj_kernel(x_ref, nw_ref, w_ref, dw_ref, db_ref, lnw_ref, lnb_ref, *refs):
    out_refs, stage, hbuf = refs[:-2], refs[-2], refs[-1]
    tm = x_ref.shape[0]

    @pl.when(pl.program_id(1) == 0)
    def _():
        hbuf[0:CF_PAD, :] = jnp.zeros((CF_PAD, CF_CH), F32)

    h = _rms(x_ref[...], nw_ref[...]).astype(BF16)
    rows16 = tm // CMP_STRIDE

    def emit(first, last):
        off = sum(SEC_WIDTHS[:first])
        wide = _dot(h, w_ref[:, off:off + sum(SEC_WIDTHS[first:last])])
        col = 0
        for k in range(first, last):
            y = wide[:, col:col + SEC_WIDTHS[k]]
            col += SEC_WIDTHS[k]
            if k == SEC_CF:
                hbuf[CF_PAD:CF_PAD + tm, :] = y[:, :CF_CH] * _sigmoid(y[:, CF_CH:])
            elif k < N_ROWMAJOR:
                out_refs[k][...] = y.astype(out_refs[k].dtype)
            else:
                stage[...] = y
                for t in range(CMP_STRIDE):
                    out_refs[k][:, t * LANES:(t + 1) * LANES] = stage[pl.ds(t, rows16, stride=CMP_STRIDE), :]

    runs = sorted(_merged_sections(), key=lambda r: not (r[0] <= SEC_CF < r[1]))
    emit(*runs[0])
    nblocks = tm // CF_ROWS
    done = 0
    for n, run in enumerate(runs[1:], start=1):
        upto = nblocks * n // (len(runs) - 1)
        for r in range(done, upto):
            _conformer_rows(r, dw_ref, db_ref, lnw_ref, lnb_ref, out_refs[SEC_CF], hbuf)
        done = upto
        emit(*run)
    hbuf[0:CF_PAD, :] = hbuf[tm:tm + CF_PAD, :]


def _in_proj(x2, nw, w_perm, dw, db, lnw, lnb, seq, tm):
    m, d = x2.shape
    n = w_perm.shape[1]
    nt = seq // tm
    shapes, blocks, dtypes = [], [], []
    for k, wd in enumerate(SEC_WIDTHS):
        if k == SEC_CF:
            shapes.append((m, CF_CH)); blocks.append((tm, CF_CH)); dtypes.append(BF16)
        elif k < N_ROWMAJOR:
            shapes.append((m, wd)); blocks.append((tm, wd)); dtypes.append(BF16 if k in SEC_BF16 else F32)
        else:
            shapes.append((m // CMP_STRIDE, CMP_STRIDE * wd)); blocks.append((tm // CMP_STRIDE, CMP_STRIDE * wd))
            dtypes.append(F32)
    row = lambda bi, t: (bi * nt + t, 0)
    const = lambda bi, t: (0, 0)
    return pl.pallas_call(
        _in_proj_kernel,
        grid=(m // seq, nt),
        in_specs=[pl.BlockSpec((tm, d), row),
                  pl.BlockSpec((1, d), const),
                  pl.BlockSpec((d, n), const),
                  pl.BlockSpec((CF_KERNEL, CF_CH), const),
                  pl.BlockSpec((1, CF_CH), const),
                  pl.BlockSpec((1, CF_CH), const),
                  pl.BlockSpec((1, CF_CH), const)],
        out_specs=[pl.BlockSpec(blk, row) for blk in blocks],
        out_shape=[jax.ShapeDtypeStruct(shp, dt) for shp, dt in zip(shapes, dtypes)],
        scratch_shapes=[pltpu.VMEM((tm, LANES), F32),
                        pltpu.VMEM((tm + CF_PAD, CF_CH), F32)],
        compiler_params=pltpu.CompilerParams(dimension_semantics=("parallel", "arbitrary"),
                                             vmem_limit_bytes=VMEM_LIMIT),
        name="in_proj",
    )(x2, nw, w_perm, dw, db, lnw, lnb)


def _mix_ffn_kernel(x_ref, odn_ref, ocf_ref, onsa_ref, wo_ref, nw_ref, wg_ref, wu_ref, wd_ref, out_ref):
    mix = _dot(jnp.concatenate([odn_ref[...], ocf_ref[...], onsa_ref[...]], axis=1), wo_ref[...])
    x1 = x_ref[...] + _rms(mix, nw_ref[1:2, :])
    h = _rms(x1, nw_ref[2:3, :]).astype(BF16)
    g = _dot(h, wg_ref[...])
    u = _dot(h, wu_ref[...])
    a = (_silu(g) * u).astype(BF16)
    f = _dot(a, wd_ref[...])
    out_ref[...] = x1 + _rms(f, nw_ref[3:4, :])


def _mix_ffn(x2, odn, ocf, onsa, wo, nw, wg, wu, wd, tm):
    m, d = x2.shape
    dff = wg.shape[1]
    const = lambda i: (0, 0)
    row = lambda i: (i, 0)
    return pl.pallas_call(
        _mix_ffn_kernel,
        grid=(m // tm,),
        in_specs=[pl.BlockSpec((tm, d), row),
                  pl.BlockSpec((tm, odn.shape[1]), row),
                  pl.BlockSpec((tm, ocf.shape[1]), row),
                  pl.BlockSpec((tm, onsa.shape[1]), row),
                  pl.BlockSpec(wo.shape, const),
                  pl.BlockSpec(nw.shape, const),
                  pl.BlockSpec((d, dff), const),
                  pl.BlockSpec((d, dff), const),
                  pl.BlockSpec((dff, d), const)],
        out_specs=pl.BlockSpec((tm, d), row),
        out_shape=jax.ShapeDtypeStruct((m, d), F32),
        compiler_params=pltpu.CompilerParams(dimension_semantics=("parallel",), vmem_limit_bytes=VMEM_LIMIT),
        name="mix_ffn",
    )(x2, odn, ocf, onsa, wo, nw, wg, wu, wd)


DN_PAD = 8


def _deltanet_kernel(qkv_ref, z_ref, sm_ref, cw_ref, alog_ref, dtb_ref, nw_ref, out_ref, xbuf, ybuf, state):
    n = SUPER
    nchunk = n // DN_CHUNK
    nb = qkv_ref.shape[0]

    @pl.when(pl.program_id(1) == 0)
    def _():
        xbuf[:, 0:DN_PAD, :] = jnp.zeros((nb, DN_PAD, DN_QKV), F32)
        state[...] = jnp.zeros(state.shape, F32)

    hr = lax.broadcasted_iota(jnp.int32, (DN_QK, DN_QK), 0) // DN_DK
    hc = lax.broadcasted_iota(jnp.int32, (DN_QK, DN_QK), 1) // DN_DK
    head_ones = (hr == hc).astype(BF16)
    ri = lax.broadcasted_iota(jnp.int32, (n, n), 0)
    ci = lax.broadcasted_iota(jnp.int32, (n, n), 1)
    same_chunk = (ri // DN_CHUNK) == (ci // DN_CHUNK)
    causal = same_chunk & (ri >= ci)
    strict = same_chunk & (ri > ci)
    causal_bf = causal.astype(BF16)
    same_chunk_bf = same_chunk.astype(BF16)
    nfac = (DN_CHUNK - 1).bit_length() - 1
    lane_chunk = lax.broadcasted_iota(jnp.int32, (1, n), 1) // DN_CHUNK
    in_chunk = [lane_chunk == c for c in range(nchunk)]
    in_chunk_bf = [m.astype(BF16) for m in in_chunk]

    def to_rowform(bd):
        return sum(jnp.where(in_chunk[c], bd[c * DN_CHUNK:(c + 1) * DN_CHUNK, :], 0.0) for c in range(nchunk))

    def to_blockdiag(rowform):
        return jnp.concatenate([rowform * in_chunk_bf[c] for c in range(nchunk)], axis=0)

    first = DN_PAD - (DN_CONV - 1)
    rows = []
    for bi in range(nb):
        xbuf[bi, DN_PAD:DN_PAD + n, :] = qkv_ref[bi]
        y = cw_ref[DN_CONV - 1:DN_CONV, :] * xbuf[bi, pl.ds(DN_PAD, n), :]
        for j in range(DN_CONV - 1):
            y = y + cw_ref[j:j + 1, :] * xbuf[bi, pl.ds(first + j, n), :]
        xbuf[bi, 0:DN_PAD, :] = xbuf[bi, n:n + DN_PAD, :]
        ybuf[bi] = _silu(y)
        q_all = ybuf[bi, :, 0:DN_QK]
        k_all = ybuf[bi, :, DN_QK:2 * DN_QK]
        v_all = ybuf[bi, :, 2 * DN_QK:]
        q_n = q_all * lax.rsqrt(_dot_x01(q_all * q_all, head_ones, 1) + EPS) * (DN_DK ** -0.5)
        k_n = k_all * lax.rsqrt(_dot_x01(k_all * k_all, head_ones, 1) + EPS)
        sm = sm_ref[bi]
        beta_all = _sigmoid(sm)
        sp_in = sm + dtb_ref[...]
        softplus = jnp.maximum(sp_in, 0.0) + jnp.log1p(jnp.exp(-jnp.abs(sp_in)))
        g_all = -jnp.exp(alog_ref[...]) * softplus
        gam = _dot_01x(causal_bf, g_all, 3)
        glast = _dot_01x(same_chunk_bf, g_all, 3)
        rows.append(dict(q_n=q_n, k_n=k_n, v_all=v_all, beta_all=beta_all, gam=gam, gam_t=gam.T,
                         e_gam=jnp.exp(gam), e_tot=jnp.exp(glast), e_rest_t=jnp.exp(glast - gam).T,
                         k_t=k_n.T))

    heads = [(bi, h) for bi in range(nb) for h in range(DN_HEADS)]
    q, k, v, bcol, egcol, etcol, lmat, qkd = {}, {}, {}, {}, {}, {}, {}, {}
    for key in heads:
        bi, h = key
        r = rows[bi]
        sl = slice(h * DN_DK, (h + 1) * DN_DK)
        lg = LANE_A + h
        q[key], k[key], v[key] = r["q_n"][:, sl], r["k_n"][:, sl], r["v_all"][:, sl]
        bcol[key] = r["beta_all"][:, LANE_BETA + h:LANE_BETA + h + 1]
        egcol[key] = r["e_gam"][:, lg:lg + 1]
        etcol[key] = r["e_tot"][:, lg:lg + 1]
        kt = r["k_t"][h * DN_DK:(h + 1) * DN_DK, :].astype(BF16)
        dec = jnp.exp(jnp.where(causal, r["gam"][:, lg:lg + 1] - r["gam_t"][lg:lg + 1, :], NEG))
        lmat[key] = jnp.where(strict, _dot((k[key] * bcol[key]).astype(BF16), kt) * dec, 0.0)
        qkd[key] = (_dot(q[key].astype(BF16), kt) * dec).astype(BF16)
    ps = {key: -to_rowform(lmat[key]) for key in heads}
    mp = {key: _dot((-ps[key]).astype(BF16), lmat[key].astype(BF16)) for key in heads}
    for s in range(nfac):
        mb = {key: mp[key].astype(BF16) for key in heads}
        mbd = {key: to_blockdiag(mb[key]) for key in heads}
        prod = {key: _dot(ps[key].astype(BF16), mbd[key]) for key in heads}
        ps = {key: ps[key] + mp[key] + prod[key] for key in heads}
        if s < nfac - 1:
            mp = {key: _dot(mb[key], mbd[key]) for key in heads}
    qeff, o_intra, b_in, n_in = {}, {}, {}, {}
    rhs = {key: jnp.concatenate([v[key] * bcol[key], k[key] * (bcol[key] * egcol[key])], axis=1) for key in heads}
    tx = {key: _dot(to_blockdiag(ps[key].astype(BF16)), rhs[key].astype(BF16)) for key in heads}
    x = {key: (rhs[key] + tx[key]).astype(BF16) for key in heads}
    qx = {key: _dot(qkd[key], x[key]) for key in heads}
    kdt = {(bi, h): (rows[bi]["k_t"][h * DN_DK:(h + 1) * DN_DK, :]
                     * rows[bi]["e_rest_t"][LANE_A + h:LANE_A + h + 1, :]).astype(BF16) for bi, h in heads}
    kx = {key: _dot(to_blockdiag(kdt[key]), x[key]) for key in heads}
    for key in heads:
        qeff[key] = (q[key] * egcol[key] - qx[key][:, DN_DV:]).astype(BF16)
        o_intra[key] = qx[key][:, :DN_DV]
        b_in[key], n_in[key] = kx[key][:, :DN_DV], kx[key][:, DN_DV:].astype(BF16)
    st = {(bi, h): state[bi * DN_HEADS + h] for bi, h in heads}
    o_chunks = {key: [] for key in heads}
    for c in range(nchunk):
        rs = slice(c * DN_CHUNK, (c + 1) * DN_CHUNK)
        ks = slice(c * DN_DK, (c + 1) * DN_DK)
        stb = {key: st[key].astype(BF16) for key in heads}
        corr = {key: _dot(n_in[key][ks], stb[key]) for key in heads}
        for key in heads:
            o_chunks[key].append(_dot(qeff[key][rs], stb[key]) + o_intra[key][rs])
            st[key] = st[key] * etcol[key][rs] + b_in[key][ks] - corr[key]
    for bi in range(nb):
        for h in range(DN_HEADS):
            state[bi * DN_HEADS + h] = st[(bi, h)]
        o = jnp.concatenate([jnp.concatenate(o_chunks[(bi, h)], axis=0) for h in range(DN_HEADS)], axis=1)
        ms = _dot_x01(o * o, head_ones, 1) * (1.0 / DN_DV)
        o = o * lax.rsqrt(ms + EPS) * nw_ref[...]
        out_ref[bi] = (o * _silu(z_ref[bi])).astype(out_ref.dtype)


def _deltanet(qkv, z, small, cw, alog_row, dtb_row, nw_row):
    b, s, _ = qkv.shape
    nb = DN_BATCH if b % DN_BATCH == 0 else 1
    const = lambda i, t: (0, 0)
    tile = lambda i, t: (i, t, 0)
    return pl.pallas_call(
        _deltanet_kernel,
        grid=(b // nb, s // SUPER),
        in_specs=[pl.BlockSpec((nb, SUPER, DN_QKV), tile),
                  pl.BlockSpec((nb, SUPER, DN_QK), tile),
                  pl.BlockSpec((nb, SUPER, LANES), tile),
                  pl.BlockSpec((DN_CONV, DN_QKV), const),
                  pl.BlockSpec((1, LANES), const),
                  pl.BlockSpec((1, LANES), const),
                  pl.BlockSpec((1, DN_QK), const)],
        out_specs=pl.BlockSpec((nb, SUPER, DN_QK), tile),
        out_shape=jax.ShapeDtypeStruct((b, s, DN_QK), BF16),
        scratch_shapes=[pltpu.VMEM((nb, SUPER + DN_PAD, DN_QKV), F32),
                        pltpu.VMEM((nb, SUPER, DN_QKV), F32),
                        pltpu.VMEM((nb * DN_HEADS, DN_DK, DN_DV), F32)],
        compiler_params=pltpu.CompilerParams(dimension_semantics=("parallel", "arbitrary"),
                                             vmem_limit_bytes=VMEM_LIMIT),
        name="deltanet",
    )(qkv, z, small, cw, alog_row, dtb_row, nw_row)


def _compress(t_ref, row, posa_ref, posb_ref, w1a_ref, w1b_ref, w2_ref):
    t = t_ref[row]
    nrow = t.shape[0]
    p = _dot((t + posa_ref[...]).astype(BF16), w1a_ref[...])
    q = _dot((t + posb_ref[...]).astype(BF16), w1b_ref[...])
    hid = _silu(p + pltpu.roll(q, nrow - 1, 0))
    return _dot(hid.astype(BF16), w2_ref[...])


def _store_vt(dst, row, v_ref):
    vt = v_ref[row].astype(F32).T
    ones = jnp.ones((BF16_ROWS, TQ), BF16)
    for g in range(NSA_KV):
        for c in range(dst.shape[1]):
            blk = vt[g * NSA_DH:(g + 1) * NSA_DH, c * TQ:(c + 1) * TQ].astype(BF16)
            dst[row * NSA_KV + g, c] = jnp.concatenate([blk, ones], axis=0)


def _attend_chunks(chunks, items, m_s, acc_s):
    scores = []
    for j, mask in chunks:
        row = []
        for _, q, key_chunk, _ in items:
            s = _dot(key_chunk(j), q)
            if mask is not None:
                s = jnp.where(mask, s, NEG)
            row.append(s.astype(BF16))
        scores.append(row)
    for (j, _), chunk_scores in zip(chunks, scores):
        probs, alphas = [], []
        for (slot, _, _, _), s in zip(items, chunk_scores):
            m_prev = m_s[slot]
            m_new = jnp.maximum(m_prev, jnp.max(s, axis=0, keepdims=True).astype(F32))
            probs.append(jnp.exp(s - m_new.astype(BF16)))
            alphas.append(jnp.exp(m_prev - m_new))
            m_s[slot] = m_new
        pv = [_dot(vt_chunk(j), p) for (_, _, _, vt_chunk), p in zip(items, probs)]
        for (slot, _, _, _), a, o in zip(items, alphas, pv):
            acc_s[slot] = a * acc_s[slot] + o


def _nsa_kernel(q_ref, sm_ref, kct_ref, vct_ref, ks_ref, vs_ref, kw_ref, vw_ref,
                kposa_ref, kposb_ref, kw1a_ref, kw1b_ref, kw2_ref,
                vposa_ref, vposb_ref, vw1a_ref, vw1b_ref, vw2_ref,
                out_ref,
                kc_s, vct_s, ksb, kwb, vst_s, vwt_s, qt_s, oc_s, m_s, acc_s):
    i = pl.program_id(1)
    nb = q_ref.shape[0]
    ncmp = kc_s.shape[1]
    nblk = ks_ref.shape[1] // SEL_BLOCK
    heads = [(bi, idx) for bi in range(nb) for idx in range(NSA_HEADS)]
    slot = lambda bi, idx: bi * NSA_HEADS + idx

    @pl.when(i == 0)
    def _():
        s_len = ks_ref.shape[1]
        key_blk = lax.broadcasted_iota(jnp.int32, (s_len, LANES), 0) // SEL_BLOCK
        blk_onehot = (lax.broadcasted_iota(jnp.int32, (s_len, LANES), 1) == key_blk).astype(BF16)
        for bi in range(nb):
            kc = _compress(kct_ref, bi, kposa_ref, kposb_ref, kw1a_ref, kw1b_ref, kw2_ref)
            kc_hi, kc_lo = _split_terms(kc, 2)
            kc_s[2 * bi] = kc_hi
            kc_s[2 * bi + 1] = kc_lo
            vct_s[bi] = _compress(vct_ref, bi, vposa_ref, vposb_ref, vw1a_ref, vw1b_ref, vw2_ref).T.astype(BF16)
            ksb[bi, :, 0:LANES] = ks_ref[bi]
            ksb[bi, :, LANES:] = blk_onehot
            kwb[bi] = kw_ref[bi]
            for idx in range(NSA_HEADS):
                qt_s[slot(bi, idx), LANES + nblk:, :] = jnp.zeros((LANES - nblk, TQ), BF16)
            _store_vt(vst_s, bi, vs_ref)
            _store_vt(vwt_s, bi, vw_ref)

    trow = i * TQ + lax.broadcasted_iota(jnp.int32, (1, TQ), 1)
    krel = lax.broadcasted_iota(jnp.int32, (TQ, 1), 0)
    cmp_end = lax.broadcasted_iota(jnp.int32, (ncmp, 1), 0) * CMP_STRIDE + (CMP_LEN - 1)
    valid_c = cmp_end <= trow
    oj = lax.broadcasted_iota(jnp.int32, (nblk, ncmp), 0) * SEL_BLOCK
    on = lax.broadcasted_iota(jnp.int32, (nblk, ncmp), 1) * CMP_STRIDE
    ov_t = ((on < oj + SEL_BLOCK) & (on + CMP_LEN > oj)).astype(BF16)
    jj = lax.broadcasted_iota(jnp.int32, (nblk, TQ), 0)
    cur = trow // SEL_BLOCK
    elig = jj <= cur
    forced = (jj == 0) | (jj == cur) | (jj == cur - 1)

    s_c = {}
    for bi in range(nb):
        qt_all = (q_ref[bi] * (NSA_DH ** -0.5)).T
        kc_hi, kc_lo = kc_s[2 * bi], kc_s[2 * bi + 1]
        zero_half = jnp.zeros((NSA_DH, TQ), F32)
        for idx in range(NSA_HEADS):
            qh = qt_all[idx * NSA_DH:(idx + 1) * NSA_DH]
            qt = jnp.concatenate([qh, zero_half] if idx // NSA_HPG == 0 else [zero_half, qh], axis=0)
            qt_hi, qt_lo = _split_terms(qt, 2)
            qt_s[slot(bi, idx), 0:LANES, :] = qt_hi
            s_c[(bi, idx)] = _dot(kc_hi, qt_hi) + _dot(kc_hi, qt_lo) + _dot(kc_lo, qt_hi)
    probs = {}
    for key in heads:
        s = jnp.where(valid_c, s_c[key], NEG)
        e = jnp.where(valid_c, jnp.exp(s - jnp.max(s, axis=0, keepdims=True)), 0.0)
        probs[key] = e * (1.0 / jnp.maximum(jnp.sum(e, axis=0, keepdims=True), 1e-30))
    for bi, idx in heads:
        g = idx // NSA_HPG
        oc_s[slot(bi, idx)] = _dot(vct_s[bi, g * NSA_DH:(g + 1) * NSA_DH, :], probs[(bi, idx)].astype(BF16))
    for bi in range(nb):
        for g in range(NSA_KV):
            psum = sum(probs[(bi, idx)] for idx in range(g * NSA_HPG, (g + 1) * NSA_HPG))
            imp_t = _dot_01x(ov_t, psum, 3)
            score = jnp.where(elig, imp_t + jnp.where(forced, SEL_BIG, 0.0), -SEL_BIG)
            rank = jnp.zeros((nblk, TQ), F32)
            for r in range(nblk):
                row = score[r:r + 1, :]
                beats = (row > score) | ((row == score) & (r < jj))
                rank = rank + beats.astype(F32)
            bias = jnp.where(rank < float(N_SEL), 0.0, NEG).astype(BF16)
            for hh in range(NSA_HPG):
                qt_s[slot(bi, g * NSA_HPG + hh), LANES:LANES + nblk, :] = bias

    m_s[...] = jnp.full(m_s.shape, NEG, BF16).astype(F32)
    acc_s[...] = jnp.zeros(acc_s.shape, F32)
    qrel = lax.broadcasted_iota(jnp.int32, (1, TQ), 1)
    on_or_below = krel <= qrel
    nslots = nb * NSA_HEADS

    sel_items = [(slot(bi, idx), qt_s[slot(bi, idx)],
                  lambda j, bi=bi: ksb[bi, pl.ds(pl.multiple_of(j * TQ, TQ), TQ), :],
                  lambda j, bi=bi, idx=idx: vst_s[bi * NSA_KV + idx // NSA_HPG, j]) for bi, idx in heads]

    def sel_step(j, carry):
        _attend_chunks([(j, None)], sel_items, m_s, acc_s)
        return carry

    lax.fori_loop(0, i, sel_step, 0)
    _attend_chunks([(i, on_or_below)], sel_items, m_s, acc_s)

    win_items = [(nslots + slot(bi, idx), qt_s[slot(bi, idx), 0:LANES, :],
                  lambda j, bi=bi: kwb[bi, pl.ds(pl.multiple_of(j * TQ, TQ), TQ), :],
                  lambda j, bi=bi, idx=idx: vwt_s[bi * NSA_KV + idx // NSA_HPG, j]) for bi, idx in heads]

    _attend_chunks([(i, on_or_below)], win_items, m_s, acc_s)

    @pl.when(i >= 1)
    def _():
        _attend_chunks([(i - 1, None)], win_items, m_s, acc_s)

    @pl.when(i >= 2)
    def _():
        _attend_chunks([(i - 2, krel > qrel)], win_items, m_s, acc_s)

    for bi in range(nb):
        gates_t = _sigmoid(sm_ref[bi]).T
        outs = []
        for idx in range(NSA_HEADS):
            acc = acc_s[slot(bi, idx)]
            o_s = acc[:NSA_DH] * (1.0 / acc[NSA_DH:NSA_DH + 1])
            acc = acc_s[nslots + slot(bi, idx)]
            o_w = acc[:NSA_DH] * (1.0 / acc[NSA_DH:NSA_DH + 1])
            lg = LANE_GATE + idx
            outs.append(gates_t[lg:lg + 1] * oc_s[slot(bi, idx)] + gates_t[lg + NSA_HEADS:lg + NSA_HEADS + 1] * o_s
                        + gates_t[lg + 2 * NSA_HEADS:lg + 2 * NSA_HEADS + 1] * o_w)
        out_ref[bi] = jnp.concatenate(outs, axis=0).T.astype(out_ref.dtype)


def _nsa(nq, small, kct, vct, ks, vs, kw, vw, kparams, vparams):
    b, s, _ = nq.shape
    nb = NSA_BATCH if b % NSA_BATCH == 0 else 1
    ncmp = kct.shape[1]
    nchunks = s // TQ
    tile = lambda bi, i: (bi, i, 0)
    full = lambda bi, i: (bi, 0, 0)
    const = lambda bi, i: (0, 0)
    wspecs = [pl.BlockSpec(p.shape, const, pipeline_mode=pl.Buffered(1)) for p in kparams + vparams]
    return pl.pallas_call(
        _nsa_kernel,
        grid=(b // nb, nchunks),
        in_specs=[pl.BlockSpec((nb, TQ, NSA_HEADS * NSA_DH), tile),
                  pl.BlockSpec((nb, TQ, LANES), tile),
                  pl.BlockSpec((nb, ncmp, kct.shape[2]), full),
                  pl.BlockSpec((nb, ncmp, vct.shape[2]), full),
                  pl.BlockSpec((nb, s, LANES), full),
                  pl.BlockSpec((nb, s, LANES), full),
                  pl.BlockSpec((nb, s, LANES), full),
                  pl.BlockSpec((nb, s, LANES), full)] + wspecs,
        out_specs=pl.BlockSpec((nb, TQ, NSA_HEADS * NSA_DH), tile),
        out_shape=jax.ShapeDtypeStruct((b, s, NSA_HEADS * NSA_DH), BF16),
        scratch_shapes=[pltpu.VMEM((2 * nb, ncmp, LANES), BF16),
                        pltpu.VMEM((nb, LANES, ncmp), BF16),
                        pltpu.VMEM((nb, s, 2 * LANES), BF16),
                        pltpu.VMEM((nb, s, LANES), BF16),
                        pltpu.VMEM((nb * NSA_KV, nchunks, VT_ROWS, TQ), BF16),
                        pltpu.VMEM((nb * NSA_KV, nchunks, VT_ROWS, TQ), BF16),
                        pltpu.VMEM((nb * NSA_HEADS, 2 * LANES, TQ), BF16),
                        pltpu.VMEM((nb * NSA_HEADS, NSA_DH, TQ), F32),
                        pltpu.VMEM((2 * nb * NSA_HEADS, 1, TQ), F32),
                        pltpu.VMEM((2 * nb * NSA_HEADS, VT_ROWS, TQ), F32)],
        compiler_params=pltpu.CompilerParams(dimension_semantics=("parallel", "arbitrary"),
                                             vmem_limit_bytes=VMEM_LIMIT),
        name="nsa",
    )(nq, small, kct, vct, ks, vs, kw, vw, *kparams, *vparams)


def _source_columns():
    o = 0
    qkv = list(range(o, o + DN_QKV)); o += DN_QKV
    z = list(range(o, o + DN_HEADS * DN_DV)); o += DN_HEADS * DN_DV
    b_raw = list(range(o, o + DN_HEADS)); o += DN_HEADS
    a_raw = list(range(o, o + DN_HEADS)); o += DN_HEADS
    cf_u = list(range(o, o + 2 * CF_CH)); o += 2 * CF_CH
    n_q = list(range(o, o + NSA_HEADS * NSA_DH)); o += NSA_HEADS * NSA_DH
    kv = []
    for _ in range(6):
        kv.append(list(range(o, o + NSA_KV * NSA_DH))); o += NSA_KV * NSA_DH
    gate = list(range(o, o + 3 * NSA_HEADS)); o += 3 * NSA_HEADS
    n_kc, n_vc, n_ks, n_vs, n_kw, n_vw = kv
    small = b_raw + a_raw + gate
    small = small + [-1] * (LANES - len(small))
    cols = qkv + z + small + cf_u + n_q + n_ks + n_vs + n_kw + n_vw + n_kc + n_vc
    assert len(cols) == sum(SEC_WIDTHS)
    return np.asarray(cols, np.int32), o


def _permute_w_in(w):
    cols, n_in = _source_columns()
    assert w.shape[1] == n_in
    select = (lax.broadcasted_iota(jnp.int32, (n_in, cols.shape[0]), 0) == cols[None, :]).astype(BF16)
    return jnp.dot(w.astype(BF16), select, preferred_element_type=BF16)


def _compress_params(pos, w1, w2):
    eye = jnp.eye(NSA_KV, dtype=F32)
    half = CMP_LEN // 2
    w1r = w1.reshape(CMP_LEN, NSA_DH, CMP_HID)

    def expand_w1(part):
        return jnp.einsum("ldj,gh->lgdhj", part, eye).reshape(half * NSA_KV * NSA_DH, NSA_KV * CMP_HID).astype(BF16)

    def expand_pos(part):
        return jnp.broadcast_to(part[:, None, :], (half, NSA_KV, NSA_DH)).reshape(1, half * NSA_KV * NSA_DH)

    w2e = jnp.einsum("jd,gh->gjhd", w2, eye).reshape(NSA_KV * CMP_HID, NSA_KV * NSA_DH).astype(BF16)
    return [expand_pos(pos[:half]), expand_pos(pos[half:]), expand_w1(w1r[:half]), expand_w1(w1r[half:]), w2e]


def _lane_row(vals, first_lane):
    pad = jnp.zeros((LANES - first_lane - vals.shape[0],), F32)
    return jnp.concatenate([jnp.zeros((first_lane,), F32), vals.astype(F32), pad])[None, :]


def kernel(x, norm_w, w_in, dn_conv_w, dn_a_log, dn_dt_bias, dn_norm_w, cf_dw_w, cf_dw_b, cf_ln_w, cf_ln_b,
           nsa_k_pos, nsa_v_pos, nsa_k_w1, nsa_k_w2, nsa_v_w1, nsa_v_w2, w_out, ffn_w_gate, ffn_w_up, ffn_w_down):
    b, s, d = x.shape
    depth = w_in.shape[0]
    assert s % SUPER == 0 and s % TQ == 0 and s // CMP_STRIDE == LANES and s // SEL_BLOCK <= LANES
    assert WINDOW == 2 * TQ and TQ % SEL_BLOCK == 0
    m = b * s
    tm = 512 if m % 512 == 0 else 256
    x2 = x.reshape(m, d)
    ndn = DN_HEADS * DN_DV
    rows16 = s // CMP_STRIDE
    for l in range(depth):
        secs = _in_proj(x2, norm_w[l, 0:1], _permute_w_in(w_in[l]), cf_dw_w[l], cf_dw_b[l][None, :],
                        cf_ln_w[l][None, :], cf_ln_b[l][None, :], s, tm)
        qkv, z, small, o_cf, nq, n_ks, n_vs, n_kw, n_vw = [t.reshape(b, s, t.shape[1]) for t in secs[:N_ROWMAJOR]]
        n_kct, n_vct = [t.reshape(b, rows16, t.shape[1]) for t in secs[N_ROWMAJOR:]]
        o_dn = _deltanet(qkv, z, small, dn_conv_w[l], _lane_row(dn_a_log[l], LANE_A), _lane_row(dn_dt_bias[l], LANE_A),
                         jnp.tile(dn_norm_w[l], DN_HEADS)[None, :])
        o_nsa = _nsa(nq, small, n_kct, n_vct, n_ks, n_vs, n_kw, n_vw,
                     _compress_params(nsa_k_pos[l], nsa_k_w1[l], nsa_k_w2[l]),
                     _compress_params(nsa_v_pos[l], nsa_v_w1[l], nsa_v_w2[l]))
        x2 = _mix_ffn(x2, o_dn.reshape(m, ndn), o_cf.reshape(m, CF_CH), o_nsa.reshape(m, NSA_HEADS * NSA_DH),
                      w_out[l].astype(BF16), norm_w[l],
                      ffn_w_gate[l].astype(BF16), ffn_w_up[l].astype(BF16), ffn_w_down[l].astype(BF16), tm)
    return x2.reshape(b, s, d)
```

```python
import jax
import jax.numpy as jnp
import numpy as np
from jax import lax
from jax.experimental import pallas as pl
from jax.experimental.pallas import tpu as pltpu

F32 = jnp.float32
BF16 = jnp.bfloat16

DN_HEADS = 6
DN_DK = 64
DN_DV = 64
DN_CONV = 4
DN_CHUNK = 64
CF_CH = 256
CF_KERNEL = 31
NSA_HEADS = 6
NSA_KV = 2
NSA_HPG = NSA_HEADS // NSA_KV
NSA_DH = 64
CMP_LEN = 32
CMP_STRIDE = 16
CMP_HID = 2 * NSA_DH
SEL_BLOCK = 64
N_SEL = 8
WINDOW = 512
EPS = 1e-6
NEG = -1e30
SEL_BIG = 1e4

LANES = 128
SUBLANES = 8
BF16_ROWS = 16
DN_QK = DN_HEADS * DN_DK
DN_QKV = 2 * DN_QK + DN_HEADS * DN_DV
SUPER = 256
DN_BATCH = 2
NSA_BATCH = 2
TQ = 256
VT_ROWS = NSA_DH + BF16_ROWS
VMEM_LIMIT = 56 * 1024 * 1024

SEC_WIDTHS = (DN_QKV, DN_HEADS * DN_DV, LANES, 2 * CF_CH, NSA_HEADS * NSA_DH,
              LANES, LANES, LANES, LANES, LANES, LANES)
N_ROWMAJOR = 9
SEC_CF = 3
SEC_BF16 = (5, 6, 7, 8)
CF_ROWS = 64
CF_PAD = 32
LANE_BETA = 0
LANE_A = DN_HEADS
LANE_GATE = 2 * DN_HEADS


def _dot(a, b, precision=None):
    return jnp.dot(a, b, preferred_element_type=F32, precision=precision)


def _split_terms(x, terms):
    out = []
    for _ in range(terms):
        hi = x.astype(BF16)
        out.append(hi)
        x = x - hi.astype(F32)
    return out


def _dot_x01(x, mat01, terms):
    return sum(_dot(p, mat01) for p in _split_terms(x, terms))


def _dot_01x(mat01, x, terms):
    return sum(_dot(mat01, p) for p in _split_terms(x, terms))


def _sigmoid(x):
    return 1.0 / (1.0 + jnp.exp(-x))


def _silu(x):
    return x * _sigmoid(x)


def _rms(x, w):
    return x * lax.rsqrt(jnp.mean(x * x, axis=-1, keepdims=True) + EPS) * w


IN_PROJ_COLS = 1792


def _merged_sections():
    runs, first, width = [], 0, 0
    for k, wd in enumerate(SEC_WIDTHS):
        if width and width + wd > IN_PROJ_COLS:
            runs.append((first, k))
            first, width = k, 0
        width += wd
    runs.append((first, len(SEC_WIDTHS)))
    return runs


def _conformer_rows(r, dw_ref, db_ref, lnw_ref, lnb_ref, out_ref, hbuf):
    first = CF_PAD - (CF_KERNEL - 1)
    acc = jnp.zeros((CF_ROWS, CF_CH), F32) + db_ref[...]
    for phase in range(SUBLANES):
        part = None
        for j in range(CF_KERNEL):
            if (first + j) % SUBLANES != phase:
                continue
            base = r * CF_ROWS + first + j - phase
            rows = CF_ROWS + (SUBLANES if phase else 0)
            term = dw_ref[j:j + 1, :] * hbuf[pl.ds(base, rows), :]
            part = term if part is None else part + term
        if part is not None:
            acc = acc + part[phase:phase + CF_ROWS, :]
    mu = jnp.mean(acc, axis=-1, keepdims=True)
    cen = acc - mu
    var = jnp.mean(cen * cen, axis=-1, keepdims=True)
    y = cen * lax.rsqrt(var + EPS) * lnw_ref[...] + lnb_ref[...]
    out_ref[r * CF_ROWS:(r + 1) * CF_ROWS, :] = _silu(y).astype(out_ref.dtype)


def _in_proj_kernel(x_ref, nw_ref, w_ref, dw_ref, db_ref, lnw_ref, lnb_ref, *refs):
    out_refs, stage, hbuf = refs[:-2], refs[-2], refs[-1]
    tm = x_ref.shape[0]

    @pl.when(pl.program_id(1) == 0)
    def _():
        hbuf[0:CF_PAD, :] = jnp.zeros((CF_PAD, CF_CH), F32)

    h = _rms(x_ref[...], nw_ref[...]).astype(BF16)
    rows16 = tm // CMP_STRIDE

    def emit(first, last):
        off = sum(SEC_WIDTHS[:first])
        wide = _dot(h, w_ref[:, off:off + sum(SEC_WIDTHS[first:last])])
        col = 0
        for k in range(first, last):
            y = wide[:, col:col + SEC_WIDTHS[k]]
            col += SEC_WIDTHS[k]
            if k == SEC_CF:
                hbuf[CF_PAD:CF_PAD + tm, :] = y[:, :CF_CH] * _sigmoid(y[:, CF_CH:])
            elif k < N_ROWMAJOR:
                out_refs[k][...] = y.astype(out_refs[k].dtype)
            else:
                stage[...] = y
                for t in range(CMP_STRIDE):
                    out_refs[k][:, t * LANES:(t + 1) * LANES] = stage[pl.ds(t, rows16, stride=CMP_STRIDE), :]

    runs = sorted(_merged_sections(), key=lambda r: not (r[0] <= SEC_CF < r[1]))
    emit(*runs[0])
    nblocks = tm // CF_ROWS
    done = 0
    for n, run in enumerate(runs[1:], start=1):
        upto = nblocks * n // (len(runs) - 1)
        for r in range(done, upto):
            _conformer_rows(r, dw_ref, db_ref, lnw_ref, lnb_ref, out_refs[SEC_CF], hbuf)
        done = upto
        emit(*run)
    hbuf[0:CF_PAD, :] = hbuf[tm:tm + CF_PAD, :]


def _in_proj(x2, nw, w_perm, dw, db, lnw, lnb, seq, tm):
    m, d = x2.shape
    n = w_perm.shape[1]
    nt = seq // tm
    shapes, blocks, dtypes = [], [], []
    for k, wd in enumerate(SEC_WIDTHS):
        if k == SEC_CF:
            shapes.append((m, CF_CH)); blocks.append((tm, CF_CH)); dtypes.append(BF16)
        elif k < N_ROWMAJOR:
            shapes.append((m, wd)); blocks.append((tm, wd)); dtypes.append(BF16 if k in SEC_BF16 else F32)
        else:
            shapes.append((m // CMP_STRIDE, CMP_STRIDE * wd)); blocks.append((tm // CMP_STRIDE, CMP_STRIDE * wd))
            dtypes.append(F32)
    row = lambda bi, t: (bi * nt + t, 0)
    const = lambda bi, t: (0, 0)
    return pl.pallas_call(
        _in_proj_kernel,
        grid=(m // seq, nt),
        in_specs=[pl.BlockSpec((tm, d), row),
                  pl.BlockSpec((1, d), const),
                  pl.BlockSpec((d, n), const),
                  pl.BlockSpec((CF_KERNEL, CF_CH), const),
                  pl.BlockSpec((1, CF_CH), const),
                  pl.BlockSpec((1, CF_CH), const),
                  pl.BlockSpec((1, CF_CH), const)],
        out_specs=[pl.BlockSpec(blk, row) for blk in blocks],
        out_shape=[jax.ShapeDtypeStruct(shp, dt) for shp, dt in zip(shapes, dtypes)],
        scratch_shapes=[pltpu.VMEM((tm, LANES), F32),
                        pltpu.VMEM((tm + CF_PAD, CF_CH), F32)],
        compiler_params=pltpu.CompilerParams(dimension_semantics=("parallel", "arbitrary"),
                                             vmem_limit_bytes=VMEM_LIMIT),
        name="in_proj",
    )(x2, nw, w_perm, dw, db, lnw, lnb)


def _mix_ffn_kernel(x_ref, odn_ref, ocf_ref, onsa_ref, wo_ref, nw_ref, wg_ref, wu_ref, wd_ref, out_ref):
    mix = _dot(jnp.concatenate([odn_ref[...], ocf_ref[...], onsa_ref[...]], axis=1), wo_ref[...])
    x1 = x_ref[...] + _rms(mix, nw_ref[1:2, :])
    h = _rms(x1, nw_ref[2:3, :]).astype(BF16)
    g = _dot(h, wg_ref[...])
    u = _dot(h, wu_ref[...])
    a = (_silu(g) * u).astype(BF16)
    f = _dot(a, wd_ref[...])
    out_ref[...] = x1 + _rms(f, nw_ref[3:4, :])


def _mix_ffn(x2, odn, ocf, onsa, wo, nw, wg, wu, wd, tm):
    m, d = x2.shape
    dff = wg.shape[1]
    const = lambda i: (0, 0)
    row = lambda i: (i, 0)
    return pl.pallas_call(
        _mix_ffn_kernel,
        grid=(m // tm,),
        in_specs=[pl.BlockSpec((tm, d), row),
                  pl.BlockSpec((tm, odn.shape[1]), row),
                  pl.BlockSpec((tm, ocf.shape[1]), row),
                  pl.BlockSpec((tm, onsa.shape[1]), row),
                  pl.BlockSpec(wo.shape, const),
                  pl.BlockSpec(nw.shape, const),
                  pl.BlockSpec((d, dff), const),
                  pl.BlockSpec((d, dff), const),
                  pl.BlockSpec((dff, d), const)],
        out_specs=pl.BlockSpec((tm, d), row),
        out_shape=jax.ShapeDtypeStruct((m, d), F32),
        compiler_params=pltpu.CompilerParams(dimension_semantics=("parallel",), vmem_limit_bytes=VMEM_LIMIT),
        name="mix_ffn",
    )(x2, odn, ocf, onsa, wo, nw, wg, wu, wd)


DN_PAD = 8


def _deltanet_kernel(qkv_ref, z_ref, sm_ref, cw_ref, alog_ref, dtb_ref, nw_ref, out_ref, xbuf, ybuf, state):
    n = SUPER
    nchunk = n // DN_CHUNK
    nb = qkv_ref.shape[0]

    @pl.when(pl.program_id(1) == 0)
    def _():
        xbuf[:, 0:DN_PAD, :] = jnp.zeros((nb, DN_PAD, DN_QKV), F32)
        state[...] = jnp.zeros(state.shape, F32)

    hr = lax.broadcasted_iota(jnp.int32, (DN_QK, DN_QK), 0) // DN_DK
    hc = lax.broadcasted_iota(jnp.int32, (DN_QK, DN_QK), 1) // DN_DK
    head_ones = (hr == hc).astype(BF16)
    ri = lax.broadcasted_iota(jnp.int32, (n, n), 0)
    ci = lax.broadcasted_iota(jnp.int32, (n, n), 1)
    same_chunk = (ri // DN_CHUNK) == (ci // DN_CHUNK)
    causal = same_chunk & (ri >= ci)
    strict = same_chunk & (ri > ci)
    upper_bf = (same_chunk & (ri <= ci)).astype(BF16)
    same_chunk_bf = same_chunk.astype(BF16)
    nfac = (DN_CHUNK - 1).bit_length() - 1
    lane_chunk = lax.broadcasted_iota(jnp.int32, (1, n), 1) // DN_CHUNK
    in_chunk = [lane_chunk == c for c in range(nchunk)]
    in_chunk_bf = [m.astype(BF16) for m in in_chunk]

    def to_rowform(bd):
        return sum(jnp.where(in_chunk[c], bd[c * DN_CHUNK:(c + 1) * DN_CHUNK, :], 0.0) for c in range(nchunk))

    def to_blockdiag(rowform):
        return jnp.concatenate([rowform * in_chunk_bf[c] for c in range(nchunk)], axis=0)

    first = DN_PAD - (DN_CONV - 1)
    rows = []
    for bi in range(nb):
        xbuf[bi, DN_PAD:DN_PAD + n, :] = qkv_ref[bi]
        y = cw_ref[DN_CONV - 1:DN_CONV, :] * xbuf[bi, pl.ds(DN_PAD, n), :]
        for j in range(DN_CONV - 1):
            y = y + cw_ref[j:j + 1, :] * xbuf[bi, pl.ds(first + j, n), :]
        xbuf[bi, 0:DN_PAD, :] = xbuf[bi, n:n + DN_PAD, :]
        ybuf[bi] = _silu(y)
        q_all = ybuf[bi, :, 0:DN_QK]
        k_all = ybuf[bi, :, DN_QK:2 * DN_QK]
        v_all = ybuf[bi, :, 2 * DN_QK:]
        q_n = q_all * lax.rsqrt(_dot_x01(q_all * q_all, head_ones, 1) + EPS) * (DN_DK ** -0.5)
        k_n = k_all * lax.rsqrt(_dot_x01(k_all * k_all, head_ones, 1) + EPS)
        sm = sm_ref[bi]
        beta_all = _sigmoid(sm)
        sp_in = sm + dtb_ref[...]
        softplus = jnp.maximum(sp_in, 0.0) + jnp.log1p(jnp.exp(-jnp.abs(sp_in)))
        g_all = -jnp.exp(alog_ref[...]) * softplus
        g_t = g_all.T[0:BF16_ROWS]
        gam_t = _dot_x01(g_t, upper_bf, 3)
        glast_t = _dot_x01(g_t, same_chunk_bf, 3)
        pad = jnp.zeros((LANES - BF16_ROWS, n), F32)
        gam = jnp.concatenate([gam_t, pad], axis=0).T
        glast = jnp.concatenate([glast_t, pad], axis=0).T
        rows.append(dict(q_n=q_n, k_n=k_n, v_all=v_all, beta_all=beta_all, gam=gam, gam_t=gam_t,
                         e_gam=jnp.exp(gam), e_tot=jnp.exp(glast), e_rest_t=jnp.exp(glast_t - gam_t),
                         k_t=k_n.T))

    heads = [(bi, h) for bi in range(nb) for h in range(DN_HEADS)]
    q, k, v, bcol, egcol, etcol, lmat, qkd = {}, {}, {}, {}, {}, {}, {}, {}
    for key in heads:
        bi, h = key
        r = rows[bi]
        sl = slice(h * DN_DK, (h + 1) * DN_DK)
        lg = LANE_A + h
        q[key], k[key], v[key] = r["q_n"][:, sl], r["k_n"][:, sl], r["v_all"][:, sl]
        bcol[key] = r["beta_all"][:, LANE_BETA + h:LANE_BETA + h + 1]
        egcol[key] = r["e_gam"][:, lg:lg + 1]
        etcol[key] = r["e_tot"][:, lg:lg + 1]
        kt = r["k_t"][h * DN_DK:(h + 1) * DN_DK, :].astype(BF16)
        dec = jnp.exp(jnp.where(causal, r["gam"][:, lg:lg + 1] - r["gam_t"][lg:lg + 1, :], NEG))
        lmat[key] = jnp.where(strict, _dot((k[key] * bcol[key]).astype(BF16), kt) * dec, 0.0)
        qkd[key] = (_dot(q[key].astype(BF16), kt) * dec).astype(BF16)
    ps = {key: -to_rowform(lmat[key]) for key in heads}
    mp = {key: _dot((-ps[key]).astype(BF16), lmat[key].astype(BF16)) for key in heads}
    for s in range(nfac):
        mb = {key: mp[key].astype(BF16) for key in heads}
        mbd = {key: to_blockdiag(mb[key]) for key in heads}
        prod = {key: _dot(ps[key].astype(BF16), mbd[key]) for key in heads}
        ps = {key: ps[key] + mp[key] + prod[key] for key in heads}
        if s < nfac - 1:
            mp = {key: _dot(mb[key], mbd[key]) for key in heads}
    qeff, o_intra, b_in, n_in = {}, {}, {}, {}
    rhs = {key: jnp.concatenate([v[key] * bcol[key], k[key] * (bcol[key] * egcol[key])], axis=1) for key in heads}
    tx = {key: _dot(to_blockdiag(ps[key].astype(BF16)), rhs[key].astype(BF16)) for key in heads}
    x = {key: (rhs[key] + tx[key]).astype(BF16) for key in heads}
    qx = {key: _dot(qkd[key], x[key]) for key in heads}
    kdt = {(bi, h): (rows[bi]["k_t"][h * DN_DK:(h + 1) * DN_DK, :]
                     * rows[bi]["e_rest_t"][LANE_A + h:LANE_A + h + 1, :]).astype(BF16) for bi, h in heads}
    kx = {key: _dot(to_blockdiag(kdt[key]), x[key]) for key in heads}
    for key in heads:
        qeff[key] = (q[key] * egcol[key] - qx[key][:, DN_DV:]).astype(BF16)
        o_intra[key] = qx[key][:, :DN_DV]
        b_in[key], n_in[key] = kx[key][:, :DN_DV], kx[key][:, DN_DV:].astype(BF16)
    st = {(bi, h): state[bi * DN_HEADS + h] for bi, h in heads}
    o_chunks = {key: [] for key in heads}
    for c in range(nchunk):
        rs = slice(c * DN_CHUNK, (c + 1) * DN_CHUNK)
        ks = slice(c * DN_DK, (c + 1) * DN_DK)
        stb = {key: st[key].astype(BF16) for key in heads}
        corr = {key: _dot(n_in[key][ks], stb[key]) for key in heads}
        for key in heads:
            o_chunks[key].append(_dot(qeff[key][rs], stb[key]) + o_intra[key][rs])
            st[key] = st[key] * etcol[key][rs] + b_in[key][ks] - corr[key]
    for bi in range(nb):
        for h in range(DN_HEADS):
            state[bi * DN_HEADS + h] = st[(bi, h)]
        o = jnp.concatenate([jnp.concatenate(o_chunks[(bi, h)], axis=0) for h in range(DN_HEADS)], axis=1)
        ms = _dot_x01(o * o, head_ones, 1) * (1.0 / DN_DV)
        o = o * lax.rsqrt(ms + EPS) * nw_ref[...]
        out_ref[bi] = (o * _silu(z_ref[bi])).astype(out_ref.dtype)


def _deltanet(qkv, z, small, cw, alog_row, dtb_row, nw_row):
    b, s, _ = qkv.shape
    nb = DN_BATCH if b % DN_BATCH == 0 else 1
    const = lambda i, t: (0, 0)
    tile = lambda i, t: (i, t, 0)
    return pl.pallas_call(
        _deltanet_kernel,
        grid=(b // nb, s // SUPER),
        in_specs=[pl.BlockSpec((nb, SUPER, DN_QKV), tile),
                  pl.BlockSpec((nb, SUPER, DN_QK), tile),
                  pl.BlockSpec((nb, SUPER, LANES), tile),
                  pl.BlockSpec((DN_CONV, DN_QKV), const),
                  pl.BlockSpec((1, LANES), const),
                  pl.BlockSpec((1, LANES), const),
                  pl.BlockSpec((1, DN_QK), const)],
        out_specs=pl.BlockSpec((nb, SUPER, DN_QK), tile),
        out_shape=jax.ShapeDtypeStruct((b, s, DN_QK), BF16),
        scratch_shapes=[pltpu.VMEM((nb, SUPER + DN_PAD, DN_QKV), F32),
                        pltpu.VMEM((nb, SUPER, DN_QKV), F32),
                        pltpu.VMEM((nb * DN_HEADS, DN_DK, DN_DV), F32)],
        compiler_params=pltpu.CompilerParams(dimension_semantics=("parallel", "arbitrary"),
                                             vmem_limit_bytes=VMEM_LIMIT),
        name="deltanet",
    )(qkv, z, small, cw, alog_row, dtb_row, nw_row)


def _compress(t_ref, row, posa_ref, posb_ref, w1a_ref, w1b_ref, w2_ref):
    t = t_ref[row]
    nrow = t.shape[0]
    p = _dot((t + posa_ref[...]).astype(BF16), w1a_ref[...])
    q = _dot((t + posb_ref[...]).astype(BF16), w1b_ref[...])
    hid = _silu(p + pltpu.roll(q, nrow - 1, 0))
    return _dot(hid.astype(BF16), w2_ref[...])


def _store_vt(dst, row, v_ref):
    vt = v_ref[row].astype(F32).T
    ones = jnp.ones((BF16_ROWS, TQ), BF16)
    for g in range(NSA_KV):
        for c in range(dst.shape[1]):
            blk = vt[g * NSA_DH:(g + 1) * NSA_DH, c * TQ:(c + 1) * TQ].astype(BF16)
            dst[row * NSA_KV + g, c] = jnp.concatenate([blk, ones], axis=0)


def _attend_chunks(chunks, items, m_s, acc_s):
    scores = []
    for j, mask in chunks:
        row = []
        for _, q, key_chunk, _ in items:
            s = _dot(key_chunk(j), q)
            if mask is not None:
                s = jnp.where(mask, s, NEG)
            row.append(s.astype(BF16))
        scores.append(row)
    for (j, _), chunk_scores in zip(chunks, scores):
        probs, alphas = [], []
        for (slot, _, _, _), s in zip(items, chunk_scores):
            m_prev = m_s[slot]
            m_new = jnp.maximum(m_prev, jnp.max(s, axis=0, keepdims=True).astype(F32))
            probs.append(jnp.exp(s - m_new.astype(BF16)))
            alphas.append(jnp.exp(m_prev - m_new))
            m_s[slot] = m_new
        pv = [_dot(vt_chunk(j), p) for (_, _, _, vt_chunk), p in zip(items, probs)]
        for (slot, _, _, _), a, o in zip(items, alphas, pv):
            acc_s[slot] = a * acc_s[slot] + o


def _nsa_kernel(q_ref, sm_ref, kct_ref, vct_ref, ks_ref, vs_ref, kw_ref, vw_ref,
                kposa_ref, kposb_ref, kw1a_ref, kw1b_ref, kw2_ref,
                vposa_ref, vposb_ref, vw1a_ref, vw1b_ref, vw2_ref,
                out_ref,
                kc_s, vct_s, ksb, kwb, vst_s, vwt_s, qt_s, oc_s, m_s, acc_s):
    i = pl.program_id(1)
    nb = q_ref.shape[0]
    ncmp = kc_s.shape[1]
    nblk = ks_ref.shape[1] // SEL_BLOCK
    heads = [(bi, idx) for bi in range(nb) for idx in range(NSA_HEADS)]
    slot = lambda bi, idx: bi * NSA_HEADS + idx

    @pl.when(i == 0)
    def _():
        s_len = ks_ref.shape[1]
        key_blk = lax.broadcasted_iota(jnp.int32, (s_len, LANES), 0) // SEL_BLOCK
        blk_onehot = (lax.broadcasted_iota(jnp.int32, (s_len, LANES), 1) == key_blk).astype(BF16)
        for bi in range(nb):
            kc = _compress(kct_ref, bi, kposa_ref, kposb_ref, kw1a_ref, kw1b_ref, kw2_ref)
            kc_hi, kc_lo = _split_terms(kc, 2)
            kc_s[2 * bi] = kc_hi
            kc_s[2 * bi + 1] = kc_lo
            vct_s[bi] = _compress(vct_ref, bi, vposa_ref, vposb_ref, vw1a_ref, vw1b_ref, vw2_ref).T.astype(BF16)
            ksb[bi, :, 0:LANES] = ks_ref[bi]
            ksb[bi, :, LANES:] = blk_onehot
            kwb[bi] = kw_ref[bi]
            for idx in range(NSA_HEADS):
                qt_s[slot(bi, idx), LANES + nblk:, :] = jnp.zeros((LANES - nblk, TQ), BF16)
            _store_vt(vst_s, bi, vs_ref)
            _store_vt(vwt_s, bi, vw_ref)

    trow = i * TQ + lax.broadcasted_iota(jnp.int32, (1, TQ), 1)
    krel = lax.broadcasted_iota(jnp.int32, (TQ, 1), 0)
    cmp_end = lax.broadcasted_iota(jnp.int32, (ncmp, 1), 0) * CMP_STRIDE + (CMP_LEN - 1)
    valid_c = cmp_end <= trow
    oj = lax.broadcasted_iota(jnp.int32, (nblk, ncmp), 0) * SEL_BLOCK
    on = lax.broadcasted_iota(jnp.int32, (nblk, ncmp), 1) * CMP_STRIDE
    ov_t = ((on < oj + SEL_BLOCK) & (on + CMP_LEN > oj)).astype(BF16)
    jj = lax.broadcasted_iota(jnp.int32, (nblk, TQ), 0)
    cur = trow // SEL_BLOCK
    elig = jj <= cur
    forced = (jj == 0) | (jj == cur) | (jj == cur - 1)

    s_c = {}
    for bi in range(nb):
        qt_all = (q_ref[bi] * (NSA_DH ** -0.5)).T
        kc_hi, kc_lo = kc_s[2 * bi], kc_s[2 * bi + 1]
        zero_half = jnp.zeros((NSA_DH, TQ), F32)
        for idx in range(NSA_HEADS):
            qh = qt_all[idx * NSA_DH:(idx + 1) * NSA_DH]
            qt = jnp.concatenate([qh, zero_half] if idx // NSA_HPG == 0 else [zero_half, qh], axis=0)
            qt_hi, qt_lo = _split_terms(qt, 2)
            qt_s[slot(bi, idx), 0:LANES, :] = qt_hi
            s_c[(bi, idx)] = _dot(kc_hi, qt_hi) + _dot(kc_hi, qt_lo) + _dot(kc_lo, qt_hi)
    probs = {}
    for key in heads:
        s = jnp.where(valid_c, s_c[key], NEG)
        e = jnp.where(valid_c, jnp.exp(s - jnp.max(s, axis=0, keepdims=True)), 0.0)
        probs[key] = e * (1.0 / jnp.maximum(jnp.sum(e, axis=0, keepdims=True), 1e-30))
    for bi, idx in heads:
        g = idx // NSA_HPG
        oc_s[slot(bi, idx)] = _dot(vct_s[bi, g * NSA_DH:(g + 1) * NSA_DH, :], probs[(bi, idx)].astype(BF16))
    for bi in range(nb):
        for g in range(NSA_KV):
            psum = sum(probs[(bi, idx)] for idx in range(g * NSA_HPG, (g + 1) * NSA_HPG))
            imp_t = _dot_01x(ov_t, psum, 3)
            score = jnp.where(elig, imp_t + jnp.where(forced, SEL_BIG, 0.0), -SEL_BIG)
            rank = jnp.zeros((nblk, TQ), F32)
            for r in range(nblk):
                row = score[r:r + 1, :]
                beats = (row > score) | ((row == score) & (r < jj))
                rank = rank + beats.astype(F32)
            bias = jnp.where(rank < float(N_SEL), 0.0, NEG).astype(BF16)
            for hh in range(NSA_HPG):
                qt_s[slot(bi, g * NSA_HPG + hh), LANES:LANES + nblk, :] = bias

    m_s[...] = jnp.full(m_s.shape, NEG, BF16).astype(F32)
    acc_s[...] = jnp.zeros(acc_s.shape, F32)
    qrel = lax.broadcasted_iota(jnp.int32, (1, TQ), 1)
    on_or_below = krel <= qrel
    nslots = nb * NSA_HEADS

    sel_items = [(slot(bi, idx), qt_s[slot(bi, idx)],
                  lambda j, bi=bi: ksb[bi, pl.ds(pl.multiple_of(j * TQ, TQ), TQ), :],
                  lambda j, bi=bi, idx=idx: vst_s[bi * NSA_KV + idx // NSA_HPG, j]) for bi, idx in heads]

    def sel_pair(p, carry):
        _attend_chunks([(2 * p, None), (2 * p + 1, None)], sel_items, m_s, acc_s)
        return carry

    lax.fori_loop(0, i // 2, sel_pair, 0)

    @pl.when(i % 2 == 0)
    def _():
        _attend_chunks([(i, on_or_below)], sel_items, m_s, acc_s)

    @pl.when(i % 2 == 1)
    def _():
        _attend_chunks([(i - 1, None), (i, on_or_below)], sel_items, m_s, acc_s)

    win_items = [(nslots + slot(bi, idx), qt_s[slot(bi, idx), 0:LANES, :],
                  lambda j, bi=bi: kwb[bi, pl.ds(pl.multiple_of(j * TQ, TQ), TQ), :],
                  lambda j, bi=bi, idx=idx: vwt_s[bi * NSA_KV + idx // NSA_HPG, j]) for bi, idx in heads]

    @pl.when(i == 0)
    def _():
        _attend_chunks([(i, on_or_below)], win_items, m_s, acc_s)

    @pl.when(i >= 1)
    def _():
        _attend_chunks([(i, on_or_below), (i - 1, None)], win_items, m_s, acc_s)

    @pl.when(i >= 2)
    def _():
        _attend_chunks([(i - 2, krel > qrel)], win_items, m_s, acc_s)

    for bi in range(nb):
        gates_t = _sigmoid(sm_ref[bi]).T
        outs = []
        for idx in range(NSA_HEADS):
            acc = acc_s[slot(bi, idx)]
            o_s = acc[:NSA_DH] * (1.0 / acc[NSA_DH:NSA_DH + 1])
            acc = acc_s[nslots + slot(bi, idx)]
            o_w = acc[:NSA_DH] * (1.0 / acc[NSA_DH:NSA_DH + 1])
            lg = LANE_GATE + idx
            outs.append(gates_t[lg:lg + 1] * oc_s[slot(bi, idx)] + gates_t[lg + NSA_HEADS:lg + NSA_HEADS + 1] * o_s
                        + gates_t[lg + 2 * NSA_HEADS:lg + 2 * NSA_HEADS + 1] * o_w)
        out_ref[bi] = jnp.concatenate(outs, axis=0).T.astype(out_ref.dtype)


def _nsa(nq, small, kct, vct, ks, vs, kw, vw, kparams, vparams):
    b, s, _ = nq.shape
    nb = NSA_BATCH if b % NSA_BATCH == 0 else 1
    ncmp = kct.shape[1]
    nchunks = s // TQ
    tile = lambda bi, i: (bi, i, 0)
    full = lambda bi, i: (bi, 0, 0)
    const = lambda bi, i: (0, 0)
    wspecs = [pl.BlockSpec(p.shape, const, pipeline_mode=pl.Buffered(1)) for p in kparams + vparams]
    return pl.pallas_call(
        _nsa_kernel,
        grid=(b // nb, nchunks),
        in_specs=[pl.BlockSpec((nb, TQ, NSA_HEADS * NSA_DH), tile),
                  pl.BlockSpec((nb, TQ, LANES), tile),
                  pl.BlockSpec((nb, ncmp, kct.shape[2]), full),
                  pl.BlockSpec((nb, ncmp, vct.shape[2]), full),
                  pl.BlockSpec((nb, s, LANES), full),
                  pl.BlockSpec((nb, s, LANES), full),
                  pl.BlockSpec((nb, s, LANES), full),
                  pl.BlockSpec((nb, s, LANES), full)] + wspecs,
        out_specs=pl.BlockSpec((nb, TQ, NSA_HEADS * NSA_DH), tile),
        out_shape=jax.ShapeDtypeStruct((b, s, NSA_HEADS * NSA_DH), BF16),
        scratch_shapes=[pltpu.VMEM((2 * nb, ncmp, LANES), BF16),
                        pltpu.VMEM((nb, LANES, ncmp), BF16),
                        pltpu.VMEM((nb, s, 2 * LANES), BF16),
                        pltpu.VMEM((nb, s, LANES), BF16),
                        pltpu.VMEM((nb * NSA_KV, nchunks, VT_ROWS, TQ), BF16),
                        pltpu.VMEM((nb * NSA_KV, nchunks, VT_ROWS, TQ), BF16),
                        pltpu.VMEM((nb * NSA_HEADS, 2 * LANES, TQ), BF16),
                        pltpu.VMEM((nb * NSA_HEADS, NSA_DH, TQ), F32),
                        pltpu.VMEM((2 * nb * NSA_HEADS, 1, TQ), F32),
                        pltpu.VMEM((2 * nb * NSA_HEADS, VT_ROWS, TQ), F32)],
        compiler_params=pltpu.CompilerParams(dimension_semantics=("parallel", "arbitrary"),
                                             vmem_limit_bytes=VMEM_LIMIT),
        name="nsa",
    )(nq, small, kct, vct, ks, vs, kw, vw, *kparams, *vparams)


def _source_columns():
    o = 0
    qkv = list(range(o, o + DN_QKV)); o += DN_QKV
    z = list(range(o, o + DN_HEADS * DN_DV)); o += DN_HEADS * DN_DV
    b_raw = list(range(o, o + DN_HEADS)); o += DN_HEADS
    a_raw = list(range(o, o + DN_HEADS)); o += DN_HEADS
    cf_u = list(range(o, o + 2 * CF_CH)); o += 2 * CF_CH
    n_q = list(range(o, o + NSA_HEADS * NSA_DH)); o += NSA_HEADS * NSA_DH
    kv = []
    for _ in range(6):
        kv.append(list(range(o, o + NSA_KV * NSA_DH))); o += NSA_KV * NSA_DH
    gate = list(range(o, o + 3 * NSA_HEADS)); o += 3 * NSA_HEADS
    n_kc, n_vc, n_ks, n_vs, n_kw, n_vw = kv
    small = b_raw + a_raw + gate
    small = small + [-1] * (LANES - len(small))
    cols = qkv + z + small + cf_u + n_q + n_ks + n_vs + n_kw + n_vw + n_kc + n_vc
    assert len(cols) == sum(SEC_WIDTHS)
    return np.asarray(cols, np.int32), o


def _permute_w_in(w):
    cols, n_in = _source_columns()
    assert w.shape[1] == n_in
    select = (lax.broadcasted_iota(jnp.int32, (n_in, cols.shape[0]), 0) == cols[None, :]).astype(BF16)
    return jnp.dot(w.astype(BF16), select, preferred_element_type=BF16)


def _compress_params(pos, w1, w2):
    eye = jnp.eye(NSA_KV, dtype=F32)
    half = CMP_LEN // 2
    w1r = w1.reshape(CMP_LEN, NSA_DH, CMP_HID)

    def expand_w1(part):
        return jnp.einsum("ldj,gh->lgdhj", part, eye).reshape(half * NSA_KV * NSA_DH, NSA_KV * CMP_HID).astype(BF16)

    def expand_pos(part):
        return jnp.broadcast_to(part[:, None, :], (half, NSA_KV, NSA_DH)).reshape(1, half * NSA_KV * NSA_DH)

    w2e = jnp.einsum("jd,gh->gjhd", w2, eye).reshape(NSA_KV * CMP_HID, NSA_KV * NSA_DH).astype(BF16)
    return [expand_pos(pos[:half]), expand_pos(pos[half:]), expand_w1(w1r[:half]), expand_w1(w1r[half:]), w2e]


def _lane_row(vals, first_lane):
    pad = jnp.zeros((LANES - first_lane - vals.shape[0],), F32)
    return jnp.concatenate([jnp.zeros((first_lane,), F32), vals.astype(F32), pad])[None, :]


def kernel(x, norm_w, w_in, dn_conv_w, dn_a_log, dn_dt_bias, dn_norm_w, cf_dw_w, cf_dw_b, cf_ln_w, cf_ln_b,
           nsa_k_pos, nsa_v_pos, nsa_k_w1, nsa_k_w2, nsa_v_w1, nsa_v_w2, w_out, ffn_w_gate, ffn_w_up, ffn_w_down):
    b, s, d = x.shape
    depth = w_in.shape[0]
    assert s % SUPER == 0 and s % TQ == 0 and s // CMP_STRIDE == LANES and s // SEL_BLOCK <= LANES
    assert WINDOW == 2 * TQ and TQ % SEL_BLOCK == 0 and 2 * DN_HEADS <= BF16_ROWS
    m = b * s
    tm = 512 if m % 512 == 0 else 256
    x2 = x.reshape(m, d)
    ndn = DN_HEADS * DN_DV
    rows16 = s // CMP_STRIDE
    for l in range(depth):
        secs = _in_proj(x2, norm_w[l, 0:1], _permute_w_in(w_in[l]), cf_dw_w[l], cf_dw_b[l][None, :],
                        cf_ln_w[l][None, :], cf_ln_b[l][None, :], s, tm)
        qkv, z, small, o_cf, nq, n_ks, n_vs, n_kw, n_vw = [t.reshape(b, s, t.shape[1]) for t in secs[:N_ROWMAJOR]]
        n_kct, n_vct = [t.reshape(b, rows16, t.shape[1]) for t in secs[N_ROWMAJOR:]]
        o_dn = _deltanet(qkv, z, small, dn_conv_w[l], _lane_row(dn_a_log[l], LANE_A), _lane_row(dn_dt_bias[l], LANE_A),
                         jnp.tile(dn_norm_w[l], DN_HEADS)[None, :])
        o_nsa = _nsa(nq, small, n_kct, n_vct, n_ks, n_vs, n_kw, n_vw,
                     _compress_params(nsa_k_pos[l], nsa_k_w1[l], nsa_k_w2[l]),
                     _compress_params(nsa_v_pos[l], nsa_v_w1[l], nsa_v_w2[l]))
        x2 = _mix_ffn(x2, o_dn.reshape(m, ndn), o_cf.reshape(m, CF_CH), o_nsa.reshape(m, NSA_HEADS * NSA_DH),
                      w_out[l].astype(BF16), norm_w[l],
                      ffn_w_gate[l].astype(BF16), ffn_w_up[l].astype(BF16), ffn_w_down[l].astype(BF16), tm)
    return x2.reshape(b, s, d)
```

```python
import jax
import jax.numpy as jnp
import numpy as np
from jax import lax
from jax.experimental import pallas as pl
from jax.experimental.pallas import tpu as pltpu

F32 = jnp.float32
BF16 = jnp.bfloat16

DN_HEADS = 6
DN_DK = 64
DN_DV = 64
DN_CONV = 4
DN_CHUNK = 64
CF_CH = 256
CF_KERNEL = 31
NSA_HEADS = 6
NSA_KV = 2
NSA_HPG = NSA_HEADS // NSA_KV
NSA_DH = 64
CMP_LEN = 32
CMP_STRIDE = 16
CMP_HID = 2 * NSA_DH
SEL_BLOCK = 64
N_SEL = 8
WINDOW = 512
EPS = 1e-6
NEG = -1e30
SEL_BIG = 1e4

LANES = 128
SUBLANES = 8
BF16_ROWS = 16
DN_QK = DN_HEADS * DN_DK
DN_QKV = 2 * DN_QK + DN_HEADS * DN_DV
SUPER = 256
DN_BATCH = 2
NSA_BATCH = 2
TQ = 256
VT_ROWS = NSA_DH + BF16_ROWS
VMEM_LIMIT = 56 * 1024 * 1024

SEC_WIDTHS = (DN_QKV, DN_HEADS * DN_DV, LANES, 2 * CF_CH, NSA_HEADS * NSA_DH,
              LANES, LANES, LANES, LANES, LANES, LANES)
N_ROWMAJOR = 9
SEC_CF = 3
SEC_BF16 = (5, 6, 7, 8)
CF_ROWS = 64
CF_PAD = 32
LANE_BETA = 0
LANE_A = DN_HEADS
LANE_GATE = 2 * DN_HEADS


def _dot(a, b, precision=None):
    return jnp.dot(a, b, preferred_element_type=F32, precision=precision)


def _split_terms(x, terms):
    out = []
    for _ in range(terms):
        hi = x.astype(BF16)
        out.append(hi)
        x = x - hi.astype(F32)
    return out


def _dot_x01(x, mat01, terms):
    return sum(_dot(p, mat01) for p in _split_terms(x, terms))


def _dot_01x(mat01, x, terms):
    return sum(_dot(mat01, p) for p in _split_terms(x, terms))


def _sigmoid(x):
    return 1.0 / (1.0 + jnp.exp(-x))


def _silu(x):
    return x * _sigmoid(x)


def _rms(x, w):
    return x * lax.rsqrt(jnp.mean(x * x, axis=-1, keepdims=True) + EPS) * w


IN_PROJ_COLS = 1280


def _merged_sections():
    runs, first, width = [], 0, 0
    for k, wd in enumerate(SEC_WIDTHS):
        if width and width + wd > IN_PROJ_COLS:
            runs.append((first, k))
            first, width = k, 0
        width += wd
    runs.append((first, len(SEC_WIDTHS)))
    return runs


def _conformer_rows(r, dw_ref, db_ref, lnw_ref, lnb_ref, out_ref, hbuf):
    first = CF_PAD - (CF_KERNEL - 1)
    acc = jnp.zeros((CF_ROWS, CF_CH), F32) + db_ref[...]
    for phase in range(SUBLANES):
        part = None
        for j in range(CF_KERNEL):
            if (first + j) % SUBLANES != phase:
                continue
            base = r * CF_ROWS + first + j - phase
            rows = CF_ROWS + (SUBLANES if phase else 0)
            term = dw_ref[j:j + 1, :] * hbuf[pl.ds(base, rows), :]
            part = term if part is None else part + term
        if part is not None:
            acc = acc + part[phase:phase + CF_ROWS, :]
    mu = jnp.mean(acc, axis=-1, keepdims=True)
    cen = acc - mu
    var = jnp.mean(cen * cen, axis=-1, keepdims=True)
    y = cen * lax.rsqrt(var + EPS) * lnw_ref[...] + lnb_ref[...]
    out_ref[r * CF_ROWS:(r + 1) * CF_ROWS, :] = _silu(y).astype(out_ref.dtype)


def _in_proj_kernel(x_ref, nw_ref, w_ref, dw_ref, db_ref, lnw_ref, lnb_ref, *refs):
    out_refs, stage, hbuf = refs[:-2], refs[-2], refs[-1]
    tm = x_ref.shape[0]

    @pl.when(pl.program_id(1) == 0)
    def _():
        hbuf[0:CF_PAD, :] = jnp.zeros((CF_PAD, CF_CH), F32)

    h = _rms(x_ref[...], nw_ref[...]).astype(BF16)
    rows16 = tm // CMP_STRIDE

    def emit(first, last):
        off = sum(SEC_WIDTHS[:first])
        wide = _dot(h, w_ref[:, off:off + sum(SEC_WIDTHS[first:last])])
        col = 0
        for k in range(first, last):
            y = wide[:, col:col + SEC_WIDTHS[k]]
            col += SEC_WIDTHS[k]
            if k == SEC_CF:
                hbuf[CF_PAD:CF_PAD + tm, :] = y[:, :CF_CH] * _sigmoid(y[:, CF_CH:])
            elif k < N_ROWMAJOR:
                out_refs[k][...] = y.astype(out_refs[k].dtype)
            else:
                stage[...] = y
                for t in range(CMP_STRIDE):
                    out_refs[k][:, t * LANES:(t + 1) * LANES] = stage[pl.ds(t, rows16, stride=CMP_STRIDE), :]

    runs = sorted(_merged_sections(), key=lambda r: not (r[0] <= SEC_CF < r[1]))
    emit(*runs[0])
    nblocks = tm // CF_ROWS
    done = 0
    for n, run in enumerate(runs[1:], start=1):
        upto = nblocks * n // (len(runs) - 1)
        for r in range(done, upto):
            _conformer_rows(r, dw_ref, db_ref, lnw_ref, lnb_ref, out_refs[SEC_CF], hbuf)
        done = upto
        emit(*run)
    hbuf[0:CF_PAD, :] = hbuf[tm:tm + CF_PAD, :]


def _in_proj(x2, nw, w_perm, dw, db, lnw, lnb, seq, tm):
    m, d = x2.shape
    n = w_perm.shape[1]
    nt = seq // tm
    shapes, blocks, dtypes = [], [], []
    for k, wd in enumerate(SEC_WIDTHS):
        if k == SEC_CF:
            shapes.append((m, CF_CH)); blocks.append((tm, CF_CH)); dtypes.append(BF16)
        elif k < N_ROWMAJOR:
            shapes.append((m, wd)); blocks.append((tm, wd)); dtypes.append(BF16 if k in SEC_BF16 else F32)
        else:
            shapes.append((m // CMP_STRIDE, CMP_STRIDE * wd)); blocks.append((tm // CMP_STRIDE, CMP_STRIDE * wd))
            dtypes.append(F32)
    row = lambda bi, t: (bi * nt + t, 0)
    const = lambda bi, t: (0, 0)
    return pl.pallas_call(
        _in_proj_kernel,
        grid=(m // seq, nt),
        in_specs=[pl.BlockSpec((tm, d), row),
                  pl.BlockSpec((1, d), const),
                  pl.BlockSpec((d, n), const),
                  pl.BlockSpec((CF_KERNEL, CF_CH), const),
                  pl.BlockSpec((1, CF_CH), const),
                  pl.BlockSpec((1, CF_CH), const),
                  pl.BlockSpec((1, CF_CH), const)],
        out_specs=[pl.BlockSpec(blk, row) for blk in blocks],
        out_shape=[jax.ShapeDtypeStruct(shp, dt) for shp, dt in zip(shapes, dtypes)],
        scratch_shapes=[pltpu.VMEM((tm, LANES), F32),
                        pltpu.VMEM((tm + CF_PAD, CF_CH), F32)],
        compiler_params=pltpu.CompilerParams(dimension_semantics=("parallel", "arbitrary"),
                                             vmem_limit_bytes=VMEM_LIMIT),
        name="in_proj",
    )(x2, nw, w_perm, dw, db, lnw, lnb)


def _mix_ffn_kernel(x_ref, odn_ref, ocf_ref, onsa_ref, wo_ref, nw_ref, wg_ref, wu_ref, wd_ref, out_ref):
    mix = _dot(jnp.concatenate([odn_ref[...], ocf_ref[...], onsa_ref[...]], axis=1), wo_ref[...])
    x1 = x_ref[...] + _rms(mix, nw_ref[1:2, :])
    h = _rms(x1, nw_ref[2:3, :]).astype(BF16)
    g = _dot(h, wg_ref[...])
    u = _dot(h, wu_ref[...])
    a = (_silu(g) * u).astype(BF16)
    f = _dot(a, wd_ref[...])
    out_ref[...] = x1 + _rms(f, nw_ref[3:4, :])


def _mix_ffn(x2, odn, ocf, onsa, wo, nw, wg, wu, wd, tm):
    m, d = x2.shape
    dff = wg.shape[1]
    const = lambda i: (0, 0)
    row = lambda i: (i, 0)
    return pl.pallas_call(
        _mix_ffn_kernel,
        grid=(m // tm,),
        in_specs=[pl.BlockSpec((tm, d), row),
                  pl.BlockSpec((tm, odn.shape[1]), row),
                  pl.BlockSpec((tm, ocf.shape[1]), row),
                  pl.BlockSpec((tm, onsa.shape[1]), row),
                  pl.BlockSpec(wo.shape, const),
                  pl.BlockSpec(nw.shape, const),
                  pl.BlockSpec((d, dff), const),
                  pl.BlockSpec((d, dff), const),
                  pl.BlockSpec((dff, d), const)],
        out_specs=pl.BlockSpec((tm, d), row),
        out_shape=jax.ShapeDtypeStruct((m, d), F32),
        compiler_params=pltpu.CompilerParams(dimension_semantics=("parallel",), vmem_limit_bytes=VMEM_LIMIT),
        name="mix_ffn",
    )(x2, odn, ocf, onsa, wo, nw, wg, wu, wd)


DN_PAD = 8


def _deltanet_kernel(qkv_ref, z_ref, sm_ref, cw_ref, alog_ref, dtb_ref, nw_ref, out_ref, xbuf, ybuf, state):
    n = SUPER
    nchunk = n // DN_CHUNK
    nb = qkv_ref.shape[0]

    @pl.when(pl.program_id(1) == 0)
    def _():
        xbuf[:, 0:DN_PAD, :] = jnp.zeros((nb, DN_PAD, DN_QKV), F32)
        state[...] = jnp.zeros(state.shape, F32)

    hr = lax.broadcasted_iota(jnp.int32, (DN_QK, DN_QK), 0) // DN_DK
    hc = lax.broadcasted_iota(jnp.int32, (DN_QK, DN_QK), 1) // DN_DK
    head_ones = (hr == hc).astype(BF16)
    ri = lax.broadcasted_iota(jnp.int32, (n, n), 0)
    ci = lax.broadcasted_iota(jnp.int32, (n, n), 1)
    same_chunk = (ri // DN_CHUNK) == (ci // DN_CHUNK)
    causal = same_chunk & (ri >= ci)
    strict = same_chunk & (ri > ci)
    causal_bf = causal.astype(BF16)
    same_chunk_bf = same_chunk.astype(BF16)
    nfac = (DN_CHUNK - 1).bit_length() - 1
    lane_chunk = lax.broadcasted_iota(jnp.int32, (1, n), 1) // DN_CHUNK
    in_chunk = [lane_chunk == c for c in range(nchunk)]
    in_chunk_bf = [m.astype(BF16) for m in in_chunk]

    def to_rowform(bd):
        return sum(jnp.where(in_chunk[c], bd[c * DN_CHUNK:(c + 1) * DN_CHUNK, :], 0.0) for c in range(nchunk))

    def to_blockdiag(rowform):
        return jnp.concatenate([rowform * in_chunk_bf[c] for c in range(nchunk)], axis=0)

    first = DN_PAD - (DN_CONV - 1)
    rows = []
    for bi in range(nb):
        xbuf[bi, DN_PAD:DN_PAD + n, :] = qkv_ref[bi]
        y = cw_ref[DN_CONV - 1:DN_CONV, :] * xbuf[bi, pl.ds(DN_PAD, n), :]
        for j in range(DN_CONV - 1):
            y = y + cw_ref[j:j + 1, :] * xbuf[bi, pl.ds(first + j, n), :]
        xbuf[bi, 0:DN_PAD, :] = xbuf[bi, n:n + DN_PAD, :]
        ybuf[bi] = _silu(y)
        q_all = ybuf[bi, :, 0:DN_QK]
        k_all = ybuf[bi, :, DN_QK:2 * DN_QK]
        v_all = ybuf[bi, :, 2 * DN_QK:]
        q_n = q_all * lax.rsqrt(_dot_x01(q_all * q_all, head_ones, 1) + EPS) * (DN_DK ** -0.5)
        k_n = k_all * lax.rsqrt(_dot_x01(k_all * k_all, head_ones, 1) + EPS)
        sm = sm_ref[bi]
        beta_all = _sigmoid(sm)
        sp_in = sm + dtb_ref[...]
        softplus = jnp.maximum(sp_in, 0.0) + jnp.log1p(jnp.exp(-jnp.abs(sp_in)))
        g_all = -jnp.exp(alog_ref[...]) * softplus
        gam = _dot_01x(causal_bf, g_all, 3)
        glast = _dot_01x(same_chunk_bf, g_all, 3)
        rows.append(dict(q_n=q_n, k_n=k_n, v_all=v_all, beta_all=beta_all, gam=gam, gam_t=gam.T,
                         e_gam=jnp.exp(gam), e_tot=jnp.exp(glast), e_rest_t=jnp.exp(glast - gam).T,
                         k_t=k_n.T))

    heads = [(bi, h) for bi in range(nb) for h in range(DN_HEADS)]
    q, k, v, bcol, egcol, etcol, lmat, qkd = {}, {}, {}, {}, {}, {}, {}, {}
    for key in heads:
        bi, h = key
        r = rows[bi]
        sl = slice(h * DN_DK, (h + 1) * DN_DK)
        lg = LANE_A + h
        q[key], k[key], v[key] = r["q_n"][:, sl], r["k_n"][:, sl], r["v_all"][:, sl]
        bcol[key] = r["beta_all"][:, LANE_BETA + h:LANE_BETA + h + 1]
        egcol[key] = r["e_gam"][:, lg:lg + 1]
        etcol[key] = r["e_tot"][:, lg:lg + 1]
        kt = r["k_t"][h * DN_DK:(h + 1) * DN_DK, :].astype(BF16)
        dec = jnp.exp(jnp.where(causal, r["gam"][:, lg:lg + 1] - r["gam_t"][lg:lg + 1, :], NEG))
        lmat[key] = jnp.where(strict, _dot((k[key] * bcol[key]).astype(BF16), kt) * dec, 0.0)
        qkd[key] = (_dot(q[key].astype(BF16), kt) * dec).astype(BF16)
    ps = {key: -to_rowform(lmat[key]) for key in heads}
    mp = {key: _dot((-ps[key]).astype(BF16), lmat[key].astype(BF16)) for key in heads}
    for s in range(nfac):
        mb = {key: mp[key].astype(BF16) for key in heads}
        mbd = {key: to_blockdiag(mb[key]) for key in heads}
        prod = {key: _dot(ps[key].astype(BF16), mbd[key]) for key in heads}
        ps = {key: ps[key] + mp[key] + prod[key] for key in heads}
        if s < nfac - 1:
            mp = {key: _dot(mb[key], mbd[key]) for key in heads}
    qeff, o_intra, b_in, n_in = {}, {}, {}, {}
    rhs = {key: jnp.concatenate([v[key] * bcol[key], k[key] * (bcol[key] * egcol[key])], axis=1) for key in heads}
    tx = {key: _dot(to_blockdiag(ps[key].astype(BF16)), rhs[key].astype(BF16)) for key in heads}
    x = {key: (rhs[key] + tx[key]).astype(BF16) for key in heads}
    qx = {key: _dot(qkd[key], x[key]) for key in heads}
    kdt = {(bi, h): (rows[bi]["k_t"][h * DN_DK:(h + 1) * DN_DK, :]
                     * rows[bi]["e_rest_t"][LANE_A + h:LANE_A + h + 1, :]).astype(BF16) for bi, h in heads}
    kx = {key: _dot(to_blockdiag(kdt[key]), x[key]) for key in heads}
    for key in heads:
        qeff[key] = (q[key] * egcol[key] - qx[key][:, DN_DV:]).astype(BF16)
        o_intra[key] = qx[key][:, :DN_DV]
        b_in[key], n_in[key] = kx[key][:, :DN_DV], kx[key][:, DN_DV:].astype(BF16)
    st = {(bi, h): state[bi * DN_HEADS + h] for bi, h in heads}
    o_chunks = {key: [] for key in heads}
    for c in range(nchunk):
        rs = slice(c * DN_CHUNK, (c + 1) * DN_CHUNK)
        ks = slice(c * DN_DK, (c + 1) * DN_DK)
        stb = {key: st[key].astype(BF16) for key in heads}
        corr = {key: _dot(n_in[key][ks], stb[key]) for key in heads}
        for key in heads:
            o_chunks[key].append(_dot(qeff[key][rs], stb[key]) + o_intra[key][rs])
            st[key] = st[key] * etcol[key][rs] + b_in[key][ks] - corr[key]
    for bi in range(nb):
        for h in range(DN_HEADS):
            state[bi * DN_HEADS + h] = st[(bi, h)]
        o = jnp.concatenate([jnp.concatenate(o_chunks[(bi, h)], axis=0) for h in range(DN_HEADS)], axis=1)
        ms = _dot_x01(o * o, head_ones, 1) * (1.0 / DN_DV)
        o = o * lax.rsqrt(ms + EPS) * nw_ref[...]
        out_ref[bi] = (o * _silu(z_ref[bi])).astype(out_ref.dtype)


def _deltanet(qkv, z, small, cw, alog_row, dtb_row, nw_row):
    b, s, _ = qkv.shape
    nb = DN_BATCH if b % DN_BATCH == 0 else 1
    const = lambda i, t: (0, 0)
    tile = lambda i, t: (i, t, 0)
    return pl.pallas_call(
        _deltanet_kernel,
        grid=(b // nb, s // SUPER),
        in_specs=[pl.BlockSpec((nb, SUPER, DN_QKV), tile),
                  pl.BlockSpec((nb, SUPER, DN_QK), tile),
                  pl.BlockSpec((nb, SUPER, LANES), tile),
                  pl.BlockSpec((DN_CONV, DN_QKV), const),
                  pl.BlockSpec((1, LANES), const),
                  pl.BlockSpec((1, LANES), const),
                  pl.BlockSpec((1, DN_QK), const)],
        out_specs=pl.BlockSpec((nb, SUPER, DN_QK), tile),
        out_shape=jax.ShapeDtypeStruct((b, s, DN_QK), BF16),
        scratch_shapes=[pltpu.VMEM((nb, SUPER + DN_PAD, DN_QKV), F32),
                        pltpu.VMEM((nb, SUPER, DN_QKV), F32),
                        pltpu.VMEM((nb * DN_HEADS, DN_DK, DN_DV), F32)],
        compiler_params=pltpu.CompilerParams(dimension_semantics=("parallel", "arbitrary"),
                                             vmem_limit_bytes=VMEM_LIMIT),
        name="deltanet",
    )(qkv, z, small, cw, alog_row, dtb_row, nw_row)


def _compress(t_ref, row, posa_ref, posb_ref, w1a_ref, w1b_ref, w2_ref):
    t = t_ref[row]
    nrow = t.shape[0]
    p = _dot((t + posa_ref[...]).astype(BF16), w1a_ref[...])
    q = _dot((t + posb_ref[...]).astype(BF16), w1b_ref[...])
    hid = _silu(p + pltpu.roll(q, nrow - 1, 0))
    return _dot(hid.astype(BF16), w2_ref[...])


def _store_vt(dst, row, v_ref):
    vt = v_ref[row].astype(F32).T
    ones = jnp.ones((BF16_ROWS, TQ), BF16)
    for g in range(NSA_KV):
        for c in range(dst.shape[1]):
            blk = vt[g * NSA_DH:(g + 1) * NSA_DH, c * TQ:(c + 1) * TQ].astype(BF16)
            dst[row * NSA_KV + g, c] = jnp.concatenate([blk, ones], axis=0)


def _attend_chunks(chunks, items, m_s, acc_s):
    scores = []
    for j, mask in chunks:
        row = []
        for _, q, key_chunk, _ in items:
            s = _dot(key_chunk(j), q)
            if mask is not None:
                s = jnp.where(mask, s, NEG)
            row.append(s.astype(BF16))
        scores.append(row)
    for (j, _), chunk_scores in zip(chunks, scores):
        probs, alphas = [], []
        for (slot, _, _, _), s in zip(items, chunk_scores):
            m_prev = m_s[slot]
            m_new = jnp.maximum(m_prev, jnp.max(s, axis=0, keepdims=True).astype(F32))
            probs.append(jnp.exp(s - m_new.astype(BF16)))
            alphas.append(jnp.exp(m_prev - m_new))
            m_s[slot] = m_new
        pv = [_dot(vt_chunk(j), p) for (_, _, _, vt_chunk), p in zip(items, probs)]
        for (slot, _, _, _), a, o in zip(items, alphas, pv):
            acc_s[slot] = a * acc_s[slot] + o


def _nsa_kernel(q_ref, sm_ref, kct_ref, vct_ref, ks_ref, vs_ref, kw_ref, vw_ref,
                kposa_ref, kposb_ref, kw1a_ref, kw1b_ref, kw2_ref,
                vposa_ref, vposb_ref, vw1a_ref, vw1b_ref, vw2_ref,
                out_ref,
                kc_s, vct_s, ksb, kwb, vst_s, vwt_s, qt_s, oc_s, score_s, m_s, acc_s):
    i = pl.program_id(1)
    nb = q_ref.shape[0]
    ncmp = kc_s.shape[1]
    nblk = ks_ref.shape[1] // SEL_BLOCK
    heads = [(bi, idx) for bi in range(nb) for idx in range(NSA_HEADS)]
    slot = lambda bi, idx: bi * NSA_HEADS + idx

    @pl.when(i == 0)
    def _():
        s_len = ks_ref.shape[1]
        key_blk = lax.broadcasted_iota(jnp.int32, (s_len, LANES), 0) // SEL_BLOCK
        blk_onehot = (lax.broadcasted_iota(jnp.int32, (s_len, LANES), 1) == key_blk).astype(BF16)
        for bi in range(nb):
            kc = _compress(kct_ref, bi, kposa_ref, kposb_ref, kw1a_ref, kw1b_ref, kw2_ref)
            kc_hi, kc_lo = _split_terms(kc, 2)
            kc_s[2 * bi] = kc_hi
            kc_s[2 * bi + 1] = kc_lo
            vct_s[bi] = _compress(vct_ref, bi, vposa_ref, vposb_ref, vw1a_ref, vw1b_ref, vw2_ref).T.astype(BF16)
            ksb[bi, :, 0:LANES] = ks_ref[bi]
            ksb[bi, :, LANES:] = blk_onehot
            kwb[bi] = kw_ref[bi]
            for idx in range(NSA_HEADS):
                qt_s[slot(bi, idx), LANES + nblk:, :] = jnp.zeros((LANES - nblk, TQ), BF16)
            _store_vt(vst_s, bi, vs_ref)
            _store_vt(vwt_s, bi, vw_ref)

    trow = i * TQ + lax.broadcasted_iota(jnp.int32, (1, TQ), 1)
    krel = lax.broadcasted_iota(jnp.int32, (TQ, 1), 0)
    cmp_end = lax.broadcasted_iota(jnp.int32, (ncmp, 1), 0) * CMP_STRIDE + (CMP_LEN - 1)
    valid_c = cmp_end <= trow
    oj = lax.broadcasted_iota(jnp.int32, (nblk, ncmp), 0) * SEL_BLOCK
    on = lax.broadcasted_iota(jnp.int32, (nblk, ncmp), 1) * CMP_STRIDE
    ov_t = ((on < oj + SEL_BLOCK) & (on + CMP_LEN > oj)).astype(BF16)
    jj = lax.broadcasted_iota(jnp.int32, (nblk, TQ), 0)
    cur = trow // SEL_BLOCK
    elig = jj <= cur
    forced = (jj == 0) | (jj == cur) | (jj == cur - 1)

    s_c = {}
    for bi in range(nb):
        qt_all = (q_ref[bi] * (NSA_DH ** -0.5)).T
        kc_hi, kc_lo = kc_s[2 * bi], kc_s[2 * bi + 1]
        zero_half = jnp.zeros((NSA_DH, TQ), F32)
        for idx in range(NSA_HEADS):
            qh = qt_all[idx * NSA_DH:(idx + 1) * NSA_DH]
            qt = jnp.concatenate([qh, zero_half] if idx // NSA_HPG == 0 else [zero_half, qh], axis=0)
            qt_hi, qt_lo = _split_terms(qt, 2)
            qt_s[slot(bi, idx), 0:LANES, :] = qt_hi
            s_c[(bi, idx)] = _dot(kc_hi, qt_hi) + _dot(kc_hi, qt_lo) + _dot(kc_lo, qt_hi)
    probs = {}
    for key in heads:
        s = jnp.where(valid_c, s_c[key], NEG)
        e = jnp.where(valid_c, jnp.exp(s - jnp.max(s, axis=0, keepdims=True)), 0.0)
        probs[key] = e * (1.0 / jnp.maximum(jnp.sum(e, axis=0, keepdims=True), 1e-30))
    for bi, idx in heads:
        g = idx // NSA_HPG
        oc_s[slot(bi, idx)] = _dot(vct_s[bi, g * NSA_DH:(g + 1) * NSA_DH, :], probs[(bi, idx)].astype(BF16))
    groups = [(bi, g) for bi in range(nb) for g in range(NSA_KV)]
    scores = []
    for n, (bi, g) in enumerate(groups):
        psum = sum(probs[(bi, idx)] for idx in range(g * NSA_HPG, (g + 1) * NSA_HPG))
        imp_t = _dot_01x(ov_t, psum, 3)
        score = jnp.where(elig, imp_t + jnp.where(forced, SEL_BIG, 0.0), -SEL_BIG)
        scores.append(score)
        for r in range(nblk):
            score_s[n * nblk + r] = score[r:r + 1, :]

    blk_per_chunk = TQ // SEL_BLOCK

    def rank_rows(jb, ranks):
        ranks = list(ranks)
        for rr in range(blk_per_chunk):
            r = jb * blk_per_chunk + rr
            for n in range(len(groups)):
                row = score_s[n * nblk + r]
                beats = (row > scores[n]) | ((row == scores[n]) & (r < jj))
                ranks[n] = ranks[n] + beats.astype(F32)
        return tuple(ranks)

    ranks = lax.fori_loop(0, i + 1, rank_rows, tuple(jnp.zeros((nblk, TQ), F32) for _ in groups))
    for n, (bi, g) in enumerate(groups):
        bias = jnp.where(ranks[n] < float(N_SEL), 0.0, NEG).astype(BF16)
        for hh in range(NSA_HPG):
            qt_s[slot(bi, g * NSA_HPG + hh), LANES:LANES + nblk, :] = bias

    m_s[...] = jnp.full(m_s.shape, NEG, BF16).astype(F32)
    acc_s[...] = jnp.zeros(acc_s.shape, F32)
    qrel = lax.broadcasted_iota(jnp.int32, (1, TQ), 1)
    on_or_below = krel <= qrel
    nslots = nb * NSA_HEADS

    sel_items = [(slot(bi, idx), qt_s[slot(bi, idx)],
                  lambda j, bi=bi: ksb[bi, pl.ds(pl.multiple_of(j * TQ, TQ), TQ), :],
                  lambda j, bi=bi, idx=idx: vst_s[bi * NSA_KV + idx // NSA_HPG, j]) for bi, idx in heads]

    def sel_pair(p, carry):
        _attend_chunks([(2 * p, None), (2 * p + 1, None)], sel_items, m_s, acc_s)
        return carry

    lax.fori_loop(0, i // 2, sel_pair, 0)

    @pl.when(i % 2 == 0)
    def _():
        _attend_chunks([(i, on_or_below)], sel_items, m_s, acc_s)

    @pl.when(i % 2 == 1)
    def _():
        _attend_chunks([(i - 1, None), (i, on_or_below)], sel_items, m_s, acc_s)

    win_items = [(nslots + slot(bi, idx), qt_s[slot(bi, idx), 0:LANES, :],
                  lambda j, bi=bi: kwb[bi, pl.ds(pl.multiple_of(j * TQ, TQ), TQ), :],
                  lambda j, bi=bi, idx=idx: vwt_s[bi * NSA_KV + idx // NSA_HPG, j]) for bi, idx in heads]

    @pl.when(i == 0)
    def _():
        _attend_chunks([(i, on_or_below)], win_items, m_s, acc_s)

    @pl.when(i >= 1)
    def _():
        _attend_chunks([(i, on_or_below), (i - 1, None)], win_items, m_s, acc_s)

    @pl.when(i >= 2)
    def _():
        _attend_chunks([(i - 2, krel > qrel)], win_items, m_s, acc_s)

    for bi in range(nb):
        gates_t = _sigmoid(sm_ref[bi]).T
        outs = []
        for idx in range(NSA_HEADS):
            acc = acc_s[slot(bi, idx)]
            o_s = acc[:NSA_DH] * (1.0 / acc[NSA_DH:NSA_DH + 1])
            acc = acc_s[nslots + slot(bi, idx)]
            o_w = acc[:NSA_DH] * (1.0 / acc[NSA_DH:NSA_DH + 1])
            lg = LANE_GATE + idx
            outs.append(gates_t[lg:lg + 1] * oc_s[slot(bi, idx)] + gates_t[lg + NSA_HEADS:lg + NSA_HEADS + 1] * o_s
                        + gates_t[lg + 2 * NSA_HEADS:lg + 2 * NSA_HEADS + 1] * o_w)
        out_ref[bi] = jnp.concatenate(outs, axis=0).T.astype(out_ref.dtype)


def _nsa(nq, small, kct, vct, ks, vs, kw, vw, kparams, vparams):
    b, s, _ = nq.shape
    nb = NSA_BATCH if b % NSA_BATCH == 0 else 1
    ncmp = kct.shape[1]
    nchunks = s // TQ
    tile = lambda bi, i: (bi, i, 0)
    full = lambda bi, i: (bi, 0, 0)
    const = lambda bi, i: (0, 0)
    wspecs = [pl.BlockSpec(p.shape, const, pipeline_mode=pl.Buffered(1)) for p in kparams + vparams]
    return pl.pallas_call(
        _nsa_kernel,
        grid=(b // nb, nchunks),
        in_specs=[pl.BlockSpec((nb, TQ, NSA_HEADS * NSA_DH), tile),
                  pl.BlockSpec((nb, TQ, LANES), tile),
                  pl.BlockSpec((nb, ncmp, kct.shape[2]), full),
                  pl.BlockSpec((nb, ncmp, vct.shape[2]), full),
                  pl.BlockSpec((nb, s, LANES), full),
                  pl.BlockSpec((nb, s, LANES), full),
                  pl.BlockSpec((nb, s, LANES), full),
                  pl.BlockSpec((nb, s, LANES), full)] + wspecs,
        out_specs=pl.BlockSpec((nb, TQ, NSA_HEADS * NSA_DH), tile),
        out_shape=jax.ShapeDtypeStruct((b, s, NSA_HEADS * NSA_DH), BF16),
        scratch_shapes=[pltpu.VMEM((2 * nb, ncmp, LANES), BF16),
                        pltpu.VMEM((nb, LANES, ncmp), BF16),
                        pltpu.VMEM((nb, s, 2 * LANES), BF16),
                        pltpu.VMEM((nb, s, LANES), BF16),
                        pltpu.VMEM((nb * NSA_KV, nchunks, VT_ROWS, TQ), BF16),
                        pltpu.VMEM((nb * NSA_KV, nchunks, VT_ROWS, TQ), BF16),
                        pltpu.VMEM((nb * NSA_HEADS, 2 * LANES, TQ), BF16),
                        pltpu.VMEM((nb * NSA_HEADS, NSA_DH, TQ), F32),
                        pltpu.VMEM((nb * NSA_KV * (s // SEL_BLOCK), 1, TQ), F32),
                        pltpu.VMEM((2 * nb * NSA_HEADS, 1, TQ), F32),
                        pltpu.VMEM((2 * nb * NSA_HEADS, VT_ROWS, TQ), F32)],
        compiler_params=pltpu.CompilerParams(dimension_semantics=("parallel", "arbitrary"),
                                             vmem_limit_bytes=VMEM_LIMIT),
        name="nsa",
    )(nq, small, kct, vct, ks, vs, kw, vw, *kparams, *vparams)


def _source_columns():
    o = 0
    qkv = list(range(o, o + DN_QKV)); o += DN_QKV
    z = list(range(o, o + DN_HEADS * DN_DV)); o += DN_HEADS * DN_DV
    b_raw = list(range(o, o + DN_HEADS)); o += DN_HEADS
    a_raw = list(range(o, o + DN_HEADS)); o += DN_HEADS
    cf_u = list(range(o, o + 2 * CF_CH)); o += 2 * CF_CH
    n_q = list(range(o, o + NSA_HEADS * NSA_DH)); o += NSA_HEADS * NSA_DH
    kv = []
    for _ in range(6):
        kv.append(list(range(o, o + NSA_KV * NSA_DH))); o += NSA_KV * NSA_DH
    gate = list(range(o, o + 3 * NSA_HEADS)); o += 3 * NSA_HEADS
    n_kc, n_vc, n_ks, n_vs, n_kw, n_vw = kv
    small = b_raw + a_raw + gate
    small = small + [-1] * (LANES - len(small))
    cols = qkv + z + small + cf_u + n_q + n_ks + n_vs + n_kw + n_vw + n_kc + n_vc
    assert len(cols) == sum(SEC_WIDTHS)
    return np.asarray(cols, np.int32), o


def _permute_w_in(w):
    cols, n_in = _source_columns()
    assert w.shape[1] == n_in
    select = (lax.broadcasted_iota(jnp.int32, (n_in, cols.shape[0]), 0) == cols[None, :]).astype(BF16)
    return jnp.dot(w.astype(BF16), select, preferred_element_type=BF16)


def _compress_params(pos, w1, w2):
    eye = jnp.eye(NSA_KV, dtype=F32)
    half = CMP_LEN // 2
    w1r = w1.reshape(CMP_LEN, NSA_DH, CMP_HID)

    def expand_w1(part):
        return jnp.einsum("ldj,gh->lgdhj", part, eye).reshape(half * NSA_KV * NSA_DH, NSA_KV * CMP_HID).astype(BF16)

    def expand_pos(part):
        return jnp.broadcast_to(part[:, None, :], (half, NSA_KV, NSA_DH)).reshape(1, half * NSA_KV * NSA_DH)

    w2e = jnp.einsum("jd,gh->gjhd", w2, eye).reshape(NSA_KV * CMP_HID, NSA_KV * NSA_DH).astype(BF16)
    return [expand_pos(pos[:half]), expand_pos(pos[half:]), expand_w1(w1r[:half]), expand_w1(w1r[half:]), w2e]


def _lane_row(vals, first_lane):
    pad = jnp.zeros((LANES - first_lane - vals.shape[0],), F32)
    return jnp.concatenate([jnp.zeros((first_lane,), F32), vals.astype(F32), pad])[None, :]


def kernel(x, norm_w, w_in, dn_conv_w, dn_a_log, dn_dt_bias, dn_norm_w, cf_dw_w, cf_dw_b, cf_ln_w, cf_ln_b,
           nsa_k_pos, nsa_v_pos, nsa_k_w1, nsa_k_w2, nsa_v_w1, nsa_v_w2, w_out, ffn_w_gate, ffn_w_up, ffn_w_down):
    b, s, d = x.shape
    depth = w_in.shape[0]
    assert s % SUPER == 0 and s % TQ == 0 and s // CMP_STRIDE == LANES and s // SEL_BLOCK <= LANES
    assert WINDOW == 2 * TQ and TQ % SEL_BLOCK == 0
    m = b * s
    tm = 512 if m % 512 == 0 else 256
    x2 = x.reshape(m, d)
    ndn = DN_HEADS * DN_DV
    rows16 = s // CMP_STRIDE
    for l in range(depth):
        secs = _in_proj(x2, norm_w[l, 0:1], _permute_w_in(w_in[l]), cf_dw_w[l], cf_dw_b[l][None, :],
                        cf_ln_w[l][None, :], cf_ln_b[l][None, :], s, tm)
        qkv, z, small, o_cf, nq, n_ks, n_vs, n_kw, n_vw = [t.reshape(b, s, t.shape[1]) for t in secs[:N_ROWMAJOR]]
        n_kct, n_vct = [t.reshape(b, rows16, t.shape[1]) for t in secs[N_ROWMAJOR:]]
        o_dn = _deltanet(qkv, z, small, dn_conv_w[l], _lane_row(dn_a_log[l], LANE_A), _lane_row(dn_dt_bias[l], LANE_A),
                         jnp.tile(dn_norm_w[l], DN_HEADS)[None, :])
        o_nsa = _nsa(nq, small, n_kct, n_vct, n_ks, n_vs, n_kw, n_vw,
                     _compress_params(nsa_k_pos[l], nsa_k_w1[l], nsa_k_w2[l]),
                     _compress_params(nsa_v_pos[l], nsa_v_w1[l], nsa_v_w2[l]))
        x2 = _mix_ffn(x2, o_dn.reshape(m, ndn), o_cf.reshape(m, CF_CH), o_nsa.reshape(m, NSA_HEADS * NSA_DH),
                      w_out[l].astype(BF16), norm_w[l],
                      ffn_w_gate[l].astype(BF16), ffn_w_up[l].astype(BF16), ffn_w_down[l].astype(BF16), tm)
    return x2.reshape(b, s, d)
```
